```python
import math, functools
import jax, jax.numpy as jnp
from jax import lax
import numpy as np

D_MODEL = 2048
BATCH = 32
SEQ = 256
DEPTH = 2
DEC_BATCH = 2
DEC_SEQ = 4096
PAST_LEN = 256

GRID_W = 64
HEAD_DIM = 128
A_HEADS = 8
A_KV_HEADS = 2
A_GROUP = A_HEADS // A_KV_HEADS
B_HEADS = 4
B_V_DIM = 2 * HEAD_DIM
ATTN_SPLITS = [A_HEADS * HEAD_DIM, A_KV_HEADS * HEAD_DIM, A_KV_HEADS * HEAD_DIM,
               B_HEADS * 2 * HEAD_DIM, B_HEADS * 2 * HEAD_DIM, B_HEADS * B_V_DIM]
ATTN_IN = sum(ATTN_SPLITS)
Q_BLOCK = 128
ROPE_BASE = 10000.0
ROPE_PAIRS_PER_AXIS = HEAD_DIM // 4
S5_GROUP_CH = 16
S5_GROUPS = D_MODEL // S5_GROUP_CH
S5_STATE = 64
D_FF = 4 * D_MODEL
N_ATTN_LAYERS = (DEPTH + 1) // 2
N_SSM_LAYERS = DEPTH // 2
N_MOD = 6
EPS = 1e-6
F32 = jnp.float32

kernel_name = "hybrid_diffusion_gqa_diffattn_s5_step"


def _rms(x, g):
    xf = x.astype(F32)
    y = xf * lax.rsqrt(jnp.mean(xf * xf, axis=-1, keepdims=True) + EPS)
    return (y * g.astype(F32)).astype(x.dtype)


def _modulation(cond, ada_w, ada_b):
    m = jax.nn.silu(cond) @ ada_w + ada_b
    return [t[:, None, :] for t in jnp.split(m, N_MOD, axis=-1)]


def _modulate(x, g, shift, scale):
    return _rms(x, g) * (1.0 + scale) + shift


def _mlp(n, w1, w2):
    h = jax.nn.relu(n @ w1)
    return (h * h) @ w2


def _axial_rope(L):
    rows = L // GRID_W
    row = jnp.repeat(jnp.arange(rows, dtype=F32), GRID_W)
    col = jnp.tile(jnp.arange(GRID_W, dtype=F32), rows)
    inv = ROPE_BASE ** (-jnp.arange(ROPE_PAIRS_PER_AXIS, dtype=F32) / ROPE_PAIRS_PER_AXIS)
    ang = jnp.concatenate([row[:, None] * inv, col[:, None] * inv], axis=-1)
    return jnp.cos(ang), jnp.sin(ang)


def _apply_rope(x, rope):
    cos, sin = rope
    shp = (1, x.shape[1]) + (1,) * (x.ndim - 3) + (HEAD_DIM // 2,)
    cos, sin = cos.reshape(shp), sin.reshape(shp)
    xf = x.astype(F32)
    x1, x2 = xf[..., :HEAD_DIM // 2], xf[..., HEAD_DIM // 2:]
    return jnp.concatenate([x1 * cos - x2 * sin, x1 * sin + x2 * cos], axis=-1).astype(x.dtype)


def _attn_qkv(n, w_in, qk_g, rope):
    B, L, _ = n.shape
    idx = [int(i) for i in np.cumsum(ATTN_SPLITS)[:-1]]
    qa, ka, va, qb, kb, vb = jnp.split(n @ w_in, idx, axis=-1)
    qa = _rms(qa.reshape(B, L, A_HEADS, HEAD_DIM), qk_g[0])
    ka = _rms(ka.reshape(B, L, A_KV_HEADS, HEAD_DIM), qk_g[1])
    va = va.reshape(B, L, A_KV_HEADS, HEAD_DIM)
    qb = qb.reshape(B, L, B_HEADS, 2, HEAD_DIM)
    kb = kb.reshape(B, L, B_HEADS, 2, HEAD_DIM)
    vb = vb.reshape(B, L, B_HEADS, B_V_DIM)
    if rope is not None:
        qa, ka, qb, kb = (_apply_rope(t, rope) for t in (qa, ka, qb, kb))
    return qa, ka, va, qb, kb, vb


def _gqa_block(q, k, v):
    B, Q = q.shape[:2]
    qg = q.reshape(B, Q, A_KV_HEADS, A_GROUP, HEAD_DIM)
    s = jnp.einsum("bqhgd,bkhd->bhgqk", qg, k).astype(F32) * (HEAD_DIM ** -0.5)
    p = jax.nn.softmax(s, axis=-1).astype(v.dtype)
    o = jnp.einsum("bhgqk,bkhd->bqhgd", p, v)
    return o.reshape(B, Q, A_HEADS * HEAD_DIM)


def _diff_block(q, k, v, lam):
    s = jnp.einsum("bqhmd,bkhmd->bhmqk", q, k).astype(F32) * (HEAD_DIM ** -0.5)
    p = jax.nn.softmax(s, axis=-1)
    w = (p[:, :, 0] - lam * p[:, :, 1]).astype(v.dtype)
    return jnp.einsum("bhqk,bkhe->bqhe", w, v)


def _sweep(fn, q):
    B, L = q.shape[:2]
    nb = L // Q_BLOCK
    qb = jnp.moveaxis(q.reshape((B, nb, Q_BLOCK) + q.shape[2:]), 1, 0)
    o = lax.map(fn, qb)
    return jnp.moveaxis(o, 0, 1).reshape((B, L) + o.shape[3:])


def _attn_mix(qa, ka, va, qb, kb, vb, lam_p, lam_init, subln_g, w_out):
    lp = lam_p.astype(F32)
    lam = jnp.exp(jnp.sum(lp[0] * lp[1])) - jnp.exp(jnp.sum(lp[2] * lp[3])) + lam_init
    oa = _sweep(lambda q: _gqa_block(q, ka, va), qa)
    ob = _sweep(lambda q: _diff_block(q, kb, vb, lam), qb)
    ob = _rms(ob, subln_g) * (1.0 - lam_init)
    B, L = oa.shape[:2]
    o = jnp.concatenate([oa, ob.reshape(B, L, B_HEADS * B_V_DIM)], axis=-1)
    return o @ w_out


def _ssm_combine(x, y):
    a1, b1 = x
    a2, b2 = y
    return a1 * a2, a2 * b1 + b2


def _s5_scan(u, a_re, a_im, log_dt, b_re, b_im, c_re, c_im, h0):
    Bsz, L, _ = u.shape
    lam = lax.complex(a_re.astype(F32), a_im.astype(F32))
    dt = jnp.exp(log_dt.astype(F32))[:, None]
    a_bar = jnp.exp(lam * dt)
    b_bar = ((a_bar - 1.0) / lam)[..., None] * lax.complex(b_re.astype(F32), b_im.astype(F32))
    ug = u.astype(F32).reshape(Bsz, L, S5_GROUPS, S5_GROUP_CH).astype(jnp.complex64)
    bu = jnp.einsum("gnc,blgc->blgn", b_bar, ug)
    if h0 is not None:
        bu = bu.at[:, 0].add(a_bar * h0)
    a_seq = jnp.broadcast_to(a_bar, bu.shape)
    _, h = lax.associative_scan(_ssm_combine, (a_seq, bu), axis=1)
    cc = lax.complex(c_re.astype(F32), c_im.astype(F32))
    y = jnp.einsum("gcn,blgn->blgc", cc, h).real
    return y.reshape(Bsz, L, D_MODEL), h[:, -1]


def _ssm_mix(n, sp, h0):
    w_in, a_re, a_im, log_dt, b_re, b_im, c_re, c_im, d, glu_w, w_out = sp
    u = n @ w_in
    per_dir = lambda k: (a_re[k], a_im[k], log_dt[k], b_re[k], b_im[k], c_re[k], c_im[k])
    y_f, h_f = _s5_scan(u, *per_dir(0), None if h0 is None else h0[:, 0])
    y_b, h_b = _s5_scan(u[:, ::-1], *per_dir(1), None if h0 is None else h0[:, 1])
    y = y_f + y_b[:, ::-1] + d.astype(F32) * u.astype(F32)
    z = jax.nn.gelu(y).astype(n.dtype)
    z = z * jax.nn.sigmoid(z @ glu_w)
    return z @ w_out, jnp.stack([h_f, h_b], axis=1)


def setup_inputs(seed: int = 0) -> dict:
    key = jax.random.key(seed)
    ks = iter(jax.random.split(key, 40))
    nrm = lambda shape, s: jax.random.normal(next(ks), shape, F32) * s
    D = D_MODEL
    G, N, C = S5_GROUPS, S5_STATE, S5_GROUP_CH
    return {
        "x_prompt": nrm((BATCH, SEQ, D), 1.0),
        "x_sample": nrm((DEC_BATCH, DEC_SEQ, D), 1.0),
        "c": nrm((DEC_BATCH, D), 1.0),
        "cache_a_k": nrm((DEC_BATCH, N_ATTN_LAYERS, PAST_LEN, A_KV_HEADS, HEAD_DIM), 1.0),
        "cache_a_v": nrm((DEC_BATCH, N_ATTN_LAYERS, PAST_LEN, A_KV_HEADS, HEAD_DIM), 1.0),
        "cache_b_k": nrm((DEC_BATCH, N_ATTN_LAYERS, PAST_LEN, B_HEADS, 2, HEAD_DIM), 1.0),
        "cache_b_v": nrm((DEC_BATCH, N_ATTN_LAYERS, PAST_LEN, B_HEADS, B_V_DIM), 1.0),
        "state_ssm": nrm((DEC_BATCH, N_SSM_LAYERS, 2, G, N, 2), 0.3),
        "c_ctx": nrm((D,), 1.0),
        "ada_w": nrm((DEPTH, D, N_MOD * D), D ** -0.5),
        "ada_b": nrm((DEPTH, N_MOD * D), 0.02),
        "norm_g": 1.0 + nrm((DEPTH, 4, D), 0.02),
        "mlp_w1": nrm((DEPTH, D, D_FF), D ** -0.5),
        "mlp_w2": nrm((DEPTH, D_FF, D), D_FF ** -0.5),
        "attn_w_in": nrm((N_ATTN_LAYERS, D, ATTN_IN), D ** -0.5),
        "attn_w_out": nrm((N_ATTN_LAYERS, D, D), D ** -0.5),
        "attn_qk_norm": 1.0 + nrm((N_ATTN_LAYERS, 2, HEAD_DIM), 0.02),
        "diff_lambda": nrm((N_ATTN_LAYERS, 4, HEAD_DIM), 0.1),
        "diff_subln": 1.0 + nrm((N_ATTN_LAYERS, B_V_DIM), 0.02),
        "ssm_w_in": nrm((N_SSM_LAYERS, D, D), D ** -0.5),
        "ssm_a_re": -0.5 + nrm((N_SSM_LAYERS, 2, G, N), 0.01),
        "ssm_a_im": math.pi * jnp.arange(N, dtype=F32) + nrm((N_SSM_LAYERS, 2, G, N), 0.01),
        "ssm_log_dt": jax.random.uniform(next(ks), (N_SSM_LAYERS, 2, G), F32, math.log(1e-3), math.log(1e-1)),
        "ssm_b_re": nrm((N_SSM_LAYERS, 2, G, N, C), (2 * C) ** -0.5),
        "ssm_b_im": nrm((N_SSM_LAYERS, 2, G, N, C), (2 * C) ** -0.5),
        "ssm_c_re": nrm((N_SSM_LAYERS, 2, G, C, N), (2 * N) ** -0.5),
        "ssm_c_im": nrm((N_SSM_LAYERS, 2, G, C, N), (2 * N) ** -0.5),
        "ssm_d": nrm((N_SSM_LAYERS, D), 1.0),
        "ssm_glu_w": nrm((N_SSM_LAYERS, D, D), D ** -0.5),
        "ssm_w_out": nrm((N_SSM_LAYERS, D, D), D ** -0.5),
    }


def reference(x_prompt, x_sample, c, cache_a_k, cache_a_v, cache_b_k, cache_b_v, state_ssm, c_ctx,
              ada_w, ada_b, norm_g, mlp_w1, mlp_w2, attn_w_in, attn_w_out, attn_qk_norm, diff_lambda,
              diff_subln, ssm_w_in, ssm_a_re, ssm_a_im, ssm_log_dt, ssm_b_re, ssm_b_im, ssm_c_re,
              ssm_c_im, ssm_d, ssm_glu_w, ssm_w_out):
    rope = _axial_rope(x_sample.shape[1])
    xp, xs = x_prompt, x_sample
    new_ak, new_av, new_bk, new_bv, new_ssm = [], [], [], [], []
    cat = lambda lat, ctx: jnp.concatenate([lat, ctx.astype(lat.dtype)], axis=1)
    for l in range(DEPTH):
        g = norm_g[l]
        mp = _modulation(c_ctx[None, :], ada_w[l], ada_b[l])
        ms = _modulation(c, ada_w[l], ada_b[l])
        n_p = _modulate(xp, g[0], mp[0], mp[1])
        n_s = _modulate(xs, g[0], ms[0], ms[1])
        i = l // 2
        if l % 2 == 0:
            lam_init = 0.8 - 0.6 * math.exp(-0.3 * l)
            mix = functools.partial(_attn_mix, lam_p=diff_lambda[i], lam_init=lam_init,
                                    subln_g=diff_subln[i], w_out=attn_w_out[i])
            qa, ka, va, qb, kb, vb = _attn_qkv(n_p, attn_w_in[i], attn_qk_norm[i], None)
            o_p = mix(qa, ka, va, qb, kb, vb)
            new_ak.append(ka)
            new_av.append(va)
            new_bk.append(kb)
            new_bv.append(vb)
            qa, ka, va, qb, kb, vb = _attn_qkv(n_s, attn_w_in[i], attn_qk_norm[i], rope)
            o_s = mix(qa, cat(ka, cache_a_k[:, i]), cat(va, cache_a_v[:, i]),
                      qb, cat(kb, cache_b_k[:, i]), cat(vb, cache_b_v[:, i]))
        else:
            sp = (ssm_w_in[i], ssm_a_re[i], ssm_a_im[i], ssm_log_dt[i], ssm_b_re[i], ssm_b_im[i],
                  ssm_c_re[i], ssm_c_im[i], ssm_d[i], ssm_glu_w[i], ssm_w_out[i])
            o_p, h_ctx = _ssm_mix(n_p, sp, None)
            new_ssm.append(jnp.stack([h_ctx.real, h_ctx.imag], axis=-1))
            st = state_ssm[:, i].astype(F32)
            o_s, _ = _ssm_mix(n_s, sp, lax.complex(st[..., 0], st[..., 1]))
        xp = xp + mp[2] * _rms(o_p, g[1])
        xs = xs + ms[2] * _rms(o_s, g[1])
        xp = xp + mp[5] * _rms(_mlp(_modulate(xp, g[2], mp[3], mp[4]), mlp_w1[l], mlp_w2[l]), g[3])
        xs = xs + ms[5] * _rms(_mlp(_modulate(xs, g[2], ms[3], ms[4]), mlp_w1[l], mlp_w2[l]), g[3])
    new_cache_a_k = jnp.stack(new_ak, axis=1)
    new_cache_a_v = jnp.stack(new_av, axis=1)
    new_cache_b_k = jnp.stack(new_bk, axis=1)
    new_cache_b_v = jnp.stack(new_bv, axis=1)
    new_state_ssm = jnp.stack(new_ssm, axis=1)
    return (xp, xs, new_cache_a_k, new_cache_a_v, new_cache_b_k, new_cache_b_v, new_state_ssm)
```

```python
import functools
import math

import jax
import jax.numpy as jnp
from jax import lax
from jax.experimental import pallas as pl
from jax.experimental.pallas import tpu as pltpu

F32 = jnp.float32
BF16 = jnp.bfloat16
EPS = 1e-6

HEAD_DIM = 128
A_HEADS = 8
A_KV_HEADS = 2
A_GROUP = A_HEADS // A_KV_HEADS
B_HEADS = 4
B_V_DIM = 2 * HEAD_DIM
GRID_W = 64
ROPE_BASE = 10000.0
ROPE_PAIRS_PER_AXIS = HEAD_DIM // 4
N_MOD = 6
S5_GROUP_CH = 16
S5_STATE = 64
SSM_CHUNK = 16
SSM_LANES = 4 * S5_STATE
SSM_PSEQ = 8

V7X_VMEM_LIMIT_BYTES = 56 * 1024 * 1024
QKV_TN = 256


def _cparams(n_axes):
    return pltpu.CompilerParams(dimension_semantics=("arbitrary",) * n_axes,
                                vmem_limit_bytes=V7X_VMEM_LIMIT_BYTES)


def _sigmoid(x):
    return 1.0 / (1.0 + jnp.exp(-x))


def _rms(x, g):
    return x * lax.rsqrt(jnp.mean(x * x, axis=-1, keepdims=True) + EPS) * g


def _mod_spec(d, slot, row_fn):
    return pl.BlockSpec((1, 1, d), lambda i, *_: (row_fn(i) * N_MOD + slot, 0, 0))


def _mod_kernel(c_ref, w_ref, b_ref, o_ref):
    c = c_ref[...]
    s = (c * _sigmoid(c)).astype(BF16)
    o_ref[0] = jnp.dot(s, w_ref[0].astype(BF16), preferred_element_type=F32) + b_ref[0]


def _modulation(cond8, ada_w, ada_b):
    depth, d, n = ada_w.shape
    tn = 1024
    return pl.pallas_call(
        _mod_kernel,
        grid=(depth, n // tn),
        in_specs=[pl.BlockSpec((8, d), lambda l, j: (0, 0)),
                  pl.BlockSpec((1, d, tn), lambda l, j: (l, 0, j)),
                  pl.BlockSpec((1, 1, tn), lambda l, j: (l, 0, j))],
        out_specs=pl.BlockSpec((1, 8, tn), lambda l, j: (l, 0, j)),
        out_shape=jax.ShapeDtypeStruct((depth, 8, n), F32),
        compiler_params=_cparams(2),
    )(cond8, ada_w, ada_b.reshape(depth, 1, n))


def _split_heads(y, fn):
    return jnp.concatenate([fn(y[:, h * HEAD_DIM:(h + 1) * HEAD_DIM]) for h in range(y.shape[1] // HEAD_DIM)], axis=1)


def _qkv_kernel(*refs, rope, caches):
    x_ref, g_ref, sh_ref, sc_ref, w_ref, qkg_ref = refs[:6]
    refs = refs[6:]
    if rope:
        cos_ref, sin_ref = refs[:2]
        refs = refs[2:]
    qkv_ref = refs[0]
    if caches:
        ka_ref, va_ref, kb_ref, vb_ref = refs[1:5]
    n_scr = refs[-1]
    j = pl.program_id(1)

    @pl.when(j == 0)
    def _():
        n = _rms(x_ref[...], g_ref[...]) * (1.0 + sc_ref[0]) + sh_ref[0]
        n_scr[...] = n.astype(BF16)

    def rot(y):
        if not rope:
            return y
        return y * cos_ref[...] + pltpu.roll(y, HEAD_DIM // 2, axis=1) * sin_ref[...]

    def acc():
        return jnp.dot(n_scr[...], w_ref[...], preferred_element_type=F32)

    q_scale = HEAD_DIM ** -0.5
    b_qa = A_HEADS * HEAD_DIM // QKV_TN
    b_ka = b_qa + A_KV_HEADS * HEAD_DIM // QKV_TN
    b_va = b_ka + A_KV_HEADS * HEAD_DIM // QKV_TN
    b_qb = b_va + B_HEADS * 2 * HEAD_DIM // QKV_TN
    b_kb = b_qb + B_HEADS * 2 * HEAD_DIM // QKV_TN

    @pl.when(j < b_qa)
    def _():
        qkv_ref[...] = _split_heads(acc(), lambda y: rot(_rms(y, qkg_ref[0:1, :])) * q_scale).astype(BF16)

    @pl.when((j >= b_qa) & (j < b_ka))
    def _():
        kn = _split_heads(acc(), lambda y: _rms(y, qkg_ref[1:2, :]))
        if caches:
            ka_ref[...] = kn
        qkv_ref[...] = _split_heads(kn, rot).astype(BF16)

    @pl.when((j >= b_ka) & (j < b_va))
    def _():
        y = acc()
        if caches:
            va_ref[...] = y
        qkv_ref[...] = y.astype(BF16)

    @pl.when((j >= b_va) & (j < b_qb))
    def _():
        qkv_ref[...] = _split_heads(acc(), lambda y: rot(y) * q_scale).astype(BF16)

    @pl.when((j >= b_qb) & (j < b_kb))
    def _():
        y = acc()
        if caches:
            kb_ref[...] = y
        qkv_ref[...] = _split_heads(y, rot).astype(BF16)

    @pl.when(j >= b_kb)
    def _():
        y = acc()
        if caches:
            vb_ref[...] = y
        qkv_ref[...] = y.astype(BF16)


def _qkv_proj(x, g, mod, row_fn, w, qk_g, rope_tabs, caches, tm):
    r, d = x.shape
    n = w.shape[1]
    tn = QKV_TN
    assert A_KV_HEADS * HEAD_DIM == tn and r % tm == 0 and n % tn == 0
    assert not (caches and rope_tabs is not None)
    in_specs = [pl.BlockSpec((tm, d), lambda i, j: (i, 0)),
                pl.BlockSpec((1, d), lambda i, j: (0, 0)),
                _mod_spec(d, 0, row_fn), _mod_spec(d, 1, row_fn),
                pl.BlockSpec((d, tn), lambda i, j: (0, j)),
                pl.BlockSpec((2, HEAD_DIM), lambda i, j: (0, 0))]
    args = [x, g, mod, mod, w, qk_g]
    if rope_tabs is not None:
        nblk = rope_tabs[0].shape[0] // tm
        in_specs += [pl.BlockSpec((tm, HEAD_DIM), lambda i, j: (i % nblk, 0))] * 2
        args += list(rope_tabs)
    out_specs = [pl.BlockSpec((tm, tn), lambda i, j: (i, j))]
    out_shape = [jax.ShapeDtypeStruct((r, n), BF16)]
    if caches:
        b_kb = (A_HEADS + 2 * A_KV_HEADS + 2 * B_HEADS) * HEAD_DIM // tn
        nb = B_HEADS * 2 * HEAD_DIM // tn
        out_specs += [pl.BlockSpec((tm, tn), lambda i, j: (i, 0)),
                      pl.BlockSpec((tm, tn), lambda i, j: (i, 0)),
                      pl.BlockSpec((tm, tn), lambda i, j: (i, jnp.clip(j - b_kb, 0, nb - 1))),
                      pl.BlockSpec((tm, tn), lambda i, j: (i, jnp.clip(j - b_kb - nb, 0, nb - 1)))]
        out_shape += [jax.ShapeDtypeStruct((r, tn), F32), jax.ShapeDtypeStruct((r, tn), F32),
                      jax.ShapeDtypeStruct((r, nb * tn), F32), jax.ShapeDtypeStruct((r, nb * tn), F32)]
    return pl.pallas_call(
        functools.partial(_qkv_kernel, rope=rope_tabs is not None, caches=caches),
        grid=(r // tm, n // tn),
        in_specs=in_specs, out_specs=out_specs, out_shape=out_shape,
        scratch_shapes=[pltpu.VMEM((tm, d), BF16)],
        compiler_params=_cparams(2),
    )(*args)


def _softmax_pv(q, k_chunks, v_chunks, s_scr, rb):
    m_rows = q.shape[0]
    accs, ls = [], []
    for r0 in range(0, m_rows, rb):
        qr = q[r0:r0 + rb]
        m = jnp.full((rb, 1), -jnp.inf, F32)
        off = 0
        for kc in k_chunks:
            k = kc()
            s = lax.dot_general(qr, k, (((1,), (1,)), ((), ())), preferred_element_type=F32)
            s_scr[r0:r0 + rb, off:off + k.shape[0]] = s
            m = jnp.maximum(m, jnp.max(s, axis=-1, keepdims=True))
            off += k.shape[0]
        l = jnp.zeros((rb, 1), F32)
        acc = None
        off = 0
        for vc in v_chunks:
            v = vc()
            p = jnp.exp(s_scr[r0:r0 + rb, off:off + v.shape[0]] - m)
            l = l + jnp.sum(p, axis=-1, keepdims=True)
            pv = jnp.dot(p.astype(BF16), v, preferred_element_type=F32)
            acc = pv if acc is None else acc + pv
            off += v.shape[0]
        accs.append(acc)
        ls.append(l)
    return jnp.concatenate(accs, axis=0), jnp.concatenate(ls, axis=0)


def _chunks(ref, c0, width, tk, cache_ref=None):
    length = ref.shape[0]
    out = [(lambda s=s: ref[s:s + tk, c0:c0 + width]) for s in range(0, length, tk)]
    if cache_ref is not None:
        out.append(lambda: cache_ref[:, c0:c0 + width].astype(BF16))
    return out


def _gqa_kernel(*refs, has_cache, tk, rb):
    if has_cache:
        q_ref, k_ref, v_ref, ck_ref, cv_ref, o_ref, s_scr = refs
    else:
        q_ref, k_ref, v_ref, o_ref, s_scr = refs
        ck_ref = cv_ref = None
    tq = q_ref.shape[0]
    q = jnp.concatenate([q_ref[:, g * HEAD_DIM:(g + 1) * HEAD_DIM] for g in range(A_GROUP)], axis=0)
    acc, l = _softmax_pv(q, _chunks(k_ref, 0, HEAD_DIM, tk, ck_ref), _chunks(v_ref, 0, HEAD_DIM, tk, cv_ref),
                         s_scr, rb)
    o = acc / l
    o_ref[...] = jnp.concatenate([o[g * tq:(g + 1) * tq] for g in range(A_GROUP)], axis=1).astype(BF16)


def _diff_kernel(*refs, has_cache, tk, rb, lam_init):
    if has_cache:
        q_ref, k_ref, v_ref, ck_ref, cv_ref, lam_ref, sg_ref, o_ref, s_scr = refs
    else:
        q_ref, k_ref, v_ref, lam_ref, sg_ref, o_ref, s_scr = refs
        ck_ref = cv_ref = None
    lp = lam_ref[...]
    lam = (jnp.exp(jnp.sum(lp[0:1] * lp[1:2], axis=-1, keepdims=True))
           - jnp.exp(jnp.sum(lp[2:3] * lp[3:4], axis=-1, keepdims=True)) + lam_init)
    outs = []
    for m in range(2):
        acc, l = _softmax_pv(q_ref[:, m * HEAD_DIM:(m + 1) * HEAD_DIM],
                             _chunks(k_ref, m * HEAD_DIM, HEAD_DIM, tk, ck_ref),
                             _chunks(v_ref, 0, B_V_DIM, tk, cv_ref), s_scr, rb)
        outs.append(acc / l)
    o = outs[0] - lam * outs[1]
    o_ref[...] = (_rms(o, sg_ref[...]) * (1.0 - lam_init)).astype(BF16)


def _attention(qkv, nb, caches, lam_p, subln_g, lam_init, tq_a, tq_b, tk):
    r = qkv.shape[0]
    length = r // nb
    has_cache = caches is not None
    lk = length + (caches[0].shape[0] // nb if has_cache else 0)
    tk = min(tk, length)
    ka0 = A_HEADS
    va0 = ka0 + A_KV_HEADS
    qb0 = (A_HEADS + 2 * A_KV_HEADS) * HEAD_DIM // B_V_DIM
    kb0 = qb0 + B_HEADS
    vb0 = kb0 + B_HEADS

    nq = length // tq_a
    wq = A_GROUP * HEAD_DIM
    in_specs = [pl.BlockSpec((tq_a, wq), lambda b, h, i: (b * nq + i, h)),
                pl.BlockSpec((length, HEAD_DIM), lambda b, h, i: (b, ka0 + h)),
                pl.BlockSpec((length, HEAD_DIM), lambda b, h, i: (b, va0 + h))]
    args = [qkv, qkv, qkv]
    if has_cache:
        pc = caches[0].shape[0] // nb
        in_specs += [pl.BlockSpec((pc, HEAD_DIM), lambda b, h, i: (b, h))] * 2
        args += [caches[0], caches[1]]
    rb_a = min(256, A_GROUP * tq_a)
    oa = pl.pallas_call(
        functools.partial(_gqa_kernel, has_cache=has_cache, tk=tk, rb=rb_a),
        grid=(nb, A_KV_HEADS, nq),
        in_specs=in_specs,
        out_specs=pl.BlockSpec((tq_a, wq), lambda b, h, i: (b * nq + i, h)),
        out_shape=jax.ShapeDtypeStruct((r, A_HEADS * HEAD_DIM), BF16),
        scratch_shapes=[pltpu.VMEM((A_GROUP * tq_a, lk), F32)],
        compiler_params=_cparams(3),
    )(*args)

    nq = length // tq_b
    in_specs = [pl.BlockSpec((tq_b, B_V_DIM), lambda b, h, i: (b * nq + i, qb0 + h)),
                pl.BlockSpec((length, B_V_DIM), lambda b, h, i: (b, kb0 + h)),
                pl.BlockSpec((length, B_V_DIM), lambda b, h, i: (b, vb0 + h))]
    args = [qkv, qkv, qkv]
    if has_cache:
        in_specs += [pl.BlockSpec((pc, B_V_DIM), lambda b, h, i: (b, h))] * 2
        args += [caches[2], caches[3]]
    in_specs += [pl.BlockSpec((4, HEAD_DIM), lambda b, h, i: (0, 0)),
                 pl.BlockSpec((1, B_V_DIM), lambda b, h, i: (0, 0))]
    args += [lam_p, subln_g]
    ob = pl.pallas_call(
        functools.partial(_diff_kernel, has_cache=has_cache, tk=tk, rb=min(256, tq_b), lam_init=lam_init),
        grid=(nb, B_HEADS, nq),
        in_specs=in_specs,
        out_specs=pl.BlockSpec((tq_b, B_V_DIM), lambda b, h, i: (b * nq + i, h)),
        out_shape=jax.ShapeDtypeStruct((r, B_HEADS * B_V_DIM), BF16),
        scratch_shapes=[pltpu.VMEM((tq_b, lk), F32)],
        compiler_params=_cparams(3),
    )(*args)
    return oa, ob


def _outproj_kernel(*refs, n_in):
    o_refs = refs[:n_in]
    w_refs = refs[n_in:2 * n_in]
    x_ref, g_ref, gate_ref, out_ref = refs[2 * n_in:]
    y = None
    for o_ref, w_ref in zip(o_refs, w_refs):
        t = jnp.dot(o_ref[...], w_ref[...], preferred_element_type=F32)
        y = t if y is None else y + t
    out_ref[...] = x_ref[...] + gate_ref[0] * _rms(y, g_ref[...])


def _outproj(os_, ws, x, g, mod, row_fn, tm):
    r, d = x.shape
    n_in = len(os_)
    in_specs = ([pl.BlockSpec((tm, o.shape[1]), lambda i: (i, 0)) for o in os_]
                + [pl.BlockSpec(w.shape, lambda i: (0, 0), pipeline_mode=pl.Buffered(1)) for w in ws]
                + [pl.BlockSpec((tm, d), lambda i: (i, 0)),
                   pl.BlockSpec((1, d), lambda i: (0, 0)),
                   _mod_spec(d, 2, row_fn)])
    return pl.pallas_call(
        functools.partial(_outproj_kernel, n_in=n_in),
        grid=(r // tm,),
        in_specs=in_specs,
        out_specs=pl.BlockSpec((tm, d), lambda i: (i, 0)),
        out_shape=jax.ShapeDtypeStruct((r, d), F32),
        compiler_params=_cparams(1),
    )(*os_, *ws, x, g, mod)


def _mlp_kernel(x_ref, g2_ref, sh_ref, sc_ref, gate_ref, g3_ref, w1_ref, w2_ref, out_ref, n_scr):
    f = pl.program_id(1)

    @pl.when(f == 0)
    def _():
        n = _rms(x_ref[...], g2_ref[...]) * (1.0 + sc_ref[0]) + sh_ref[0]
        n_scr[...] = n.astype(BF16)

    h = jnp.maximum(jnp.dot(n_scr[...], w1_ref[...], preferred_element_type=F32), 0.0)
    t = jnp.dot((h * h).astype(BF16), w2_ref[...], preferred_element_type=F32)

    @pl.when(f == 0)
    def _():
        out_ref[...] = t

    @pl.when(f > 0)
    def _():
        out_ref[...] += t

    @pl.when(f == pl.num_programs(1) - 1)
    def _():
        out_ref[...] = x_ref[...] + gate_ref[0] * _rms(out_ref[...], g3_ref[...])


def _mlp(x, g2, g3, mod, row_fn, w1, w2, tm, tf):
    r, d = x.shape
    dff = w1.shape[1]
    return pl.pallas_call(
        _mlp_kernel,
        grid=(r // tm, dff // tf),
        in_specs=[pl.BlockSpec((tm, d), lambda i, f: (i, 0)),
                  pl.BlockSpec((1, d), lambda i, f: (0, 0)),
                  _mod_spec(d, 3, row_fn), _mod_spec(d, 4, row_fn), _mod_spec(d, 5, row_fn),
                  pl.BlockSpec((1, d), lambda i, f: (0, 0)),
                  pl.BlockSpec((d, tf), lambda i, f: (0, f)),
                  pl.BlockSpec((tf, d), lambda i, f: (f, 0))],
        out_specs=pl.BlockSpec((tm, d), lambda i, f: (i, 0)),
        out_shape=jax.ShapeDtypeStruct((r, d), F32),
        scratch_shapes=[pltpu.VMEM((tm, d), BF16)],
        compiler_params=_cparams(2),
    )(x, g2, mod, mod, mod, g3, w1, w2)


def _ssm_in_kernel(x_ref, g_ref, sh_ref, sc_ref, w_ref, u_ref, ub_ref, n_scr):
    @pl.when(pl.program_id(1) == 0)
    def _():
        n = _rms(x_ref[...], g_ref[...]) * (1.0 + sc_ref[0]) + sh_ref[0]
        n_scr[...] = n.astype(BF16)

    u = jnp.dot(n_scr[...], w_ref[...], preferred_element_type=F32)
    u_ref[...] = u
    ub_ref[...] = u.astype(BF16)


def _ssm_in(x, g, mod, row_fn, w, tm, tn):
    r, d = x.shape
    n = w.shape[1]
    return pl.pallas_call(
        _ssm_in_kernel,
        grid=(r // tm, n // tn),
        in_specs=[pl.BlockSpec((tm, d), lambda i, j: (i, 0)),
                  pl.BlockSpec((1, d), lambda i, j: (0, 0)),
                  _mod_spec(d, 0, row_fn), _mod_spec(d, 1, row_fn),
                  pl.BlockSpec((d, tn), lambda i, j: (0, j))],
        out_specs=[pl.BlockSpec((tm, tn), lambda i, j: (i, j))] * 2,
        out_shape=[jax.ShapeDtypeStruct((r, n), F32), jax.ShapeDtypeStruct((r, n), BF16)],
        scratch_shapes=[pltpu.VMEM((tm, d), BF16)],
        compiler_params=_cparams(2),
    )(x, g, mod, mod, w)


def _ssm_params(a_re, a_im, log_dt, b_re, b_im, c_re, c_im, seg_chunks):
    t = SSM_CHUNK
    lam = lax.complex(a_re.astype(F32), a_im.astype(F32))
    dt = jnp.exp(log_dt.astype(F32))[..., None]
    ldt = lam * dt
    a_bar = jnp.exp(ldt)
    k = jnp.arange(t + 1, dtype=F32)
    ap = jnp.exp(ldt[None] * k[:, None, None, None])
    bb = ((a_bar - 1.0) / lam)[..., None] * lax.complex(b_re.astype(F32), b_im.astype(F32))
    cc = lax.complex(c_re.astype(F32), c_im.astype(F32))
    g = lam.shape[1]

    wf = ap[t - 1 - jnp.arange(t), 0][..., None] * bb[0][None]
    wb = ap[jnp.arange(t), 1][..., None] * bb[1][None]
    to_rows = lambda w: jnp.transpose(w, (1, 0, 3, 2)).reshape(g, t * S5_GROUP_CH, S5_STATE)
    w_in = jnp.concatenate([to_rows(wf.real), to_rows(wb.real), to_rows(wf.imag), to_rows(wb.imag)], axis=-1)

    taps = jnp.einsum("dgcn,kdgn,dgne->dkgce", cc, ap[:t], bb).real
    i_idx = jnp.arange(t)[None, :]
    s_idx = jnp.arange(t)[:, None]
    diff = i_idx - s_idx
    kf = jnp.where((diff >= 0)[..., None, None, None], taps[0][jnp.clip(diff, 0, t - 1)], 0.0)
    kb = jnp.where((diff <= 0)[..., None, None, None], taps[1][jnp.clip(-diff, 0, t - 1)], 0.0)
    toep = jnp.transpose(kf + kb, (2, 0, 4, 1, 3)).reshape(g, t * S5_GROUP_CH, t * S5_GROUP_CH)

    cf = cc[0][None] * ap[1 + jnp.arange(t), 0][:, :, None, :]
    cb = cc[1][None] * ap[t - jnp.arange(t), 1][:, :, None, :]
    to_cols = lambda w: jnp.transpose(w, (1, 3, 0, 2)).reshape(g, S5_STATE, t * S5_GROUP_CH)
    c_out = jnp.concatenate([to_cols(cf.real), to_cols(cb.real), to_cols(-cf.imag), to_cols(-cb.imag)], axis=1)
    w_out = jnp.concatenate([toep, c_out], axis=1)

    def decay(power):
        a = jnp.exp(ldt * float(power))
        return jnp.concatenate([a[0].real, a[1].real, a[0].imag, a[1].imag], axis=-1)[:, None, :]

    return w_in.astype(BF16), w_out.astype(BF16), decay(t), decay(t * seg_chunks)


def _cmul_add(ar, ai, h_re, h_im, s_re, s_im):
    return ar * h_re - ai * h_im + s_re, ar * h_im + ai * h_re + s_im


def _chunk_scan(s_scr, t_scr, rows, nc, ar, ai, init, store):
    half = SSM_LANES // 2
    fwd = lax.broadcasted_iota(jnp.int32, (rows, half), 1) < S5_STATE
    h_re, h_im = init
    for k in range(nc):
        rf, rbk = k * rows, (nc - 1 - k) * rows
        s_re = jnp.where(fwd, s_scr[rf:rf + rows, 0:half], s_scr[rbk:rbk + rows, 0:half])
        s_im = jnp.where(fwd, s_scr[rf:rf + rows, half:], s_scr[rbk:rbk + rows, half:])
        if store:
            t_scr[rf:rf + rows, 0:half] = h_re
            t_scr[rf:rf + rows, half:] = h_im
        h_re, h_im = _cmul_add(ar, ai, h_re, h_im, s_re, s_im)
    return h_re, h_im


def _entering_states(t_scr, hp_scr, rows, nc):
    half = SSM_LANES // 2
    fwd = lax.broadcasted_iota(jnp.int32, (rows, SSM_LANES), 1) % half < S5_STATE
    for k in range(nc):
        rf, rbk = k * rows, (nc - 1 - k) * rows
        hp_scr[rf:rf + rows, :] = jnp.where(fwd, t_scr[rf:rf + rows, :], t_scr[rbk:rbk + rows, :]).astype(BF16)


def _ssm_core_kernel(up_ref, us_ref, win_ref, wout_ref, a_ref, aseg_ref, h0_ref, yp_ref, ys_ref, hfin_ref,
                     s_scr, t_scr, hp_scr, *, p_rows, nsub):
    half = SSM_LANES // 2
    kdim = SSM_CHUNK * S5_GROUP_CH
    ar, ai = a_ref[0][:, 0:half], a_ref[0][:, half:]

    def emit(u_ref, y_ref, rows):
        y_ref[0] = (jnp.dot(u_ref[0], wout_ref[0, 0:kdim, :], preferred_element_type=F32)
                    + jnp.dot(hp_scr[0:rows, :], wout_ref[0, kdim:, :], preferred_element_type=F32))

    rp = up_ref.shape[1]
    s_scr[0:rp, :] = jnp.dot(up_ref[0], win_ref[0], preferred_element_type=F32)
    zero = jnp.zeros((p_rows, half), F32)
    h_re, h_im = _chunk_scan(s_scr, t_scr, p_rows, rp // p_rows, ar, ai, (zero, zero), True)
    hfin_ref[0, :, 0:half] = h_re
    hfin_ref[0, :, half:] = h_im
    _entering_states(t_scr, hp_scr, p_rows, rp // p_rows)
    emit(up_ref, yp_ref, rp)

    rs = us_ref.shape[1]
    ncl = rs // SSM_PSEQ
    s_scr[0:rs, :] = jnp.dot(us_ref[0], win_ref[0], preferred_element_type=F32)
    zero = jnp.zeros((SSM_PSEQ, half), F32)
    f_re, f_im = _chunk_scan(s_scr, t_scr, SSM_PSEQ, ncl, ar, ai, (zero, zero), False)
    gr, gi = aseg_ref[0][:, 0:half], aseg_ref[0][:, half:]
    h0_re, h0_im = h0_ref[0][:, 0:half], h0_ref[0][:, half:]
    fwd = lax.broadcasted_iota(jnp.int32, (SSM_PSEQ, half), 1) < S5_STATE
    seg = lax.broadcasted_iota(jnp.int32, (SSM_PSEQ, half), 0) % nsub
    carried = jnp.where(fwd, seg, nsub - 1 - seg) != 0
    e_re, e_im = h0_re, h0_im
    for _ in range(nsub - 1):
        x_re, x_im = _cmul_add(gr, gi, e_re, e_im, f_re, f_im)
        shift = lambda x: jnp.where(fwd, pltpu.roll(x, 1, axis=0), pltpu.roll(x, SSM_PSEQ - 1, axis=0))
        e_re = h0_re + jnp.where(carried, shift(x_re), 0.0)
        e_im = h0_im + jnp.where(carried, shift(x_im), 0.0)
    _chunk_scan(s_scr, t_scr, SSM_PSEQ, ncl, ar, ai, (e_re, e_im), True)
    _entering_states(t_scr, hp_scr, SSM_PSEQ, ncl)
    emit(us_ref, ys_ref, rs)


def _ssm_core(ug_p, ug_s, w_in, w_out, a_chunk, a_seg, h0, p_rows, nsub):
    g, rp, kdim = ug_p.shape
    rs = ug_s.shape[1]
    rmax = max(rp, rs)
    blk = lambda a: pl.BlockSpec((1,) + a.shape[1:], lambda i: (i, 0, 0))
    args = (ug_p, ug_s, w_in, w_out, a_chunk, a_seg, h0)
    return pl.pallas_call(
        functools.partial(_ssm_core_kernel, p_rows=p_rows, nsub=nsub),
        grid=(g,),
        in_specs=[blk(a) for a in args],
        out_specs=[pl.BlockSpec((1, rp, kdim), lambda i: (i, 0, 0)),
                   pl.BlockSpec((1, rs, kdim), lambda i: (i, 0, 0)),
                   pl.BlockSpec((1, p_rows, SSM_LANES), lambda i: (i, 0, 0))],
        out_shape=[jax.ShapeDtypeStruct((g, rp, kdim), F32), jax.ShapeDtypeStruct((g, rs, kdim), F32),
                   jax.ShapeDtypeStruct((g, p_rows, SSM_LANES), F32)],
        scratch_shapes=[pltpu.VMEM((rmax, SSM_LANES), F32), pltpu.VMEM((rmax, SSM_LANES), F32),
                        pltpu.VMEM((rmax, SSM_LANES), BF16)],
        compiler_params=_cparams(1),
    )(*args)


def _glu_kernel(y_ref, u_ref, d_ref, w_ref, z_ref):
    v = y_ref[...] + d_ref[...] * u_ref[...]
    z = 0.5 * v * (1.0 + jnp.tanh(math.sqrt(2.0 / math.pi) * (v + 0.044715 * (v * v * v))))
    gl = jnp.dot(z.astype(BF16), w_ref[...], preferred_element_type=F32)
    z_ref[...] = (z * _sigmoid(gl)).astype(BF16)


def _glu(y, u, d, w, tm):
    r, dm = y.shape
    return pl.pallas_call(
        _glu_kernel,
        grid=(r // tm,),
        in_specs=[pl.BlockSpec((tm, dm), lambda i: (i, 0)),
                  pl.BlockSpec((tm, dm), lambda i: (i, 0)),
                  pl.BlockSpec((1, dm), lambda i: (0, 0)),
                  pl.BlockSpec(w.shape, lambda i: (0, 0), pipeline_mode=pl.Buffered(1))],
        out_specs=pl.BlockSpec((tm, dm), lambda i: (i, 0)),
        out_shape=jax.ShapeDtypeStruct((r, dm), BF16),
        compiler_params=_cparams(1),
    )(y, u, d, w)


def _rope_tables(length):
    rows = length // GRID_W
    row = jnp.repeat(jnp.arange(rows, dtype=F32), GRID_W)
    col = jnp.tile(jnp.arange(GRID_W, dtype=F32), rows)
    inv = ROPE_BASE ** (-jnp.arange(ROPE_PAIRS_PER_AXIS, dtype=F32) / ROPE_PAIRS_PER_AXIS)
    ang = jnp.concatenate([row[:, None] * inv, col[:, None] * inv], axis=-1)
    cos, sin = jnp.cos(ang), jnp.sin(ang)
    return jnp.concatenate([cos, cos], axis=-1), jnp.concatenate([-sin, sin], axis=-1)


def kernel(x_prompt, x_sample, c, cache_a_k, cache_a_v, cache_b_k, cache_b_v, state_ssm, c_ctx, ada_w, ada_b, norm_g, mlp_w1, mlp_w2, attn_w_in, attn_w_out, attn_qk_norm, diff_lambda, diff_subln, ssm_w_in, ssm_a_re, ssm_a_im, ssm_log_dt, ssm_b_re, ssm_b_im, ssm_c_re, ssm_c_im, ssm_d, ssm_glu_w, ssm_w_out):
    bp, lp, d = x_prompt.shape
    bs, ls, _ = x_sample.shape
    depth = ada_w.shape[0]
    past = cache_a_k.shape[2]
    assert bs + 1 <= 8 and bp % SSM_PSEQ == 0 and SSM_PSEQ % bs == 0
    nsub = SSM_PSEQ // bs
    assert lp % SSM_CHUNK == 0 and ls % (SSM_CHUNK * nsub) == 0 and ls % GRID_W == 0

    cond8 = jnp.zeros((8, d), F32).at[0].set(c_ctx).at[1:1 + bs].set(c)
    mods = _modulation(cond8, ada_w, ada_b).reshape(depth, 8 * N_MOD, 1, d)

    xp = x_prompt.reshape(bp * lp, d)
    xs = x_sample.reshape(bs * ls, d)
    tm = 512
    row_p = lambda i: 0
    row_s = lambda i: 1 + i // (ls // tm)
    new_ak = new_av = new_bk = new_bv = new_ssm = None
    for l in range(depth):
        mod = mods[l]
        g = norm_g[l][:, None, :]
        i = l // 2
        if l % 2 == 0:
            lam_init = 0.8 - 0.6 * math.exp(-0.3 * l)
            w_in = attn_w_in[i].astype(BF16)
            w_out = attn_w_out[i].astype(BF16)
            n_a = A_HEADS * HEAD_DIM
            w_outs = [w_out[:n_a], w_out[n_a:]]
            qkv, ka, va, kb, vb = _qkv_proj(xp, g[0], mod, row_p, w_in, attn_qk_norm[i], None, True, tm)
            new_ak = ka.reshape(bp, 1, lp, A_KV_HEADS, HEAD_DIM)
            new_av = va.reshape(bp, 1, lp, A_KV_HEADS, HEAD_DIM)
            new_bk = kb.reshape(bp, 1, lp, B_HEADS, 2, HEAD_DIM)
            new_bv = vb.reshape(bp, 1, lp, B_HEADS, B_V_DIM)
            oa, ob = _attention(qkv, bp, None, diff_lambda[i], diff_subln[i][None, :], lam_init,
                                tq_a=lp, tq_b=lp, tk=512)
            xp = _outproj([oa, ob], w_outs, xp, g[1], mod, row_p, tm)
            (qkv,) = _qkv_proj(xs, g[0], mod, row_s, w_in, attn_qk_norm[i], _rope_tables(ls), False, tm)
            caches = (cache_a_k[:, i].reshape(bs * past, A_KV_HEADS * HEAD_DIM),
                      cache_a_v[:, i].reshape(bs * past, A_KV_HEADS * HEAD_DIM),
                      cache_b_k[:, i].reshape(bs * past, B_HEADS * 2 * HEAD_DIM),
                      cache_b_v[:, i].reshape(bs * past, B_HEADS * B_V_DIM))
            oa, ob = _attention(qkv, bs, caches, diff_lambda[i], diff_subln[i][None, :], lam_init,
                                tq_a=256, tq_b=512, tk=512)
            xs = _outproj([oa, ob], w_outs, xs, g[1], mod, row_s, tm)
        else:
            w_in = ssm_w_in[i].astype(BF16)
            u_p, ub_p = _ssm_in(xp, g[0], mod, row_p, w_in, tm, 512)
            u_s, ub_s = _ssm_in(xs, g[0], mod, row_s, w_in, tm, 512)
            ng = d // S5_GROUP_CH
            ncp, ncl = lp // SSM_CHUNK, ls // SSM_CHUNK // nsub
            kd = SSM_CHUNK * S5_GROUP_CH
            ug_p = jnp.transpose(ub_p.reshape(bp, ncp, SSM_CHUNK, ng, S5_GROUP_CH), (3, 1, 0, 2, 4))
            ug_p = ug_p.reshape(ng, ncp * bp, kd)
            ug_s = jnp.transpose(ub_s.reshape(bs, nsub, ncl, SSM_CHUNK, ng, S5_GROUP_CH), (4, 2, 0, 1, 3, 5))
            ug_s = ug_s.reshape(ng, ncl * SSM_PSEQ, kd)
            wi, wo, a_chunk, a_seg = _ssm_params(ssm_a_re[i], ssm_a_im[i], ssm_log_dt[i], ssm_b_re[i], ssm_b_im[i],
                                                 ssm_c_re[i], ssm_c_im[i], ncl)
            st = state_ssm[:, i].astype(F32)
            h0 = jnp.zeros((ng, bs, nsub, 4, S5_STATE), F32)
            h0 = h0.at[:, :, 0, 0].set(jnp.transpose(st[:, 0, :, :, 0], (1, 0, 2)))
            h0 = h0.at[:, :, 0, 2].set(jnp.transpose(st[:, 0, :, :, 1], (1, 0, 2)))
            h0 = h0.at[:, :, nsub - 1, 1].set(jnp.transpose(st[:, 1, :, :, 0], (1, 0, 2)))
            h0 = h0.at[:, :, nsub - 1, 3].set(jnp.transpose(st[:, 1, :, :, 1], (1, 0, 2)))
            h0 = h0.reshape(ng, SSM_PSEQ, SSM_LANES)
            yg_p, yg_s, hfin = _ssm_core(ug_p, ug_s, wi, wo, a_chunk, a_seg, h0, bp, nsub)
            y_p = jnp.transpose(yg_p.reshape(ng, ncp, bp, SSM_CHUNK, S5_GROUP_CH), (2, 1, 3, 0, 4)).reshape(bp * lp, d)
            y_s = jnp.transpose(yg_s.reshape(ng, ncl, bs, nsub, SSM_CHUNK, S5_GROUP_CH), (2, 3, 1, 4, 0, 5))
            y_s = y_s.reshape(bs * ls, d)
            new_ssm = jnp.transpose(hfin.reshape(ng, bp, 2, 2, S5_STATE), (1, 3, 0, 4, 2))
            glu_w = ssm_glu_w[i].astype(BF16)
            w_out = ssm_w_out[i].astype(BF16)
            dvec = ssm_d[i][None, :]
            z_p = _glu(y_p, u_p, dvec, glu_w, tm)
            z_s = _glu(y_s, u_s, dvec, glu_w, tm)
            xp = _outproj([z_p], [w_out], xp, g[1], mod, row_p, tm)
            xs = _outproj([z_s], [w_out], xs, g[1], mod, row_s, tm)
        w1 = mlp_w1[l].astype(BF16)
        w2 = mlp_w2[l].astype(BF16)
        xp = _mlp(xp, g[2], g[3], mod, row_p, w1, w2, tm, 1024)
        xs = _mlp(xs, g[2], g[3], mod, row_s, w1, w2, tm, 1024)
    return (xp.reshape(bp, lp, d), xs.reshape(bs, ls, d), new_ak, new_av, new_bk, new_bv,
            new_ssm[:, None])
```

```python
import functools
import math

import jax
import jax.numpy as jnp
from jax import lax
from jax.experimental import pallas as pl
from jax.experimental.pallas import tpu as pltpu

F32 = jnp.float32
BF16 = jnp.bfloat16
EPS = 1e-6

HEAD_DIM = 128
A_HEADS = 8
A_KV_HEADS = 2
A_GROUP = A_HEADS // A_KV_HEADS
B_HEADS = 4
B_V_DIM = 2 * HEAD_DIM
GRID_W = 64
ROPE_BASE = 10000.0
ROPE_PAIRS_PER_AXIS = HEAD_DIM // 4
N_MOD = 6
S5_GROUP_CH = 16
S5_STATE = 64
SSM_CHUNK = 16
SSM_LANES = 4 * S5_STATE
SSM_PSEQ = 8

V7X_VMEM_LIMIT_BYTES = 56 * 1024 * 1024
MLP_ACC_CHUNK = 512
QKV_TN = 512


def _cparams(n_axes):
    return pltpu.CompilerParams(dimension_semantics=("arbitrary",) * n_axes,
                                vmem_limit_bytes=V7X_VMEM_LIMIT_BYTES)


def _sigmoid(x):
    return 1.0 / (1.0 + jnp.exp(-x))


def _rms(x, g):
    return x * lax.rsqrt(jnp.mean(x * x, axis=-1, keepdims=True) + EPS) * g


def _mod_spec(d, slot, row_fn):
    return pl.BlockSpec((1, 1, d), lambda i, *_: (row_fn(i) * N_MOD + slot, 0, 0))


def _mod_kernel(c_ref, w_ref, b_ref, o_ref):
    c = c_ref[...]
    s = (c * _sigmoid(c)).astype(BF16)
    o_ref[0] = jnp.dot(s, w_ref[0].astype(BF16), preferred_element_type=F32) + b_ref[0]


def _modulation(cond8, ada_w, ada_b):
    depth, d, n = ada_w.shape
    tn = 1024
    return pl.pallas_call(
        _mod_kernel,
        grid=(depth, n // tn),
        in_specs=[pl.BlockSpec((8, d), lambda l, j: (0, 0)),
                  pl.BlockSpec((1, d, tn), lambda l, j: (l, 0, j)),
                  pl.BlockSpec((1, 1, tn), lambda l, j: (l, 0, j))],
        out_specs=pl.BlockSpec((1, 8, tn), lambda l, j: (l, 0, j)),
        out_shape=jax.ShapeDtypeStruct((depth, 8, n), F32),
        compiler_params=_cparams(2),
    )(cond8, ada_w, ada_b.reshape(depth, 1, n))


def _split_heads(y, fn):
    return jnp.concatenate([fn(y[:, h * HEAD_DIM:(h + 1) * HEAD_DIM]) for h in range(y.shape[1] // HEAD_DIM)], axis=1)


def _qkv_kernel(*refs, rope, caches):
    x_ref, g_ref, sh_ref, sc_ref, w_ref, qkg_ref = refs[:6]
    refs = refs[6:]
    if rope:
        cos_ref, sin_ref = refs[:2]
        refs = refs[2:]
    qkv_ref = refs[0]
    if caches:
        ka_ref, va_ref, kb_ref, vb_ref = refs[1:5]
    n_scr = refs[-1]
    j = pl.program_id(1)

    @pl.when(j == 0)
    def _():
        n = _rms(x_ref[...], g_ref[...]) * (1.0 + sc_ref[0]) + sh_ref[0]
        n_scr[...] = n.astype(BF16)

    def rot(y):
        if not rope:
            return y
        return y * cos_ref[...] + pltpu.roll(y, HEAD_DIM // 2, axis=1) * sin_ref[...]

    def acc():
        return jnp.dot(n_scr[...], w_ref[...], preferred_element_type=F32)

    q_scale = HEAD_DIM ** -0.5
    kv_w = A_KV_HEADS * HEAD_DIM
    b_qa = A_HEADS * HEAD_DIM // QKV_TN
    b_va = b_qa + 2 * kv_w // QKV_TN
    b_qb = b_va + B_HEADS * 2 * HEAD_DIM // QKV_TN
    b_kb = b_qb + B_HEADS * 2 * HEAD_DIM // QKV_TN

    @pl.when(j < b_qa)
    def _():
        qkv_ref[...] = _split_heads(acc(), lambda y: rot(_rms(y, qkg_ref[0:1, :])) * q_scale).astype(BF16)

    @pl.when((j >= b_qa) & (j < b_va))
    def _():
        y = acc()
        kn = _split_heads(y[:, :kv_w], lambda t: _rms(t, qkg_ref[1:2, :]))
        if caches:
            ka_ref[...] = kn
            va_ref[...] = y[:, kv_w:]
        qkv_ref[:, :kv_w] = _split_heads(kn, rot).astype(BF16)
        qkv_ref[:, kv_w:] = y[:, kv_w:].astype(BF16)

    @pl.when((j >= b_va) & (j < b_qb))
    def _():
        qkv_ref[...] = _split_heads(acc(), lambda y: rot(y) * q_scale).astype(BF16)

    @pl.when((j >= b_qb) & (j < b_kb))
    def _():
        y = acc()
        if caches:
            kb_ref[...] = y
        qkv_ref[...] = _split_heads(y, rot).astype(BF16)

    @pl.when(j >= b_kb)
    def _():
        y = acc()
        if caches:
            vb_ref[...] = y
        qkv_ref[...] = y.astype(BF16)


def _qkv_proj(x, g, mod, row_fn, w, qk_g, rope_tabs, caches, tm):
    r, d = x.shape
    n = w.shape[1]
    tn = QKV_TN
    kv_w = A_KV_HEADS * HEAD_DIM
    assert 2 * kv_w == tn and r % tm == 0 and n % tn == 0
    assert not (caches and rope_tabs is not None)
    in_specs = [pl.BlockSpec((tm, d), lambda i, j: (i, 0)),
                pl.BlockSpec((1, d), lambda i, j: (0, 0)),
                _mod_spec(d, 0, row_fn), _mod_spec(d, 1, row_fn),
                pl.BlockSpec((d, tn), lambda i, j: (0, j)),
                pl.BlockSpec((2, HEAD_DIM), lambda i, j: (0, 0))]
    args = [x, g, mod, mod, w, qk_g]
    if rope_tabs is not None:
        nblk = rope_tabs[0].shape[0] // tm
        in_specs += [pl.BlockSpec((tm, HEAD_DIM), lambda i, j: (i % nblk, 0))] * 2
        args += list(rope_tabs)
    out_specs = [pl.BlockSpec((tm, tn), lambda i, j: (i, j))]
    out_shape = [jax.ShapeDtypeStruct((r, n), BF16)]
    if caches:
        b_kb = (A_HEADS + 2 * A_KV_HEADS + 2 * B_HEADS) * HEAD_DIM // tn
        nb = B_HEADS * 2 * HEAD_DIM // tn
        out_specs += [pl.BlockSpec((tm, kv_w), lambda i, j: (i, 0)),
                      pl.BlockSpec((tm, kv_w), lambda i, j: (i, 0)),
                      pl.BlockSpec((tm, tn), lambda i, j: (i, jnp.clip(j - b_kb, 0, nb - 1))),
                      pl.BlockSpec((tm, tn), lambda i, j: (i, jnp.clip(j - b_kb - nb, 0, nb - 1)))]
        out_shape += [jax.ShapeDtypeStruct((r, kv_w), F32), jax.ShapeDtypeStruct((r, kv_w), F32),
                      jax.ShapeDtypeStruct((r, nb * tn), F32), jax.ShapeDtypeStruct((r, nb * tn), F32)]
    return pl.pallas_call(
        functools.partial(_qkv_kernel, rope=rope_tabs is not None, caches=caches),
        grid=(r // tm, n // tn),
        in_specs=in_specs, out_specs=out_specs, out_shape=out_shape,
        scratch_shapes=[pltpu.VMEM((tm, d), BF16)],
        compiler_params=_cparams(2),
    )(*args)


def _softmax_pv(q, k_chunks, v_chunks, s_scr, rb):
    m_rows = q.shape[0]
    accs, ls = [], []
    for r0 in range(0, m_rows, rb):
        qr = q[r0:r0 + rb]
        m = jnp.full((rb, 1), -jnp.inf, F32)
        off = 0
        for kc in k_chunks:
            k = kc()
            s = lax.dot_general(qr, k, (((1,), (1,)), ((), ())), preferred_element_type=F32)
            s_scr[r0:r0 + rb, off:off + k.shape[0]] = s
            m = jnp.maximum(m, jnp.max(s, axis=-1, keepdims=True))
            off += k.shape[0]
        l = jnp.zeros((rb, 1), F32)
        acc = None
        off = 0
        for vc in v_chunks:
            v = vc()
            p = jnp.exp(s_scr[r0:r0 + rb, off:off + v.shape[0]] - m)
            l = l + jnp.sum(p, axis=-1, keepdims=True)
            pv = jnp.dot(p.astype(BF16), v, preferred_element_type=F32)
            acc = pv if acc is None else acc + pv
            off += v.shape[0]
        accs.append(acc)
        ls.append(l)
    return jnp.concatenate(accs, axis=0), jnp.concatenate(ls, axis=0)


def _chunks(ref, c0, width, tk, cache_ref=None):
    length = ref.shape[0]
    out = [(lambda s=s: ref[s:s + tk, c0:c0 + width]) for s in range(0, length, tk)]
    if cache_ref is not None:
        out.append(lambda: cache_ref[:, c0:c0 + width].astype(BF16))
    return out


def _gqa_kernel(*refs, has_cache, tk, rb):
    if has_cache:
        q_ref, k_ref, v_ref, ck_ref, cv_ref, o_ref, s_scr = refs
    else:
        q_ref, k_ref, v_ref, o_ref, s_scr = refs
        ck_ref = cv_ref = None
    tq = q_ref.shape[0]
    q = jnp.concatenate([q_ref[:, g * HEAD_DIM:(g + 1) * HEAD_DIM] for g in range(A_GROUP)], axis=0)
    acc, l = _softmax_pv(q, _chunks(k_ref, 0, HEAD_DIM, tk, ck_ref), _chunks(v_ref, 0, HEAD_DIM, tk, cv_ref),
                         s_scr, rb)
    o = acc / l
    o_ref[...] = jnp.concatenate([o[g * tq:(g + 1) * tq] for g in range(A_GROUP)], axis=1).astype(BF16)


def _diff_kernel(*refs, has_cache, tk, rb, lam_init):
    if has_cache:
        q_ref, k_ref, v_ref, ck_ref, cv_ref, lam_ref, sg_ref, o_ref, s_scr = refs
    else:
        q_ref, k_ref, v_ref, lam_ref, sg_ref, o_ref, s_scr = refs
        ck_ref = cv_ref = None
    lp = lam_ref[...]
    lam = (jnp.exp(jnp.sum(lp[0:1] * lp[1:2], axis=-1, keepdims=True))
           - jnp.exp(jnp.sum(lp[2:3] * lp[3:4], axis=-1, keepdims=True)) + lam_init)
    outs = []
    for m in range(2):
        acc, l = _softmax_pv(q_ref[:, m * HEAD_DIM:(m + 1) * HEAD_DIM],
                             _chunks(k_ref, m * HEAD_DIM, HEAD_DIM, tk, ck_ref),
                             _chunks(v_ref, 0, B_V_DIM, tk, cv_ref), s_scr, rb)
        outs.append(acc / l)
    o = outs[0] - lam * outs[1]
    o_ref[...] = (_rms(o, sg_ref[...]) * (1.0 - lam_init)).astype(BF16)


def _attention(qkv, nb, caches, lam_p, subln_g, lam_init, tq_a, tq_b, tk):
    r = qkv.shape[0]
    length = r // nb
    has_cache = caches is not None
    lk = length + (caches[0].shape[0] // nb if has_cache else 0)
    tk = min(tk, length)
    ka0 = A_HEADS
    va0 = ka0 + A_KV_HEADS
    qb0 = (A_HEADS + 2 * A_KV_HEADS) * HEAD_DIM // B_V_DIM
    kb0 = qb0 + B_HEADS
    vb0 = kb0 + B_HEADS

    nq = length // tq_a
    wq = A_GROUP * HEAD_DIM
    in_specs = [pl.BlockSpec((tq_a, wq), lambda b, h, i: (b * nq + i, h)),
                pl.BlockSpec((length, HEAD_DIM), lambda b, h, i: (b, ka0 + h)),
                pl.BlockSpec((length, HEAD_DIM), lambda b, h, i: (b, va0 + h))]
    args = [qkv, qkv, qkv]
    if has_cache:
        pc = caches[0].shape[0] // nb
        in_specs += [pl.BlockSpec((pc, HEAD_DIM), lambda b, h, i: (b, h))] * 2
        args += [caches[0], caches[1]]
    rb_a = min(256, A_GROUP * tq_a)
    oa = pl.pallas_call(
        functools.partial(_gqa_kernel, has_cache=has_cache, tk=tk, rb=rb_a),
        grid=(nb, A_KV_HEADS, nq),
        in_specs=in_specs,
        out_specs=pl.BlockSpec((tq_a, wq), lambda b, h, i: (b * nq + i, h)),
        out_shape=jax.ShapeDtypeStruct((r, A_HEADS * HEAD_DIM), BF16),
        scratch_shapes=[pltpu.VMEM((A_GROUP * tq_a, lk), F32)],
        compiler_params=_cparams(3),
    )(*args)

    nq = length // tq_b
    in_specs = [pl.BlockSpec((tq_b, B_V_DIM), lambda b, h, i: (b * nq + i, qb0 + h)),
                pl.BlockSpec((length, B_V_DIM), lambda b, h, i: (b, kb0 + h)),
                pl.BlockSpec((length, B_V_DIM), lambda b, h, i: (b, vb0 + h))]
    args = [qkv, qkv, qkv]
    if has_cache:
        in_specs += [pl.BlockSpec((pc, B_V_DIM), lambda b, h, i: (b, h))] * 2
        args += [caches[2], caches[3]]
    in_specs += [pl.BlockSpec((4, HEAD_DIM), lambda b, h, i: (0, 0)),
                 pl.BlockSpec((1, B_V_DIM), lambda b, h, i: (0, 0))]
    args += [lam_p, subln_g]
    ob = pl.pallas_call(
        functools.partial(_diff_kernel, has_cache=has_cache, tk=tk, rb=min(256, tq_b), lam_init=lam_init),
        grid=(nb, B_HEADS, nq),
        in_specs=in_specs,
        out_specs=pl.BlockSpec((tq_b, B_V_DIM), lambda b, h, i: (b * nq + i, h)),
        out_shape=jax.ShapeDtypeStruct((r, B_HEADS * B_V_DIM), BF16),
        scratch_shapes=[pltpu.VMEM((tq_b, lk), F32)],
        compiler_params=_cparams(3),
    )(*args)
    return oa, ob


def _outproj_kernel(*refs, n_in):
    o_refs = refs[:n_in]
    w_refs = refs[n_in:2 * n_in]
    x_ref, g_ref, gate_ref, out_ref = refs[2 * n_in:]
    y = None
    for o_ref, w_ref in zip(o_refs, w_refs):
        t = jnp.dot(o_ref[...], w_ref[...], preferred_element_type=F32)
        y = t if y is None else y + t
    out_ref[...] = x_ref[...] + gate_ref[0] * _rms(y, g_ref[...])


def _outproj(os_, ws, x, g, mod, row_fn, tm):
    r, d = x.shape
    n_in = len(os_)
    in_specs = ([pl.BlockSpec((tm, o.shape[1]), lambda i: (i, 0)) for o in os_]
                + [pl.BlockSpec(w.shape, lambda i: (0, 0), pipeline_mode=pl.Buffered(1)) for w in ws]
                + [pl.BlockSpec((tm, d), lambda i: (i, 0)),
                   pl.BlockSpec((1, d), lambda i: (0, 0)),
                   _mod_spec(d, 2, row_fn)])
    return pl.pallas_call(
        functools.partial(_outproj_kernel, n_in=n_in),
        grid=(r // tm,),
        in_specs=in_specs,
        out_specs=pl.BlockSpec((tm, d), lambda i: (i, 0)),
        out_shape=jax.ShapeDtypeStruct((r, d), F32),
        compiler_params=_cparams(1),
    )(*os_, *ws, x, g, mod)


def _mlp_kernel(x_ref, g2_ref, sh_ref, sc_ref, gate_ref, g3_ref, w1_ref, w2_ref, out_ref, n_scr):
    f = pl.program_id(1)

    @pl.when(f == 0)
    def _():
        n = _rms(x_ref[...], g2_ref[...]) * (1.0 + sc_ref[0]) + sh_ref[0]
        n_scr[...] = n.astype(BF16)
        out_ref[...] = jnp.zeros_like(out_ref)

    h = jnp.maximum(jnp.dot(n_scr[...], w1_ref[...], preferred_element_type=F32), 0.0)
    hb = (h * h).astype(BF16)
    d = out_ref.shape[1]
    for c0 in range(0, d, MLP_ACC_CHUNK):
        out_ref[:, c0:c0 + MLP_ACC_CHUNK] += jnp.dot(hb, w2_ref[:, c0:c0 + MLP_ACC_CHUNK],
                                                     preferred_element_type=F32)

    @pl.when(f == pl.num_programs(1) - 1)
    def _():
        out_ref[...] = x_ref[...] + gate_ref[0] * _rms(out_ref[...], g3_ref[...])


def _mlp(x, g2, g3, mod, row_fn, w1, w2, tm, tf):
    r, d = x.shape
    dff = w1.shape[1]
    return pl.pallas_call(
        _mlp_kernel,
        grid=(r // tm, dff // tf),
        in_specs=[pl.BlockSpec((tm, d), lambda i, f: (i, 0)),
                  pl.BlockSpec((1, d), lambda i, f: (0, 0)),
                  _mod_spec(d, 3, row_fn), _mod_spec(d, 4, row_fn), _mod_spec(d, 5, row_fn),
                  pl.BlockSpec((1, d), lambda i, f: (0, 0)),
                  pl.BlockSpec((d, tf), lambda i, f: (0, f)),
                  pl.BlockSpec((tf, d), lambda i, f: (f, 0))],
        out_specs=pl.BlockSpec((tm, d), lambda i, f: (i, 0)),
        out_shape=jax.ShapeDtypeStruct((r, d), F32),
        scratch_shapes=[pltpu.VMEM((tm, d), BF16)],
        compiler_params=_cparams(2),
    )(x, g2, mod, mod, mod, g3, w1, w2)


def _ssm_in_kernel(x_ref, g_ref, sh_ref, sc_ref, w_ref, u_ref, ub_ref, n_scr):
    @pl.when(pl.program_id(1) == 0)
    def _():
        n = _rms(x_ref[...], g_ref[...]) * (1.0 + sc_ref[0]) + sh_ref[0]
        n_scr[...] = n.astype(BF16)

    u = jnp.dot(n_scr[...], w_ref[...], preferred_element_type=F32)
    u_ref[...] = u
    ub_ref[...] = u.astype(BF16)


def _ssm_in(x, g, mod, row_fn, w, tm, tn):
    r, d = x.shape
    n = w.shape[1]
    return pl.pallas_call(
        _ssm_in_kernel,
        grid=(r // tm, n // tn),
        in_specs=[pl.BlockSpec((tm, d), lambda i, j: (i, 0)),
                  pl.BlockSpec((1, d), lambda i, j: (0, 0)),
                  _mod_spec(d, 0, row_fn), _mod_spec(d, 1, row_fn),
                  pl.BlockSpec((d, tn), lambda i, j: (0, j))],
        out_specs=[pl.BlockSpec((tm, tn), lambda i, j: (i, j))] * 2,
        out_shape=[jax.ShapeDtypeStruct((r, n), F32), jax.ShapeDtypeStruct((r, n), BF16)],
        scratch_shapes=[pltpu.VMEM((tm, d), BF16)],
        compiler_params=_cparams(2),
    )(x, g, mod, mod, w)


def _ssm_params(a_re, a_im, log_dt, b_re, b_im, c_re, c_im, seg_chunks):
    t = SSM_CHUNK
    lam = lax.complex(a_re.astype(F32), a_im.astype(F32))
    dt = jnp.exp(log_dt.astype(F32))[..., None]
    ldt = lam * dt
    a_bar = jnp.exp(ldt)
    k = jnp.arange(t + 1, dtype=F32)
    ap = jnp.exp(ldt[None] * k[:, None, None, None])
    bb = ((a_bar - 1.0) / lam)[..., None] * lax.complex(b_re.astype(F32), b_im.astype(F32))
    cc = lax.complex(c_re.astype(F32), c_im.astype(F32))
    g = lam.shape[1]

    wf = ap[t - 1 - jnp.arange(t), 0][..., None] * bb[0][None]
    wb = ap[jnp.arange(t), 1][..., None] * bb[1][None]
    to_rows = lambda w: jnp.transpose(w, (1, 0, 3, 2)).reshape(g, t * S5_GROUP_CH, S5_STATE)
    w_in = jnp.concatenate([to_rows(wf.real), to_rows(wb.real), to_rows(wf.imag), to_rows(wb.imag)], axis=-1)

    taps = jnp.einsum("dgcn,kdgn,dgne->dkgce", cc, ap[:t], bb).real
    i_idx = jnp.arange(t)[None, :]
    s_idx = jnp.arange(t)[:, None]
    diff = i_idx - s_idx
    kf = jnp.where((diff >= 0)[..., None, None, None], taps[0][jnp.clip(diff, 0, t - 1)], 0.0)
    kb = jnp.where((diff <= 0)[..., None, None, None], taps[1][jnp.clip(-diff, 0, t - 1)], 0.0)
    toep = jnp.transpose(kf + kb, (2, 0, 4, 1, 3)).reshape(g, t * S5_GROUP_CH, t * S5_GROUP_CH)

    cf = cc[0][None] * ap[1 + jnp.arange(t), 0][:, :, None, :]
    cb = cc[1][None] * ap[t - jnp.arange(t), 1][:, :, None, :]
    to_cols = lambda w: jnp.transpose(w, (1, 3, 0, 2)).reshape(g, S5_STATE, t * S5_GROUP_CH)
    c_out = jnp.concatenate([to_cols(cf.real), to_cols(cb.real), to_cols(-cf.imag), to_cols(-cb.imag)], axis=1)
    w_out = jnp.concatenate([toep, c_out], axis=1)

    def decay(power):
        a = jnp.exp(ldt * float(power))
        return jnp.concatenate([a[0].real, a[1].real, a[0].imag, a[1].imag], axis=-1)[:, None, :]

    return w_in.astype(BF16), w_out.astype(BF16), decay(t), decay(t * seg_chunks)


def _cmul_add(ar, ai, h_re, h_im, s_re, s_im):
    return ar * h_re - ai * h_im + s_re, ar * h_im + ai * h_re + s_im


def _chunk_scan(s_scr, t_scr, rows, nc, ar, ai, init, store):
    half = SSM_LANES // 2
    fwd = lax.broadcasted_iota(jnp.int32, (rows, half), 1) < S5_STATE
    h_re, h_im = init
    for k in range(nc):
        rf, rbk = k * rows, (nc - 1 - k) * rows
        s_re = jnp.where(fwd, s_scr[rf:rf + rows, 0:half], s_scr[rbk:rbk + rows, 0:half])
        s_im = jnp.where(fwd, s_scr[rf:rf + rows, half:], s_scr[rbk:rbk + rows, half:])
        if store:
            t_scr[rf:rf + rows, 0:half] = h_re
            t_scr[rf:rf + rows, half:] = h_im
        h_re, h_im = _cmul_add(ar, ai, h_re, h_im, s_re, s_im)
    return h_re, h_im


def _entering_states(t_scr, hp_scr, rows, nc):
    half = SSM_LANES // 2
    fwd = lax.broadcasted_iota(jnp.int32, (rows, SSM_LANES), 1) % half < S5_STATE
    for k in range(nc):
        rf, rbk = k * rows, (nc - 1 - k) * rows
        hp_scr[rf:rf + rows, :] = jnp.where(fwd, t_scr[rf:rf + rows, :], t_scr[rbk:rbk + rows, :]).astype(BF16)


def _ssm_core_kernel(up_ref, us_ref, win_ref, wout_ref, a_ref, aseg_ref, h0_ref, yp_ref, ys_ref, hfin_ref,
                     s_scr, t_scr, hp_scr, *, p_rows, nsub):
    half = SSM_LANES // 2
    kdim = SSM_CHUNK * S5_GROUP_CH
    ar, ai = a_ref[0][:, 0:half], a_ref[0][:, half:]

    def emit(u_ref, y_ref, rows):
        y_ref[0] = (jnp.dot(u_ref[0], wout_ref[0, 0:kdim, :], preferred_element_type=F32)
                    + jnp.dot(hp_scr[0:rows, :], wout_ref[0, kdim:, :], preferred_element_type=F32))

    rp = up_ref.shape[1]
    s_scr[0:rp, :] = jnp.dot(up_ref[0], win_ref[0], preferred_element_type=F32)
    zero = jnp.zeros((p_rows, half), F32)
    h_re, h_im = _chunk_scan(s_scr, t_scr, p_rows, rp // p_rows, ar, ai, (zero, zero), True)
    hfin_ref[0, :, 0:half] = h_re
    hfin_ref[0, :, half:] = h_im
    _entering_states(t_scr, hp_scr, p_rows, rp // p_rows)
    emit(up_ref, yp_ref, rp)

    rs = us_ref.shape[1]
    ncl = rs // SSM_PSEQ
    s_scr[0:rs, :] = jnp.dot(us_ref[0], win_ref[0], preferred_element_type=F32)
    zero = jnp.zeros((SSM_PSEQ, half), F32)
    f_re, f_im = _chunk_scan(s_scr, t_scr, SSM_PSEQ, ncl, ar, ai, (zero, zero), False)
    gr, gi = aseg_ref[0][:, 0:half], aseg_ref[0][:, half:]
    h0_re, h0_im = h0_ref[0][:, 0:half], h0_ref[0][:, half:]
    fwd = lax.broadcasted_iota(jnp.int32, (SSM_PSEQ, half), 1) < S5_STATE
    seg = lax.broadcasted_iota(jnp.int32, (SSM_PSEQ, half), 0) % nsub
    carried = jnp.where(fwd, seg, nsub - 1 - seg) != 0
    e_re, e_im = h0_re, h0_im
    for _ in range(nsub - 1):
        x_re, x_im = _cmul_add(gr, gi, e_re, e_im, f_re, f_im)
        shift = lambda x: jnp.where(fwd, pltpu.roll(x, 1, axis=0), pltpu.roll(x, SSM_PSEQ - 1, axis=0))
        e_re = h0_re + jnp.where(carried, shift(x_re), 0.0)
        e_im = h0_im + jnp.where(carried, shift(x_im), 0.0)
    _chunk_scan(s_scr, t_scr, SSM_PSEQ, ncl, ar, ai, (e_re, e_im), True)
    _entering_states(t_scr, hp_scr, SSM_PSEQ, ncl)
    emit(us_ref, ys_ref, rs)


def _ssm_core(ug_p, ug_s, w_in, w_out, a_chunk, a_seg, h0, p_rows, nsub):
    g, rp, kdim = ug_p.shape
    rs = ug_s.shape[1]
    rmax = max(rp, rs)
    blk = lambda a: pl.BlockSpec((1,) + a.shape[1:], lambda i: (i, 0, 0))
    args = (ug_p, ug_s, w_in, w_out, a_chunk, a_seg, h0)
    return pl.pallas_call(
        functools.partial(_ssm_core_kernel, p_rows=p_rows, nsub=nsub),
        grid=(g,),
        in_specs=[blk(a) for a in args],
        out_specs=[pl.BlockSpec((1, rp, kdim), lambda i: (i, 0, 0)),
                   pl.BlockSpec((1, rs, kdim), lambda i: (i, 0, 0)),
                   pl.BlockSpec((1, p_rows, SSM_LANES), lambda i: (i, 0, 0))],
        out_shape=[jax.ShapeDtypeStruct((g, rp, kdim), F32), jax.ShapeDtypeStruct((g, rs, kdim), F32),
                   jax.ShapeDtypeStruct((g, p_rows, SSM_LANES), F32)],
        scratch_shapes=[pltpu.VMEM((rmax, SSM_LANES), F32), pltpu.VMEM((rmax, SSM_LANES), F32),
                        pltpu.VMEM((rmax, SSM_LANES), BF16)],
        compiler_params=_cparams(1),
    )(*args)


def _glu_kernel(y_ref, u_ref, d_ref, w_ref, z_ref):
    v = y_ref[...] + d_ref[...] * u_ref[...]
    z = 0.5 * v * (1.0 + jnp.tanh(math.sqrt(2.0 / math.pi) * (v + 0.044715 * (v * v * v))))
    gl = jnp.dot(z.astype(BF16), w_ref[...], preferred_element_type=F32)
    z_ref[...] = (z * _sigmoid(gl)).astype(BF16)


def _glu(y, u, d, w, tm):
    r, dm = y.shape
    return pl.pallas_call(
        _glu_kernel,
        grid=(r // tm,),
        in_specs=[pl.BlockSpec((tm, dm), lambda i: (i, 0)),
                  pl.BlockSpec((tm, dm), lambda i: (i, 0)),
                  pl.BlockSpec((1, dm), lambda i: (0, 0)),
                  pl.BlockSpec(w.shape, lambda i: (0, 0), pipeline_mode=pl.Buffered(1))],
        out_specs=pl.BlockSpec((tm, dm), lambda i: (i, 0)),
        out_shape=jax.ShapeDtypeStruct((r, dm), BF16),
        compiler_params=_cparams(1),
    )(y, u, d, w)


def _rope_tables(length):
    rows = length // GRID_W
    row = jnp.repeat(jnp.arange(rows, dtype=F32), GRID_W)
    col = jnp.tile(jnp.arange(GRID_W, dtype=F32), rows)
    inv = ROPE_BASE ** (-jnp.arange(ROPE_PAIRS_PER_AXIS, dtype=F32) / ROPE_PAIRS_PER_AXIS)
    ang = jnp.concatenate([row[:, None] * inv, col[:, None] * inv], axis=-1)
    cos, sin = jnp.cos(ang), jnp.sin(ang)
    return jnp.concatenate([cos, cos], axis=-1), jnp.concatenate([-sin, sin], axis=-1)


def kernel(x_prompt, x_sample, c, cache_a_k, cache_a_v, cache_b_k, cache_b_v, state_ssm, c_ctx, ada_w, ada_b, norm_g, mlp_w1, mlp_w2, attn_w_in, attn_w_out, attn_qk_norm, diff_lambda, diff_subln, ssm_w_in, ssm_a_re, ssm_a_im, ssm_log_dt, ssm_b_re, ssm_b_im, ssm_c_re, ssm_c_im, ssm_d, ssm_glu_w, ssm_w_out):
    bp, lp, d = x_prompt.shape
    bs, ls, _ = x_sample.shape
    depth = ada_w.shape[0]
    past = cache_a_k.shape[2]
    assert bs + 1 <= 8 and bp % SSM_PSEQ == 0 and SSM_PSEQ % bs == 0
    nsub = SSM_PSEQ // bs
    assert lp % SSM_CHUNK == 0 and ls % (SSM_CHUNK * nsub) == 0 and ls % GRID_W == 0

    cond8 = jnp.zeros((8, d), F32).at[0].set(c_ctx).at[1:1 + bs].set(c)
    mods = _modulation(cond8, ada_w, ada_b).reshape(depth, 8 * N_MOD, 1, d)

    xp = x_prompt.reshape(bp * lp, d)
    xs = x_sample.reshape(bs * ls, d)
    tm = min(512, bp * lp, ls)
    tm_big = min(1024, bp * lp, ls)
    row_p = lambda i: 0
    row_s = lambda i: 1 + i // (ls // tm)
    row_s_big = lambda i: 1 + i // (ls // tm_big)
    new_ak = new_av = new_bk = new_bv = new_ssm = None
    for l in range(depth):
        mod = mods[l]
        g = norm_g[l][:, None, :]
        i = l // 2
        if l % 2 == 0:
            lam_init = 0.8 - 0.6 * math.exp(-0.3 * l)
            w_in = attn_w_in[i].astype(BF16)
            w_out = attn_w_out[i].astype(BF16)
            n_a = A_HEADS * HEAD_DIM
            w_outs = [w_out[:n_a], w_out[n_a:]]
            qkv, ka, va, kb, vb = _qkv_proj(xp, g[0], mod, row_p, w_in, attn_qk_norm[i], None, True, tm_big)
            new_ak = ka.reshape(bp, 1, lp, A_KV_HEADS, HEAD_DIM)
            new_av = va.reshape(bp, 1, lp, A_KV_HEADS, HEAD_DIM)
            new_bk = kb.reshape(bp, 1, lp, B_HEADS, 2, HEAD_DIM)
            new_bv = vb.reshape(bp, 1, lp, B_HEADS, B_V_DIM)
            oa, ob = _attention(qkv, bp, None, diff_lambda[i], diff_subln[i][None, :], lam_init,
                                tq_a=lp, tq_b=lp, tk=512)
            xp = _outproj([oa, ob], w_outs, xp, g[1], mod, row_p, tm)
            (qkv,) = _qkv_proj(xs, g[0], mod, row_s_big, w_in, attn_qk_norm[i], _rope_tables(ls), False, tm_big)
            caches = (cache_a_k[:, i].reshape(bs * past, A_KV_HEADS * HEAD_DIM),
                      cache_a_v[:, i].reshape(bs * past, A_KV_HEADS * HEAD_DIM),
                      cache_b_k[:, i].reshape(bs * past, B_HEADS * 2 * HEAD_DIM),
                      cache_b_v[:, i].reshape(bs * past, B_HEADS * B_V_DIM))
            oa, ob = _attention(qkv, bs, caches, diff_lambda[i], diff_subln[i][None, :], lam_init,
                                tq_a=256, tq_b=512, tk=512)
            xs = _outproj([oa, ob], w_outs, xs, g[1], mod, row_s, tm)
        else:
            w_in = ssm_w_in[i].astype(BF16)
            u_p, ub_p = _ssm_in(xp, g[0], mod, row_p, w_in, tm_big, 1024)
            u_s, ub_s = _ssm_in(xs, g[0], mod, row_s_big, w_in, tm_big, 1024)
            ng = d // S5_GROUP_CH
            ncp, ncl = lp // SSM_CHUNK, ls // SSM_CHUNK // nsub
            kd = SSM_CHUNK * S5_GROUP_CH
            ug_p = jnp.transpose(ub_p.reshape(bp, ncp, SSM_CHUNK, ng, S5_GROUP_CH), (3, 1, 0, 2, 4))
            ug_p = ug_p.reshape(ng, ncp * bp, kd)
            ug_s = jnp.transpose(ub_s.reshape(bs, nsub, ncl, SSM_CHUNK, ng, S5_GROUP_CH), (4, 2, 0, 1, 3, 5))
            ug_s = ug_s.reshape(ng, ncl * SSM_PSEQ, kd)
            wi, wo, a_chunk, a_seg = _ssm_params(ssm_a_re[i], ssm_a_im[i], ssm_log_dt[i], ssm_b_re[i], ssm_b_im[i],
                                                 ssm_c_re[i], ssm_c_im[i], ncl)
            st = state_ssm[:, i].astype(F32)
            h0 = jnp.zeros((ng, bs, nsub, 4, S5_STATE), F32)
            h0 = h0.at[:, :, 0, 0].set(jnp.transpose(st[:, 0, :, :, 0], (1, 0, 2)))
            h0 = h0.at[:, :, 0, 2].set(jnp.transpose(st[:, 0, :, :, 1], (1, 0, 2)))
            h0 = h0.at[:, :, nsub - 1, 1].set(jnp.transpose(st[:, 1, :, :, 0], (1, 0, 2)))
            h0 = h0.at[:, :, nsub - 1, 3].set(jnp.transpose(st[:, 1, :, :, 1], (1, 0, 2)))
            h0 = h0.reshape(ng, SSM_PSEQ, SSM_LANES)
            yg_p, yg_s, hfin = _ssm_core(ug_p, ug_s, wi, wo, a_chunk, a_seg, h0, bp, nsub)
            y_p = jnp.transpose(yg_p.reshape(ng, ncp, bp, SSM_CHUNK, S5_GROUP_CH), (2, 1, 3, 0, 4)).reshape(bp * lp, d)
            y_s = jnp.transpose(yg_s.reshape(ng, ncl, bs, nsub, SSM_CHUNK, S5_GROUP_CH), (2, 3, 1, 4, 0, 5))
            y_s = y_s.reshape(bs * ls, d)
            new_ssm = jnp.transpose(hfin.reshape(ng, bp, 2, 2, S5_STATE), (1, 3, 0, 4, 2))
            glu_w = ssm_glu_w[i].astype(BF16)
            w_out = ssm_w_out[i].astype(BF16)
            dvec = ssm_d[i][None, :]
            z_p = _glu(y_p, u_p, dvec, glu_w, tm)
            z_s = _glu(y_s, u_s, dvec, glu_w, tm)
            xp = _outproj([z_p], [w_out], xp, g[1], mod, row_p, tm)
            xs = _outproj([z_s], [w_out], xs, g[1], mod, row_s, tm)
        w1 = mlp_w1[l].astype(BF16)
        w2 = mlp_w2[l].astype(BF16)
        xp = _mlp(xp, g[2], g[3], mod, row_p, w1, w2, tm_big, 512)
        xs = _mlp(xs, g[2], g[3], mod, row_s_big, w1, w2, tm_big, 512)
    return (xp.reshape(bp, lp, d), xs.reshape(bs, ls, d), new_ak, new_av, new_bk, new_bv,
            new_ssm[:, None])
```

```python
import functools
import math

import jax
import jax.numpy as jnp
from jax import lax
from jax.experimental import pallas as pl
from jax.experimental.pallas import tpu as pltpu

F32 = jnp.float32
BF16 = jnp.bfloat16
EPS = 1e-6

HEAD_DIM = 128
A_HEADS = 8
A_KV_HEADS = 2
A_GROUP = A_HEADS // A_KV_HEADS
B_HEADS = 4
B_V_DIM = 2 * HEAD_DIM
GRID_W = 64
ROPE_BASE = 10000.0
ROPE_PAIRS_PER_AXIS = HEAD_DIM // 4
N_MOD = 6
S5_GROUP_CH = 16
S5_STATE = 64

V7X_LANES = 128
V7X_SUBLANES = 8
V7X_VMEM_LIMIT_BYTES = 56 * 1024 * 1024

SSM_CHUNK = V7X_SUBLANES
SSM_TILE_GROUPS = V7X_LANES // S5_GROUP_CH
SSM_KDIM = SSM_CHUNK * S5_GROUP_CH
SSM_LANES = 4 * S5_STATE
SSM_SEGMENTS = 8
SSM_REGROUP_ROWS = 32
SSM_PITCH_PAD = 8

MLP_ACC_CHUNK = 512
QKV_TN = 512


def _cparams(n_axes):
    return pltpu.CompilerParams(dimension_semantics=("arbitrary",) * n_axes,
                                vmem_limit_bytes=V7X_VMEM_LIMIT_BYTES)


def _sigmoid(x):
    return 1.0 / (1.0 + jnp.exp(-x))


def _rms(x, g):
    return x * lax.rsqrt(jnp.mean(x * x, axis=-1, keepdims=True) + EPS) * g


def _mod_spec(d, slot, row_fn):
    return pl.BlockSpec((1, 1, d), lambda i, *_: (row_fn(i) * N_MOD + slot, 0, 0))


def _mod_kernel(c_ref, w_ref, b_ref, o_ref):
    c = c_ref[...]
    s = (c * _sigmoid(c)).astype(BF16)
    o_ref[0] = jnp.dot(s, w_ref[0].astype(BF16), preferred_element_type=F32) + b_ref[0]


def _modulation(cond8, ada_w, ada_b):
    depth, d, n = ada_w.shape
    tn = 1024
    return pl.pallas_call(
        _mod_kernel,
        grid=(depth, n // tn),
        in_specs=[pl.BlockSpec((8, d), lambda l, j: (0, 0)),
                  pl.BlockSpec((1, d, tn), lambda l, j: (l, 0, j)),
                  pl.BlockSpec((1, 1, tn), lambda l, j: (l, 0, j))],
        out_specs=pl.BlockSpec((1, 8, tn), lambda l, j: (l, 0, j)),
        out_shape=jax.ShapeDtypeStruct((depth, 8, n), F32),
        compiler_params=_cparams(2),
    )(cond8, ada_w, ada_b.reshape(depth, 1, n))


def _split_heads(y, fn):
    return jnp.concatenate([fn(y[:, h * HEAD_DIM:(h + 1) * HEAD_DIM]) for h in range(y.shape[1] // HEAD_DIM)], axis=1)


def _qkv_kernel(*refs, rope, caches):
    x_ref, g_ref, sh_ref, sc_ref, w_ref, qkg_ref = refs[:6]
    refs = refs[6:]
    if rope:
        cos_ref, sin_ref = refs[:2]
        refs = refs[2:]
    qkv_ref = refs[0]
    if caches:
        ka_ref, va_ref, kb_ref, vb_ref = refs[1:5]
    n_scr = refs[-1]
    j = pl.program_id(1)

    @pl.when(j == 0)
    def _():
        n = _rms(x_ref[...], g_ref[...]) * (1.0 + sc_ref[0]) + sh_ref[0]
        n_scr[...] = n.astype(BF16)

    def rot(y):
        if not rope:
            return y
        return y * cos_ref[...] + pltpu.roll(y, HEAD_DIM // 2, axis=1) * sin_ref[...]

    def acc():
        return jnp.dot(n_scr[...], w_ref[...], preferred_element_type=F32)

    q_scale = HEAD_DIM ** -0.5
    kv_w = A_KV_HEADS * HEAD_DIM
    b_qa = A_HEADS * HEAD_DIM // QKV_TN
    b_va = b_qa + 2 * kv_w // QKV_TN
    b_qb = b_va + B_HEADS * 2 * HEAD_DIM // QKV_TN
    b_kb = b_qb + B_HEADS * 2 * HEAD_DIM // QKV_TN

    @pl.when(j < b_qa)
    def _():
        qkv_ref[...] = _split_heads(acc(), lambda y: rot(_rms(y, qkg_ref[0:1, :])) * q_scale).astype(BF16)

    @pl.when((j >= b_qa) & (j < b_va))
    def _():
        y = acc()
        kn = _split_heads(y[:, :kv_w], lambda t: _rms(t, qkg_ref[1:2, :]))
        if caches:
            ka_ref[...] = kn
            va_ref[...] = y[:, kv_w:]
        qkv_ref[:, :kv_w] = _split_heads(kn, rot).astype(BF16)
        qkv_ref[:, kv_w:] = y[:, kv_w:].astype(BF16)

    @pl.when((j >= b_va) & (j < b_qb))
    def _():
        qkv_ref[...] = _split_heads(acc(), lambda y: rot(y) * q_scale).astype(BF16)

    @pl.when((j >= b_qb) & (j < b_kb))
    def _():
        y = acc()
        if caches:
            kb_ref[...] = y
        qkv_ref[...] = _split_heads(y, rot).astype(BF16)

    @pl.when(j >= b_kb)
    def _():
        y = acc()
        if caches:
            vb_ref[...] = y
        qkv_ref[...] = y.astype(BF16)


def _qkv_proj(x, g, mod, row_fn, w, qk_g, rope_tabs, caches, tm):
    r, d = x.shape
    n = w.shape[1]
    tn = QKV_TN
    kv_w = A_KV_HEADS * HEAD_DIM
    assert 2 * kv_w == tn and r % tm == 0 and n % tn == 0
    assert not (caches and rope_tabs is not None)
    in_specs = [pl.BlockSpec((tm, d), lambda i, j: (i, 0)),
                pl.BlockSpec((1, d), lambda i, j: (0, 0)),
                _mod_spec(d, 0, row_fn), _mod_spec(d, 1, row_fn),
                pl.BlockSpec((d, tn), lambda i, j: (0, j)),
                pl.BlockSpec((2, HEAD_DIM), lambda i, j: (0, 0))]
    args = [x, g, mod, mod, w, qk_g]
    if rope_tabs is not None:
        nblk = rope_tabs[0].shape[0] // tm
        in_specs += [pl.BlockSpec((tm, HEAD_DIM), lambda i, j: (i % nblk, 0))] * 2
        args += list(rope_tabs)
    out_specs = [pl.BlockSpec((tm, tn), lambda i, j: (i, j))]
    out_shape = [jax.ShapeDtypeStruct((r, n), BF16)]
    if caches:
        b_kb = (A_HEADS + 2 * A_KV_HEADS + 2 * B_HEADS) * HEAD_DIM // tn
        nb = B_HEADS * 2 * HEAD_DIM // tn
        out_specs += [pl.BlockSpec((tm, kv_w), lambda i, j: (i, 0)),
                      pl.BlockSpec((tm, kv_w), lambda i, j: (i, 0)),
                      pl.BlockSpec((tm, tn), lambda i, j: (i, jnp.clip(j - b_kb, 0, nb - 1))),
                      pl.BlockSpec((tm, tn), lambda i, j: (i, jnp.clip(j - b_kb - nb, 0, nb - 1)))]
        out_shape += [jax.ShapeDtypeStruct((r, kv_w), F32), jax.ShapeDtypeStruct((r, kv_w), F32),
                      jax.ShapeDtypeStruct((r, nb * tn), F32), jax.ShapeDtypeStruct((r, nb * tn), F32)]
    return pl.pallas_call(
        functools.partial(_qkv_kernel, rope=rope_tabs is not None, caches=caches),
        grid=(r // tm, n // tn),
        in_specs=in_specs, out_specs=out_specs, out_shape=out_shape,
        scratch_shapes=[pltpu.VMEM((tm, d), BF16)],
        compiler_params=_cparams(2),
    )(*args)


def _softmax_pv(q, k_chunks, v_chunks, s_scr, rb):
    m_rows = q.shape[0]
    accs, ls = [], []
    for r0 in range(0, m_rows, rb):
        qr = q[r0:r0 + rb]
        m = jnp.full((rb, 1), -jnp.inf, F32)
        off = 0
        for kc in k_chunks:
            k = kc()
            s = lax.dot_general(qr, k, (((1,), (1,)), ((), ())), preferred_element_type=F32)
            s_scr[r0:r0 + rb, off:off + k.shape[0]] = s
            m = jnp.maximum(m, jnp.max(s, axis=-1, keepdims=True))
            off += k.shape[0]
        l = jnp.zeros((rb, 1), F32)
        acc = None
        off = 0
        for vc in v_chunks:
            v = vc()
            p = jnp.exp(s_scr[r0:r0 + rb, off:off + v.shape[0]] - m)
            l = l + jnp.sum(p, axis=-1, keepdims=True)
            pv = jnp.dot(p.astype(BF16), v, preferred_element_type=F32)
            acc = pv if acc is None else acc + pv
            off += v.shape[0]
        accs.append(acc)
        ls.append(l)
    return jnp.concatenate(accs, axis=0), jnp.concatenate(ls, axis=0)


def _chunks(ref, c0, width, tk, cache_ref=None):
    length = ref.shape[0]
    out = [(lambda s=s: ref[s:s + tk, c0:c0 + width]) for s in range(0, length, tk)]
    if cache_ref is not None:
        out.append(lambda: cache_ref[:, c0:c0 + width].astype(BF16))
    return out


def _gqa_kernel(*refs, has_cache, tk, rb):
    if has_cache:
        q_ref, k_ref, v_ref, ck_ref, cv_ref, o_ref, s_scr = refs
    else:
        q_ref, k_ref, v_ref, o_ref, s_scr = refs
        ck_ref = cv_ref = None
    tq = q_ref.shape[0]
    q = jnp.concatenate([q_ref[:, g * HEAD_DIM:(g + 1) * HEAD_DIM] for g in range(A_GROUP)], axis=0)
    acc, l = _softmax_pv(q, _chunks(k_ref, 0, HEAD_DIM, tk, ck_ref), _chunks(v_ref, 0, HEAD_DIM, tk, cv_ref),
                         s_scr, rb)
    o = acc / l
    o_ref[...] = jnp.concatenate([o[g * tq:(g + 1) * tq] for g in range(A_GROUP)], axis=1).astype(BF16)


def _diff_kernel(*refs, has_cache, tk, rb, lam_init):
    if has_cache:
        q_ref, k_ref, v_ref, ck_ref, cv_ref, lam_ref, sg_ref, o_ref, s_scr = refs
    else:
        q_ref, k_ref, v_ref, lam_ref, sg_ref, o_ref, s_scr = refs
        ck_ref = cv_ref = None
    lp = lam_ref[...]
    lam = (jnp.exp(jnp.sum(lp[0:1] * lp[1:2], axis=-1, keepdims=True))
           - jnp.exp(jnp.sum(lp[2:3] * lp[3:4], axis=-1, keepdims=True)) + lam_init)
    outs = []
    for m in range(2):
        acc, l = _softmax_pv(q_ref[:, m * HEAD_DIM:(m + 1) * HEAD_DIM],
                             _chunks(k_ref, m * HEAD_DIM, HEAD_DIM, tk, ck_ref),
                             _chunks(v_ref, 0, B_V_DIM, tk, cv_ref), s_scr, rb)
        outs.append(acc / l)
    o = outs[0] - lam * outs[1]
    o_ref[...] = (_rms(o, sg_ref[...]) * (1.0 - lam_init)).astype(BF16)


def _attention(qkv, nb, caches, lam_p, subln_g, lam_init, tq_a, tq_b, tk):
    r = qkv.shape[0]
    length = r // nb
    has_cache = caches is not None
    lk = length + (caches[0].shape[0] // nb if has_cache else 0)
    tk = min(tk, length)
    ka0 = A_HEADS
    va0 = ka0 + A_KV_HEADS
    qb0 = (A_HEADS + 2 * A_KV_HEADS) * HEAD_DIM // B_V_DIM
    kb0 = qb0 + B_HEADS
    vb0 = kb0 + B_HEADS

    nq = length // tq_a
    wq = A_GROUP * HEAD_DIM
    in_specs = [pl.BlockSpec((tq_a, wq), lambda b, h, i: (b * nq + i, h)),
                pl.BlockSpec((length, HEAD_DIM), lambda b, h, i: (b, ka0 + h)),
                pl.BlockSpec((length, HEAD_DIM), lambda b, h, i: (b, va0 + h))]
    args = [qkv, qkv, qkv]
    if has_cache:
        pc = caches[0].shape[0] // nb
        in_specs += [pl.BlockSpec((pc, HEAD_DIM), lambda b, h, i: (b, h))] * 2
        args += [caches[0], caches[1]]
    rb_a = min(256, A_GROUP * tq_a)
    oa = pl.pallas_call(
        functools.partial(_gqa_kernel, has_cache=has_cache, tk=tk, rb=rb_a),
        grid=(nb, A_KV_HEADS, nq),
        in_specs=in_specs,
        out_specs=pl.BlockSpec((tq_a, wq), lambda b, h, i: (b * nq + i, h)),
        out_shape=jax.ShapeDtypeStruct((r, A_HEADS * HEAD_DIM), BF16),
        scratch_shapes=[pltpu.VMEM((A_GROUP * tq_a, lk), F32)],
        compiler_params=_cparams(3),
    )(*args)

    nq = length // tq_b
    in_specs = [pl.BlockSpec((tq_b, B_V_DIM), lambda b, h, i: (b * nq + i, qb0 + h)),
                pl.BlockSpec((length, B_V_DIM), lambda b, h, i: (b, kb0 + h)),
                pl.BlockSpec((length, B_V_DIM), lambda b, h, i: (b, vb0 + h))]
    args = [qkv, qkv, qkv]
    if has_cache:
        in_specs += [pl.BlockSpec((pc, B_V_DIM), lambda b, h, i: (b, h))] * 2
        args += [caches[2], caches[3]]
    in_specs += [pl.BlockSpec((4, HEAD_DIM), lambda b, h, i: (0, 0)),
                 pl.BlockSpec((1, B_V_DIM), lambda b, h, i: (0, 0))]
    args += [lam_p, subln_g]
    ob = pl.pallas_call(
        functools.partial(_diff_kernel, has_cache=has_cache, tk=tk, rb=min(256, tq_b), lam_init=lam_init),
        grid=(nb, B_HEADS, nq),
        in_specs=in_specs,
        out_specs=pl.BlockSpec((tq_b, B_V_DIM), lambda b, h, i: (b * nq + i, h)),
        out_shape=jax.ShapeDtypeStruct((r, B_HEADS * B_V_DIM), BF16),
        scratch_shapes=[pltpu.VMEM((tq_b, lk), F32)],
        compiler_params=_cparams(3),
    )(*args)
    return oa, ob


def _outproj_kernel(*refs, n_in):
    o_refs = refs[:n_in]
    w_refs = refs[n_in:2 * n_in]
    x_ref, g_ref, gate_ref, out_ref = refs[2 * n_in:]
    y = None
    for o_ref, w_ref in zip(o_refs, w_refs):
        t = jnp.dot(o_ref[...], w_ref[...], preferred_element_type=F32)
        y = t if y is None else y + t
    out_ref[...] = x_ref[...] + gate_ref[0] * _rms(y, g_ref[...])


def _outproj(os_, w, x, g, mod, row_fn, tm):
    r, d = x.shape
    n_in = len(os_)
    kw = os_[0].shape[1]
    assert all(o.shape[1] == kw for o in os_) and w.shape == (n_in * kw, d)
    in_specs = ([pl.BlockSpec((tm, kw), lambda i: (i, 0)) for _ in os_]
                + [pl.BlockSpec((kw, d), lambda i, k=k: (k, 0), pipeline_mode=pl.Buffered(1)) for k in range(n_in)]
                + [pl.BlockSpec((tm, d), lambda i: (i, 0)),
                   pl.BlockSpec((1, d), lambda i: (0, 0)),
                   _mod_spec(d, 2, row_fn)])
    return pl.pallas_call(
        functools.partial(_outproj_kernel, n_in=n_in),
        grid=(r // tm,),
        in_specs=in_specs,
        out_specs=pl.BlockSpec((tm, d), lambda i: (i, 0)),
        out_shape=jax.ShapeDtypeStruct((r, d), F32),
        compiler_params=_cparams(1),
    )(*os_, *([w] * n_in), x, g, mod)


def _mlp_kernel(x_ref, g2_ref, sh_ref, sc_ref, gate_ref, g3_ref, w1_ref, w2_ref, out_ref, n_scr):
    f = pl.program_id(1)

    @pl.when(f == 0)
    def _():
        n = _rms(x_ref[...], g2_ref[...]) * (1.0 + sc_ref[0]) + sh_ref[0]
        n_scr[...] = n.astype(BF16)
        out_ref[...] = jnp.zeros_like(out_ref)

    h = jnp.maximum(jnp.dot(n_scr[...], w1_ref[...], preferred_element_type=F32), 0.0)
    hb = (h * h).astype(BF16)
    d = out_ref.shape[1]
    for c0 in range(0, d, MLP_ACC_CHUNK):
        out_ref[:, c0:c0 + MLP_ACC_CHUNK] += jnp.dot(hb, w2_ref[:, c0:c0 + MLP_ACC_CHUNK],
                                                     preferred_element_type=F32)

    @pl.when(f == pl.num_programs(1) - 1)
    def _():
        out_ref[...] = x_ref[...] + gate_ref[0] * _rms(out_ref[...], g3_ref[...])


def _mlp(x, g2, g3, mod, row_fn, w1, w2, layer, tm, tf):
    r, d = x.shape
    dff = w1.shape[2]
    return pl.pallas_call(
        _mlp_kernel,
        grid=(r // tm, dff // tf),
        in_specs=[pl.BlockSpec((tm, d), lambda i, f: (i, 0)),
                  pl.BlockSpec((1, d), lambda i, f: (0, 0)),
                  _mod_spec(d, 3, row_fn), _mod_spec(d, 4, row_fn), _mod_spec(d, 5, row_fn),
                  pl.BlockSpec((1, d), lambda i, f: (0, 0)),
                  pl.BlockSpec((None, d, tf), lambda i, f: (layer, 0, f)),
                  pl.BlockSpec((None, tf, d), lambda i, f: (layer, f, 0))],
        out_specs=pl.BlockSpec((tm, d), lambda i, f: (i, 0)),
        out_shape=jax.ShapeDtypeStruct((r, d), F32),
        scratch_shapes=[pltpu.VMEM((tm, d), BF16)],
        compiler_params=_cparams(2),
    )(x, g2, mod, mod, mod, g3, w1, w2)


def _ssm_in_kernel(x_ref, g_ref, sh_ref, sc_ref, w_ref, u_ref, n_scr):
    @pl.when(pl.program_id(1) == 0)
    def _():
        n = _rms(x_ref[...], g_ref[...]) * (1.0 + sc_ref[0]) + sh_ref[0]
        n_scr[...] = n.astype(BF16)

    u_ref[...] = jnp.dot(n_scr[...], w_ref[...], preferred_element_type=F32)


def _ssm_in(x, g, mod, row_fn, w, tm, tn):
    r, d = x.shape
    n = w.shape[1]
    return pl.pallas_call(
        _ssm_in_kernel,
        grid=(r // tm, n // tn),
        in_specs=[pl.BlockSpec((tm, d), lambda i, j: (i, 0)),
                  pl.BlockSpec((1, d), lambda i, j: (0, 0)),
                  _mod_spec(d, 0, row_fn), _mod_spec(d, 1, row_fn),
                  pl.BlockSpec((d, tn), lambda i, j: (0, j))],
        out_specs=pl.BlockSpec((tm, tn), lambda i, j: (i, j)),
        out_shape=jax.ShapeDtypeStruct((r, n), F32),
        scratch_shapes=[pltpu.VMEM((tm, d), BF16)],
        compiler_params=_cparams(2),
    )(x, g, mod, mod, w)


def _ssm_params(a_re, a_im, log_dt, b_re, b_im, c_re, c_im, seg_chunks):
    t, nst, nch = SSM_CHUNK, S5_STATE, S5_GROUP_CH
    hi = lax.Precision.HIGHEST
    ar, ai = a_re.astype(F32), a_im.astype(F32)
    dt = jnp.exp(log_dt.astype(F32))[..., None]
    lr, li = ar * dt, ai * dt
    g = ar.shape[1]

    def power(k):
        mag = jnp.exp(lr * k)
        return mag * jnp.cos(li * k), mag * jnp.sin(li * k)

    pr, pi = power(jnp.arange(t + 1, dtype=F32)[:, None, None, None])
    xr, xi = pr[1] - 1.0, pi[1]
    den = ar * ar + ai * ai
    qr, qi = (xr * ar + xi * ai) / den, (xi * ar - xr * ai) / den
    br, bi = b_re.astype(F32), b_im.astype(F32)
    bbr = qr[..., None] * br - qi[..., None] * bi
    bbi = qr[..., None] * bi + qi[..., None] * br
    ccr, cci = c_re.astype(F32), c_im.astype(F32)
    steps = jnp.arange(t)

    def w_in_dir(d, idx):
        er, ei = pr[idx, d][..., None], pi[idx, d][..., None]
        wr = er * bbr[d][None] - ei * bbi[d][None]
        wi = er * bbi[d][None] + ei * bbr[d][None]
        rows = lambda w: jnp.transpose(w, (1, 0, 3, 2)).reshape(g, t * nch, nst)
        return rows(wr), rows(wi)

    wfr, wfi = w_in_dir(0, t - 1 - steps)
    wbr, wbi = w_in_dir(1, steps)
    w_in = jnp.concatenate([wfr, wbr, wfi, wbi], axis=-1)

    cpr = ccr[None] * pr[:, :, :, None, :] - cci[None] * pi[:, :, :, None, :]
    cpi = ccr[None] * pi[:, :, :, None, :] + cci[None] * pr[:, :, :, None, :]
    taps = (jnp.einsum("kdgcn,dgne->dkgce", cpr[:t], bbr, precision=hi)
            - jnp.einsum("kdgcn,dgne->dkgce", cpi[:t], bbi, precision=hi))
    diff = steps[None, :] - steps[:, None]
    kf = jnp.where((diff >= 0)[..., None, None, None], taps[0][jnp.clip(diff, 0, t - 1)], 0.0)
    kb = jnp.where((diff <= 0)[..., None, None, None], taps[1][jnp.clip(-diff, 0, t - 1)], 0.0)
    toep = jnp.transpose(kf + kb, (2, 0, 4, 1, 3)).reshape(g, t * nch, t * nch)

    cols = lambda w: jnp.transpose(w, (1, 3, 0, 2)).reshape(g, nst, t * nch)
    c_out = jnp.concatenate([cols(cpr[1 + steps, 0]), cols(cpr[t - steps, 1]),
                             cols(-cpi[1 + steps, 0]), cols(-cpi[t - steps, 1])], axis=1)
    w_out = jnp.concatenate([toep, c_out], axis=1)

    def decay(k):
        dr, di = power(float(k))
        return jnp.concatenate([dr[0], dr[1], di[0], di[1]], axis=-1)[:, None, :]

    return w_in.astype(BF16), w_out.astype(BF16), decay(t), decay(t * seg_chunks)


def _cmul_add(ar, ai, h_re, h_im, s_re, s_im):
    return ar * h_re - ai * h_im + s_re, ar * h_im + ai * h_re + s_im


def _lane_group(rows):
    return lax.broadcasted_iota(jnp.int32, (rows, V7X_LANES), 1) // S5_GROUP_CH


def _regroup_in(u_ref, lhs_scr, nrows):
    t, rbk = SSM_CHUNK, SSM_REGROUP_ROWS
    grp = _lane_group(rbk)

    def body(rb, carry):
        pieces = [u_ref[pl.ds(rb * (rbk * t) + tt, rbk, stride=t), :] for tt in range(t)]
        for gl in range(SSM_TILE_GROUPS):
            acc = None
            for tt in range(t):
                sh = ((tt - gl) * S5_GROUP_CH) % V7X_LANES
                p = pieces[tt] if sh == 0 else pltpu.roll(pieces[tt], sh, axis=1)
                acc = p if acc is None else jnp.where(grp == tt, p, acc)
            lhs_scr[gl, pl.ds(pl.multiple_of(rb * rbk, rbk), rbk), :] = acc.astype(BF16)
        return carry

    lax.fori_loop(0, nrows // rbk, body, 0)


def _regroup_out(y_scr, y_ref, nrows):
    t, rbk = SSM_CHUNK, SSM_REGROUP_ROWS
    grp = _lane_group(rbk)

    def body(rb, carry):
        pieces = [y_scr[gl, pl.ds(pl.multiple_of(rb * rbk, rbk), rbk), :] for gl in range(SSM_TILE_GROUPS)]
        for tt in range(t):
            acc = None
            for gl in range(SSM_TILE_GROUPS):
                sh = ((gl - tt) * S5_GROUP_CH) % V7X_LANES
                p = pieces[gl] if sh == 0 else pltpu.roll(pieces[gl], sh, axis=1)
                acc = p if acc is None else jnp.where(grp == gl, p, acc)
            y_ref[pl.ds(rb * (rbk * t) + tt, rbk, stride=t), :] = acc
        return carry

    lax.fori_loop(0, nrows // rbk, body, 0)


def _ssm_core_kernel(u_ref, win_ref, wout_ref, a_ref, aseg_ref, h0_ref, y_ref, hfin_ref,
                     lhs_scr, y_scr, s_re, s_im, tf_re, tb_re, tf_im, tb_im, *, nseq, nc, nsub):
    g = pl.program_id(1)
    half = SSM_LANES // 2
    pitch = nc + SSM_PITCH_PAD
    nrows = nseq * nc
    nslab = nseq // V7X_SUBLANES

    @pl.when(g == 0)
    def _():
        _regroup_in(u_ref, lhs_scr, nrows)

    lhs = lhs_scr[g]
    s = jnp.dot(lhs, win_ref[0], preferred_element_type=F32)
    for p in range(nseq):
        s_re[p * pitch:p * pitch + nc, :] = s[p * nc:(p + 1) * nc, 0:half]
        s_im[p * pitch:p * pitch + nc, :] = s[p * nc:(p + 1) * nc, half:]

    ar, ai = a_ref[0][:, 0:half], a_ref[0][:, half:]
    fwd = lax.broadcasted_iota(jnp.int32, (V7X_SUBLANES, half), 1) < S5_STATE

    def slab(q, c):
        return pl.ds(q * V7X_SUBLANES * pitch + c, V7X_SUBLANES, stride=pitch)

    def scan(init, store):
        hs = list(init)
        for c in range(nc):
            cb = nc - 1 - c
            for q in range(nslab):
                h_re, h_im = hs[q]
                x_re = jnp.where(fwd, s_re[slab(q, c), :], s_re[slab(q, cb), :])
                x_im = jnp.where(fwd, s_im[slab(q, c), :], s_im[slab(q, cb), :])
                if store:
                    tf_re[slab(q, c), :] = h_re
                    tb_re[slab(q, cb), :] = h_re
                    tf_im[slab(q, c), :] = h_im
                    tb_im[slab(q, cb), :] = h_im
                hs[q] = _cmul_add(ar, ai, h_re, h_im, x_re, x_im)
        return hs

    rows8 = lambda q: slice(q * V7X_SUBLANES, (q + 1) * V7X_SUBLANES)
    init = [(h0_ref[0, rows8(q), 0:half], h0_ref[0, rows8(q), half:]) for q in range(nslab)]
    if nsub > 1:
        zero = jnp.zeros((V7X_SUBLANES, half), F32)
        ends = scan([(zero, zero)] * nslab, False)
        gr, gi = aseg_ref[0][:, 0:half], aseg_ref[0][:, half:]
        seg = lax.broadcasted_iota(jnp.int32, (V7X_SUBLANES, half), 0) % nsub
        carried = jnp.where(fwd, seg, nsub - 1 - seg) != 0
        shift = lambda x: jnp.where(fwd, pltpu.roll(x, 1, axis=0), pltpu.roll(x, V7X_SUBLANES - 1, axis=0))
        for q in range(nslab):
            (h0_re, h0_im), (f_re, f_im) = init[q], ends[q]
            e_re, e_im = h0_re, h0_im
            for _ in range(nsub - 1):
                x_re, x_im = _cmul_add(gr, gi, e_re, e_im, f_re, f_im)
                e_re = h0_re + jnp.where(carried, shift(x_re), 0.0)
                e_im = h0_im + jnp.where(carried, shift(x_im), 0.0)
            init[q] = (e_re, e_im)
    hs = scan(init, True)
    for q in range(nslab):
        hfin_ref[0, rows8(q), 0:half] = hs[q][0]
        hfin_ref[0, rows8(q), half:] = hs[q][1]

    gather = lambda ref: jnp.concatenate([ref[p * pitch:p * pitch + nc, :] for p in range(nseq)], axis=0)
    fwd_rows = lax.broadcasted_iota(jnp.int32, (nrows, half), 1) < S5_STATE
    hp = jnp.concatenate([jnp.where(fwd_rows, gather(tf_re), gather(tb_re)),
                          jnp.where(fwd_rows, gather(tf_im), gather(tb_im))], axis=1).astype(BF16)
    y_scr[g] = (jnp.dot(lhs, wout_ref[0, 0:SSM_KDIM, :], preferred_element_type=F32)
                + jnp.dot(hp, wout_ref[0, SSM_KDIM:, :], preferred_element_type=F32))

    @pl.when(g == SSM_TILE_GROUPS - 1)
    def _():
        _regroup_out(y_scr, y_ref, nrows)


def _ssm_core(u, w_in, w_out, a_chunk, a_seg, h0, nseq, nc, nsub):
    r, d = u.shape
    nrows = nseq * nc
    assert r == nrows * SSM_CHUNK and nseq % V7X_SUBLANES == 0 and V7X_SUBLANES % nsub == 0
    assert nrows % SSM_REGROUP_ROWS == 0 and nc % V7X_SUBLANES == 0
    ntile = d // V7X_LANES
    grp = lambda j, g: (j * SSM_TILE_GROUPS + g, 0, 0)
    scan_rows = nseq * (nc + SSM_PITCH_PAD)
    return pl.pallas_call(
        functools.partial(_ssm_core_kernel, nseq=nseq, nc=nc, nsub=nsub),
        grid=(ntile, SSM_TILE_GROUPS),
        in_specs=[pl.BlockSpec((r, V7X_LANES), lambda j, g: (0, j)),
                  pl.BlockSpec((1,) + w_in.shape[1:], grp),
                  pl.BlockSpec((1,) + w_out.shape[1:], grp),
                  pl.BlockSpec((1, 1, SSM_LANES), grp),
                  pl.BlockSpec((1, 1, SSM_LANES), grp),
                  pl.BlockSpec((1, nseq, SSM_LANES), grp)],
        out_specs=[pl.BlockSpec((r, V7X_LANES), lambda j, g: (0, j)),
                   pl.BlockSpec((1, nseq, SSM_LANES), grp)],
        out_shape=[jax.ShapeDtypeStruct((r, d), F32),
                   jax.ShapeDtypeStruct((d // S5_GROUP_CH, nseq, SSM_LANES), F32)],
        scratch_shapes=[pltpu.VMEM((SSM_TILE_GROUPS, nrows, SSM_KDIM), BF16),
                        pltpu.VMEM((SSM_TILE_GROUPS, nrows, SSM_KDIM), F32)]
                       + [pltpu.VMEM((scan_rows, V7X_LANES), F32)] * 6,
        compiler_params=_cparams(2),
    )(u, w_in, w_out, a_chunk, a_seg, h0)


def _glu_kernel(y_ref, u_ref, d_ref, w_ref, z_ref):
    v = y_ref[...] + d_ref[...] * u_ref[...]
    z = 0.5 * v * (1.0 + jnp.tanh(math.sqrt(2.0 / math.pi) * (v + 0.044715 * (v * v * v))))
    gl = jnp.dot(z.astype(BF16), w_ref[...], preferred_element_type=F32)
    z_ref[...] = (z * _sigmoid(gl)).astype(BF16)


def _glu(y, u, d, w, tm):
    r, dm = y.shape
    return pl.pallas_call(
        _glu_kernel,
        grid=(r // tm,),
        in_specs=[pl.BlockSpec((tm, dm), lambda i: (i, 0)),
                  pl.BlockSpec((tm, dm), lambda i: (i, 0)),
                  pl.BlockSpec((1, dm), lambda i: (0, 0)),
                  pl.BlockSpec(w.shape, lambda i: (0, 0), pipeline_mode=pl.Buffered(1))],
        out_specs=pl.BlockSpec((tm, dm), lambda i: (i, 0)),
        out_shape=jax.ShapeDtypeStruct((r, dm), BF16),
        compiler_params=_cparams(1),
    )(y, u, d, w)


def _rope_tables(length):
    rows = length // GRID_W
    row = jnp.repeat(jnp.arange(rows, dtype=F32), GRID_W)
    col = jnp.tile(jnp.arange(GRID_W, dtype=F32), rows)
    inv = ROPE_BASE ** (-jnp.arange(ROPE_PAIRS_PER_AXIS, dtype=F32) / ROPE_PAIRS_PER_AXIS)
    ang = jnp.concatenate([row[:, None] * inv, col[:, None] * inv], axis=-1)
    cos, sin = jnp.cos(ang), jnp.sin(ang)
    return jnp.concatenate([cos, cos], axis=-1), jnp.concatenate([-sin, sin], axis=-1)


def kernel(x_prompt, x_sample, c, cache_a_k, cache_a_v, cache_b_k, cache_b_v, state_ssm, c_ctx, ada_w, ada_b, norm_g, mlp_w1, mlp_w2, attn_w_in, attn_w_out, attn_qk_norm, diff_lambda, diff_subln, ssm_w_in, ssm_a_re, ssm_a_im, ssm_log_dt, ssm_b_re, ssm_b_im, ssm_c_re, ssm_c_im, ssm_d, ssm_glu_w, ssm_w_out):
    bp, lp, d = x_prompt.shape
    bs, ls, _ = x_sample.shape
    depth = ada_w.shape[0]
    past = cache_a_k.shape[2]
    nsub = SSM_SEGMENTS
    assert bs + 1 <= 8 and bp % V7X_SUBLANES == 0
    assert lp % SSM_CHUNK == 0 and ls % (SSM_CHUNK * nsub) == 0 and ls % GRID_W == 0

    cond8 = jnp.zeros((8, d), F32).at[0].set(c_ctx).at[1:1 + bs].set(c)
    mods = _modulation(cond8, ada_w, ada_b).reshape(depth, 8 * N_MOD, 1, d)

    xp = x_prompt.reshape(bp * lp, d)
    xs = x_sample.reshape(bs * ls, d)
    tm = min(512, bp * lp, ls)
    tm_big = min(1024, bp * lp, ls)
    row_p = lambda i: 0
    row_s = lambda i: 1 + i // (ls // tm)
    row_s_big = lambda i: 1 + i // (ls // tm_big)
    w1_all = mlp_w1.astype(BF16)
    w2_all = mlp_w2.astype(BF16)
    new_ak = new_av = new_bk = new_bv = new_ssm = None
    for l in range(depth):
        mod = mods[l]
        g = norm_g[l][:, None, :]
        i = l // 2
        if l % 2 == 0:
            lam_init = 0.8 - 0.6 * math.exp(-0.3 * l)
            w_in = attn_w_in[i].astype(BF16)
            w_out = attn_w_out[i].astype(BF16)
            qkv, ka, va, kb, vb = _qkv_proj(xp, g[0], mod, row_p, w_in, attn_qk_norm[i], None, True, tm_big)
            new_ak = ka.reshape(bp, 1, lp, A_KV_HEADS, HEAD_DIM)
            new_av = va.reshape(bp, 1, lp, A_KV_HEADS, HEAD_DIM)
            new_bk = kb.reshape(bp, 1, lp, B_HEADS, 2, HEAD_DIM)
            new_bv = vb.reshape(bp, 1, lp, B_HEADS, B_V_DIM)
            oa, ob = _attention(qkv, bp, None, diff_lambda[i], diff_subln[i][None, :], lam_init,
                                tq_a=lp, tq_b=lp, tk=512)
            xp = _outproj([oa, ob], w_out, xp, g[1], mod, row_p, tm)
            (qkv,) = _qkv_proj(xs, g[0], mod, row_s_big, w_in, attn_qk_norm[i], _rope_tables(ls), False, tm_big)
            caches = (cache_a_k[:, i].reshape(bs * past, A_KV_HEADS * HEAD_DIM),
                      cache_a_v[:, i].reshape(bs * past, A_KV_HEADS * HEAD_DIM),
                      cache_b_k[:, i].reshape(bs * past, B_HEADS * 2 * HEAD_DIM),
                      cache_b_v[:, i].reshape(bs * past, B_HEADS * B_V_DIM))
            oa, ob = _attention(qkv, bs, caches, diff_lambda[i], diff_subln[i][None, :], lam_init,
                                tq_a=min(256, ls), tq_b=min(512, ls), tk=512)
            xs = _outproj([oa, ob], w_out, xs, g[1], mod, row_s, tm)
        else:
            w_in = ssm_w_in[i].astype(BF16)
            u_p = _ssm_in(xp, g[0], mod, row_p, w_in, tm_big, 1024)
            u_s = _ssm_in(xs, g[0], mod, row_s_big, w_in, tm_big, 1024)
            ng = d // S5_GROUP_CH
            nc_p, nc_s = lp // SSM_CHUNK, ls // SSM_CHUNK // nsub
            wi, wo, a_chunk, a_seg = _ssm_params(ssm_a_re[i], ssm_a_im[i], ssm_log_dt[i], ssm_b_re[i], ssm_b_im[i],
                                                 ssm_c_re[i], ssm_c_im[i], nc_s)
            st = state_ssm[:, i].astype(F32)
            h0 = jnp.zeros((ng, bs, nsub, 4, S5_STATE), F32)
            h0 = h0.at[:, :, 0, 0].set(jnp.transpose(st[:, 0, :, :, 0], (1, 0, 2)))
            h0 = h0.at[:, :, 0, 2].set(jnp.transpose(st[:, 0, :, :, 1], (1, 0, 2)))
            h0 = h0.at[:, :, nsub - 1, 1].set(jnp.transpose(st[:, 1, :, :, 0], (1, 0, 2)))
            h0 = h0.at[:, :, nsub - 1, 3].set(jnp.transpose(st[:, 1, :, :, 1], (1, 0, 2)))
            h0 = h0.reshape(ng, bs * nsub, SSM_LANES)
            y_p, hfin = _ssm_core(u_p, wi, wo, a_chunk, a_seg, jnp.zeros((ng, bp, SSM_LANES), F32), bp, nc_p, 1)
            y_s, _ = _ssm_core(u_s, wi, wo, a_chunk, a_seg, h0, bs * nsub, nc_s, nsub)
            new_ssm = jnp.transpose(hfin.reshape(ng, bp, 2, 2, S5_STATE), (1, 3, 0, 4, 2))
            glu_w = ssm_glu_w[i].astype(BF16)
            w_out = ssm_w_out[i].astype(BF16)
            dvec = ssm_d[i][None, :]
            z_p = _glu(y_p, u_p, dvec, glu_w, tm)
            z_s = _glu(y_s, u_s, dvec, glu_w, tm)
            xp = _outproj([z_p], w_out, xp, g[1], mod, row_p, tm)
            xs = _outproj([z_s], w_out, xs, g[1], mod, row_s, tm)
        xp = _mlp(xp, g[2], g[3], mod, row_p, w1_all, w2_all, l, tm_big, 512)
        xs = _mlp(xs, g[2], g[3], mod, row_s_big, w1_all, w2_all, l, tm_big, 512)
    return (xp.reshape(bp, lp, d), xs.reshape(bs, ls, d), new_ak, new_av, new_bk, new_bv,
            new_ssm[:, None])
```

```python
import functools
import math

import jax
import jax.numpy as jnp
from jax import lax
from jax.experimental import pallas as pl
from jax.experimental.pallas import tpu as pltpu

F32 = jnp.float32
BF16 = jnp.bfloat16
EPS = 1e-6

HEAD_DIM = 128
A_HEADS = 8
A_KV_HEADS = 2
A_GROUP = A_HEADS // A_KV_HEADS
B_HEADS = 4
B_V_DIM = 2 * HEAD_DIM
GRID_W = 64
ROPE_BASE = 10000.0
ROPE_PAIRS_PER_AXIS = HEAD_DIM // 4
N_MOD = 6
S5_GROUP_CH = 16
S5_STATE = 64

V7X_LANES = 128
V7X_SUBLANES = 8
V7X_VMEM_LIMIT_BYTES = 56 * 1024 * 1024

SSM_CHUNK = V7X_SUBLANES
SSM_TILE_GROUPS = V7X_LANES // S5_GROUP_CH
SSM_KDIM = SSM_CHUNK * S5_GROUP_CH
SSM_LANES = 4 * S5_STATE
SSM_SEGMENTS = 8
SSM_REGROUP_ROWS = 256
SSM_PITCH_PAD = 8

MLP_ACC_CHUNK = 512
QKV_TN = 512


def _cparams(n_axes):
    return pltpu.CompilerParams(dimension_semantics=("arbitrary",) * n_axes,
                                vmem_limit_bytes=V7X_VMEM_LIMIT_BYTES)


def _sigmoid(x):
    return 1.0 / (1.0 + jnp.exp(-x))


def _rms(x, g):
    return x * lax.rsqrt(jnp.mean(x * x, axis=-1, keepdims=True) + EPS) * g


def _mod_spec(d, slot, row_fn):
    return pl.BlockSpec((1, 1, d), lambda i, *_: (row_fn(i) * N_MOD + slot, 0, 0))


def _mod_kernel(c_ref, w_ref, b_ref, o_ref):
    c = c_ref[...]
    s = (c * _sigmoid(c)).astype(BF16)
    o_ref[0] = jnp.dot(s, w_ref[0].astype(BF16), preferred_element_type=F32) + b_ref[0]


def _modulation(cond8, ada_w, ada_b):
    depth, d, n = ada_w.shape
    tn = 1024
    return pl.pallas_call(
        _mod_kernel,
        grid=(depth, n // tn),
        in_specs=[pl.BlockSpec((8, d), lambda l, j: (0, 0)),
                  pl.BlockSpec((1, d, tn), lambda l, j: (l, 0, j)),
                  pl.BlockSpec((1, 1, tn), lambda l, j: (l, 0, j))],
        out_specs=pl.BlockSpec((1, 8, tn), lambda l, j: (l, 0, j)),
        out_shape=jax.ShapeDtypeStruct((depth, 8, n), F32),
        compiler_params=_cparams(2),
    )(cond8, ada_w, ada_b.reshape(depth, 1, n))


def _split_heads(y, fn):
    return jnp.concatenate([fn(y[:, h * HEAD_DIM:(h + 1) * HEAD_DIM]) for h in range(y.shape[1] // HEAD_DIM)], axis=1)


def _qkv_kernel(*refs, rope, caches):
    x_ref, g_ref, sh_ref, sc_ref, w_ref, qkg_ref = refs[:6]
    refs = refs[6:]
    if rope:
        cos_ref, sin_ref = refs[:2]
        refs = refs[2:]
    qkv_ref = refs[0]
    if caches:
        ka_ref, va_ref, kb_ref, vb_ref = refs[1:5]
    n_scr = refs[-1]
    j = pl.program_id(1)

    @pl.when(j == 0)
    def _():
        n = _rms(x_ref[...], g_ref[...]) * (1.0 + sc_ref[0]) + sh_ref[0]
        n_scr[...] = n.astype(BF16)

    def rot(y):
        if not rope:
            return y
        return y * cos_ref[...] + pltpu.roll(y, HEAD_DIM // 2, axis=1) * sin_ref[...]

    def acc():
        return jnp.dot(n_scr[...], w_ref[...], preferred_element_type=F32)

    q_scale = HEAD_DIM ** -0.5 * math.log2(math.e)
    kv_w = A_KV_HEADS * HEAD_DIM
    b_qa = A_HEADS * HEAD_DIM // QKV_TN
    b_va = b_qa + 2 * kv_w // QKV_TN
    b_qb = b_va + B_HEADS * 2 * HEAD_DIM // QKV_TN
    b_kb = b_qb + B_HEADS * 2 * HEAD_DIM // QKV_TN

    @pl.when(j < b_qa)
    def _():
        qkv_ref[...] = _split_heads(acc(), lambda y: rot(_rms(y, qkg_ref[0:1, :])) * q_scale).astype(BF16)

    @pl.when((j >= b_qa) & (j < b_va))
    def _():
        y = acc()
        kn = _split_heads(y[:, :kv_w], lambda t: _rms(t, qkg_ref[1:2, :]))
        if caches:
            ka_ref[...] = kn
            va_ref[...] = y[:, kv_w:]
        qkv_ref[:, :kv_w] = _split_heads(kn, rot).astype(BF16)
        qkv_ref[:, kv_w:] = y[:, kv_w:].astype(BF16)

    @pl.when((j >= b_va) & (j < b_qb))
    def _():
        qkv_ref[...] = _split_heads(acc(), lambda y: rot(y) * q_scale).astype(BF16)

    @pl.when((j >= b_qb) & (j < b_kb))
    def _():
        y = acc()
        if caches:
            kb_ref[...] = y
        qkv_ref[...] = _split_heads(y, rot).astype(BF16)

    @pl.when(j >= b_kb)
    def _():
        y = acc()
        if caches:
            vb_ref[...] = y
        qkv_ref[...] = y.astype(BF16)


def _qkv_proj(x, g, mod, row_fn, w, qk_g, rope_tabs, caches, tm):
    r, d = x.shape
    n = w.shape[1]
    tn = QKV_TN
    kv_w = A_KV_HEADS * HEAD_DIM
    assert 2 * kv_w == tn and r % tm == 0 and n % tn == 0
    assert not (caches and rope_tabs is not None)
    in_specs = [pl.BlockSpec((tm, d), lambda i, j: (i, 0)),
                pl.BlockSpec((1, d), lambda i, j: (0, 0)),
                _mod_spec(d, 0, row_fn), _mod_spec(d, 1, row_fn),
                pl.BlockSpec((d, tn), lambda i, j: (0, j)),
                pl.BlockSpec((2, HEAD_DIM), lambda i, j: (0, 0))]
    args = [x, g, mod, mod, w, qk_g]
    if rope_tabs is not None:
        nblk = rope_tabs[0].shape[0] // tm
        in_specs += [pl.BlockSpec((tm, HEAD_DIM), lambda i, j: (i % nblk, 0))] * 2
        args += list(rope_tabs)
    out_specs = [pl.BlockSpec((tm, tn), lambda i, j: (i, j))]
    out_shape = [jax.ShapeDtypeStruct((r, n), BF16)]
    if caches:
        b_kb = (A_HEADS + 2 * A_KV_HEADS + 2 * B_HEADS) * HEAD_DIM // tn
        nb = B_HEADS * 2 * HEAD_DIM // tn
        out_specs += [pl.BlockSpec((tm, kv_w), lambda i, j: (i, 0)),
                      pl.BlockSpec((tm, kv_w), lambda i, j: (i, 0)),
                      pl.BlockSpec((tm, tn), lambda i, j: (i, jnp.clip(j - b_kb, 0, nb - 1))),
                      pl.BlockSpec((tm, tn), lambda i, j: (i, jnp.clip(j - b_kb - nb, 0, nb - 1)))]
        out_shape += [jax.ShapeDtypeStruct((r, kv_w), F32), jax.ShapeDtypeStruct((r, kv_w), F32),
                      jax.ShapeDtypeStruct((r, nb * tn), F32), jax.ShapeDtypeStruct((r, nb * tn), F32)]
    return pl.pallas_call(
        functools.partial(_qkv_kernel, rope=rope_tabs is not None, caches=caches),
        grid=(r // tm, n // tn),
        in_specs=in_specs, out_specs=out_specs, out_shape=out_shape,
        scratch_shapes=[pltpu.VMEM((tm, d), BF16)],
        compiler_params=_cparams(2),
    )(*args)


def _softmax_pv(q, k_chunks, v_chunks, s_scr, rb, sum_on_mxu):
    m_rows = q.shape[0]
    outs = []
    for r0 in range(0, m_rows, rb):
        qr = q[r0:r0 + rb]
        m = jnp.full((rb, 1), -jnp.inf, F32)
        off = 0
        for kc in k_chunks:
            k = kc()
            s = lax.dot_general(qr, k, (((1,), (1,)), ((), ())), preferred_element_type=F32)
            s_scr[r0:r0 + rb, off:off + k.shape[0]] = s
            m = jnp.maximum(m, jnp.max(s, axis=-1, keepdims=True))
            off += k.shape[0]
        l = jnp.zeros((rb, 1), F32)
        acc = None
        off = 0
        for vc in v_chunks:
            v = vc()
            p = jnp.exp2(s_scr[r0:r0 + rb, off:off + v.shape[0]] - m)
            if sum_on_mxu:
                v = jnp.concatenate([v, jnp.ones_like(v)], axis=1)
            else:
                l = l + jnp.sum(p, axis=-1, keepdims=True)
            pv = jnp.dot(p.astype(BF16), v, preferred_element_type=F32)
            acc = pv if acc is None else acc + pv
            off += v.shape[0]
        if sum_on_mxu:
            dv = acc.shape[1] // 2
            outs.append(acc[:, :dv] / acc[:, dv:])
        else:
            outs.append(acc / l)
    return jnp.concatenate(outs, axis=0)


def _chunks(ref, c0, width, tk, cache_ref=None):
    length = ref.shape[0]
    out = [(lambda s=s: ref[s:s + tk, c0:c0 + width]) for s in range(0, length, tk)]
    if cache_ref is not None:
        out.append(lambda: cache_ref[:, c0:c0 + width].astype(BF16))
    return out


def _gqa_kernel(*refs, has_cache, tk, rb):
    if has_cache:
        q_ref, k_ref, v_ref, ck_ref, cv_ref, o_ref, s_scr = refs
    else:
        q_ref, k_ref, v_ref, o_ref, s_scr = refs
        ck_ref = cv_ref = None
    tq = q_ref.shape[0]
    q = jnp.concatenate([q_ref[:, g * HEAD_DIM:(g + 1) * HEAD_DIM] for g in range(A_GROUP)], axis=0)
    o = _softmax_pv(q, _chunks(k_ref, 0, HEAD_DIM, tk, ck_ref), _chunks(v_ref, 0, HEAD_DIM, tk, cv_ref),
                    s_scr, rb, True)
    o_ref[...] = jnp.concatenate([o[g * tq:(g + 1) * tq] for g in range(A_GROUP)], axis=1).astype(BF16)


def _diff_kernel(*refs, has_cache, tk, rb, lam_init):
    if has_cache:
        q_ref, k_ref, v_ref, ck_ref, cv_ref, lam_ref, sg_ref, o_ref, s_scr = refs
    else:
        q_ref, k_ref, v_ref, lam_ref, sg_ref, o_ref, s_scr = refs
        ck_ref = cv_ref = None
    lp = lam_ref[...]
    lam = (jnp.exp(jnp.sum(lp[0:1] * lp[1:2], axis=-1, keepdims=True))
           - jnp.exp(jnp.sum(lp[2:3] * lp[3:4], axis=-1, keepdims=True)) + lam_init)
    outs = []
    for m in range(2):
        outs.append(_softmax_pv(q_ref[:, m * HEAD_DIM:(m + 1) * HEAD_DIM],
                                _chunks(k_ref, m * HEAD_DIM, HEAD_DIM, tk, ck_ref),
                                _chunks(v_ref, 0, B_V_DIM, tk, cv_ref), s_scr, rb, False))
    o = outs[0] - lam * outs[1]
    o_ref[...] = (_rms(o, sg_ref[...]) * (1.0 - lam_init)).astype(BF16)


def _attention(qkv, nb, caches, lam_p, subln_g, lam_init, tq_a, tq_b, tk):
    r = qkv.shape[0]
    length = r // nb
    has_cache = caches is not None
    lk = length + (caches[0].shape[0] // nb if has_cache else 0)
    tk = min(tk, length)
    ka0 = A_HEADS
    va0 = ka0 + A_KV_HEADS
    qb0 = (A_HEADS + 2 * A_KV_HEADS) * HEAD_DIM // B_V_DIM
    kb0 = qb0 + B_HEADS
    vb0 = kb0 + B_HEADS

    nq = length // tq_a
    wq = A_GROUP * HEAD_DIM
    in_specs = [pl.BlockSpec((tq_a, wq), lambda b, h, i: (b * nq + i, h)),
                pl.BlockSpec((length, HEAD_DIM), lambda b, h, i: (b, ka0 + h)),
                pl.BlockSpec((length, HEAD_DIM), lambda b, h, i: (b, va0 + h))]
    args = [qkv, qkv, qkv]
    if has_cache:
        pc = caches[0].shape[0] // nb
        in_specs += [pl.BlockSpec((pc, HEAD_DIM), lambda b, h, i: (b, h))] * 2
        args += [caches[0], caches[1]]
    rb_a = min(256, A_GROUP * tq_a)
    oa = pl.pallas_call(
        functools.partial(_gqa_kernel, has_cache=has_cache, tk=tk, rb=rb_a),
        grid=(nb, A_KV_HEADS, nq),
        in_specs=in_specs,
        out_specs=pl.BlockSpec((tq_a, wq), lambda b, h, i: (b * nq + i, h)),
        out_shape=jax.ShapeDtypeStruct((r, A_HEADS * HEAD_DIM), BF16),
        scratch_shapes=[pltpu.VMEM((A_GROUP * tq_a, lk), F32)],
        compiler_params=_cparams(3),
    )(*args)

    nq = length // tq_b
    in_specs = [pl.BlockSpec((tq_b, B_V_DIM), lambda b, h, i: (b * nq + i, qb0 + h)),
                pl.BlockSpec((length, B_V_DIM), lambda b, h, i: (b, kb0 + h)),
                pl.BlockSpec((length, B_V_DIM), lambda b, h, i: (b, vb0 + h))]
    args = [qkv, qkv, qkv]
    if has_cache:
        in_specs += [pl.BlockSpec((pc, B_V_DIM), lambda b, h, i: (b, h))] * 2
        args += [caches[2], caches[3]]
    in_specs += [pl.BlockSpec((4, HEAD_DIM), lambda b, h, i: (0, 0)),
                 pl.BlockSpec((1, B_V_DIM), lambda b, h, i: (0, 0))]
    args += [lam_p, subln_g]
    ob = pl.pallas_call(
        functools.partial(_diff_kernel, has_cache=has_cache, tk=tk, rb=min(256, tq_b), lam_init=lam_init),
        grid=(nb, B_HEADS, nq),
        in_specs=in_specs,
        out_specs=pl.BlockSpec((tq_b, B_V_DIM), lambda b, h, i: (b * nq + i, h)),
        out_shape=jax.ShapeDtypeStruct((r, B_HEADS * B_V_DIM), BF16),
        scratch_shapes=[pltpu.VMEM((tq_b, lk), F32)],
        compiler_params=_cparams(3),
    )(*args)
    return oa, ob


def _outproj_kernel(*refs, n_in):
    o_refs = refs[:n_in]
    w_refs = refs[n_in:2 * n_in]
    x_ref, g_ref, gate_ref, out_ref = refs[2 * n_in:]
    y = None
    for o_ref, w_ref in zip(o_refs, w_refs):
        t = jnp.dot(o_ref[...], w_ref[...], preferred_element_type=F32)
        y = t if y is None else y + t
    out_ref[...] = x_ref[...] + gate_ref[0] * _rms(y, g_ref[...])


def _outproj(os_, w, x, g, mod, row_fn, tm):
    r, d = x.shape
    n_in = len(os_)
    kw = os_[0].shape[1]
    assert all(o.shape[1] == kw for o in os_) and w.shape == (n_in * kw, d)
    in_specs = ([pl.BlockSpec((tm, kw), lambda i: (i, 0)) for _ in os_]
                + [pl.BlockSpec((kw, d), lambda i, k=k: (k, 0), pipeline_mode=pl.Buffered(1)) for k in range(n_in)]
                + [pl.BlockSpec((tm, d), lambda i: (i, 0)),
                   pl.BlockSpec((1, d), lambda i: (0, 0)),
                   _mod_spec(d, 2, row_fn)])
    return pl.pallas_call(
        functools.partial(_outproj_kernel, n_in=n_in),
        grid=(r // tm,),
        in_specs=in_specs,
        out_specs=pl.BlockSpec((tm, d), lambda i: (i, 0)),
        out_shape=jax.ShapeDtypeStruct((r, d), F32),
        compiler_params=_cparams(1),
    )(*os_, *([w] * n_in), x, g, mod)


def _mlp_kernel(x_ref, g2_ref, sh_ref, sc_ref, gate_ref, g3_ref, w1_ref, w2_ref, out_ref, n_scr):
    f = pl.program_id(1)

    @pl.when(f == 0)
    def _():
        n = _rms(x_ref[...], g2_ref[...]) * (1.0 + sc_ref[0]) + sh_ref[0]
        n_scr[...] = n.astype(BF16)
        out_ref[...] = jnp.zeros_like(out_ref)

    h = jnp.maximum(jnp.dot(n_scr[...], w1_ref[...], preferred_element_type=F32), 0.0)
    hb = (h * h).astype(BF16)
    d = out_ref.shape[1]
    for c0 in range(0, d, MLP_ACC_CHUNK):
        out_ref[:, c0:c0 + MLP_ACC_CHUNK] += jnp.dot(hb, w2_ref[:, c0:c0 + MLP_ACC_CHUNK],
                                                     preferred_element_type=F32)

    @pl.when(f == pl.num_programs(1) - 1)
    def _():
        out_ref[...] = x_ref[...] + gate_ref[0] * _rms(out_ref[...], g3_ref[...])


def _mlp(x, g2, g3, mod, row_fn, w1, w2, layer, tm, tf):
    r, d = x.shape
    dff = w1.shape[2]
    return pl.pallas_call(
        _mlp_kernel,
        grid=(r // tm, dff // tf),
        in_specs=[pl.BlockSpec((tm, d), lambda i, f: (i, 0)),
                  pl.BlockSpec((1, d), lambda i, f: (0, 0)),
                  _mod_spec(d, 3, row_fn), _mod_spec(d, 4, row_fn), _mod_spec(d, 5, row_fn),
                  pl.BlockSpec((1, d), lambda i, f: (0, 0)),
                  pl.BlockSpec((None, d, tf), lambda i, f: (layer, 0, f)),
                  pl.BlockSpec((None, tf, d), lambda i, f: (layer, f, 0))],
        out_specs=pl.BlockSpec((tm, d), lambda i, f: (i, 0)),
        out_shape=jax.ShapeDtypeStruct((r, d), F32),
        scratch_shapes=[pltpu.VMEM((tm, d), BF16)],
        compiler_params=_cparams(2),
    )(x, g2, mod, mod, mod, g3, w1, w2)


def _ssm_in_kernel(x_ref, g_ref, sh_ref, sc_ref, w_ref, u_ref, n_scr):
    @pl.when(pl.program_id(1) == 0)
    def _():
        n = _rms(x_ref[...], g_ref[...]) * (1.0 + sc_ref[0]) + sh_ref[0]
        n_scr[...] = n.astype(BF16)

    u_ref[...] = jnp.dot(n_scr[...], w_ref[...], preferred_element_type=F32)


def _ssm_in(x, g, mod, row_fn, w, tm, tn):
    r, d = x.shape
    n = w.shape[1]
    return pl.pallas_call(
        _ssm_in_kernel,
        grid=(r // tm, n // tn),
        in_specs=[pl.BlockSpec((tm, d), lambda i, j: (i, 0)),
                  pl.BlockSpec((1, d), lambda i, j: (0, 0)),
                  _mod_spec(d, 0, row_fn), _mod_spec(d, 1, row_fn),
                  pl.BlockSpec((d, tn), lambda i, j: (0, j))],
        out_specs=pl.BlockSpec((tm, tn), lambda i, j: (i, j)),
        out_shape=jax.ShapeDtypeStruct((r, n), F32),
        scratch_shapes=[pltpu.VMEM((tm, d), BF16)],
        compiler_params=_cparams(2),
    )(x, g, mod, mod, w)


def _ssm_params(a_re, a_im, log_dt, b_re, b_im, c_re, c_im, seg_chunks):
    t, nst, nch = SSM_CHUNK, S5_STATE, S5_GROUP_CH
    hi = lax.Precision.HIGHEST
    ar, ai = a_re.astype(F32), a_im.astype(F32)
    dt = jnp.exp(log_dt.astype(F32))[..., None]
    lr, li = ar * dt, ai * dt
    g = ar.shape[1]

    def power(k):
        mag = jnp.exp(lr * k)
        return mag * jnp.cos(li * k), mag * jnp.sin(li * k)

    pr, pi = power(jnp.arange(t + 1, dtype=F32)[:, None, None, None])
    xr, xi = pr[1] - 1.0, pi[1]
    den = ar * ar + ai * ai
    qr, qi = (xr * ar + xi * ai) / den, (xi * ar - xr * ai) / den
    br, bi = b_re.astype(F32), b_im.astype(F32)
    bbr = qr[..., None] * br - qi[..., None] * bi
    bbi = qr[..., None] * bi + qi[..., None] * br
    ccr, cci = c_re.astype(F32), c_im.astype(F32)
    steps = jnp.arange(t)

    def w_in_dir(d, idx):
        er, ei = pr[idx, d][..., None], pi[idx, d][..., None]
        wr = er * bbr[d][None] - ei * bbi[d][None]
        wi = er * bbi[d][None] + ei * bbr[d][None]
        rows = lambda w: jnp.transpose(w, (1, 0, 3, 2)).reshape(g, t * nch, nst)
        return rows(wr), rows(wi)

    wfr, wfi = w_in_dir(0, t - 1 - steps)
    wbr, wbi = w_in_dir(1, steps)
    w_in = jnp.concatenate([wfr, wbr, wfi, wbi], axis=-1)

    cpr = ccr[None] * pr[:, :, :, None, :] - cci[None] * pi[:, :, :, None, :]
    cpi = ccr[None] * pi[:, :, :, None, :] + cci[None] * pr[:, :, :, None, :]
    taps = (jnp.einsum("kdgcn,dgne->dkgce", cpr[:t], bbr, precision=hi)
            - jnp.einsum("kdgcn,dgne->dkgce", cpi[:t], bbi, precision=hi))
    diff = steps[None, :] - steps[:, None]
    kf = jnp.where((diff >= 0)[..., None, None, None], taps[0][jnp.clip(diff, 0, t - 1)], 0.0)
    kb = jnp.where((diff <= 0)[..., None, None, None], taps[1][jnp.clip(-diff, 0, t - 1)], 0.0)
    toep = jnp.transpose(kf + kb, (2, 0, 4, 1, 3)).reshape(g, t * nch, t * nch)

    cols = lambda w: jnp.transpose(w, (1, 3, 0, 2)).reshape(g, nst, t * nch)
    c_out = jnp.concatenate([cols(cpr[1 + steps, 0]), cols(cpr[t - steps, 1]),
                             cols(-cpi[1 + steps, 0]), cols(-cpi[t - steps, 1])], axis=1)
    w_out = jnp.concatenate([toep, c_out], axis=1)

    def decay(k):
        dr, di = power(float(k))
        return jnp.concatenate([dr[0], dr[1], di[0], di[1]], axis=-1)[:, None, :]

    return w_in.astype(BF16), w_out.astype(BF16), decay(t), decay(t * seg_chunks)


def _cmul_add(ar, ai, h_re, h_im, s_re, s_im):
    return ar * h_re - ai * h_im + s_re, ar * h_im + ai * h_re + s_im


def _regroup_perms():
    n = SSM_CHUNK * V7X_LANES
    idx = jnp.arange(n)
    tok, grp, ch = idx // V7X_LANES, idx % V7X_LANES // S5_GROUP_CH, idx % S5_GROUP_CH
    dst = grp * SSM_KDIM + tok * S5_GROUP_CH + ch
    p_in = (dst[:, None] == idx[None, :]).astype(BF16)
    return p_in, p_in.T


def _regroup_in(u_ref, pin_ref, lhs_scr, nrows):
    t, rbk = SSM_CHUNK, min(SSM_REGROUP_ROWS, nrows)

    def body(rb, carry):
        pieces = [u_ref[pl.ds(rb * (rbk * t) + tt, rbk, stride=t), :].astype(BF16) for tt in range(t)]
        out = jnp.dot(jnp.concatenate(pieces, axis=1), pin_ref[...], preferred_element_type=F32)
        rows = pl.ds(pl.multiple_of(rb * rbk, rbk), rbk)
        for gl in range(SSM_TILE_GROUPS):
            lhs_scr[gl, rows, :] = out[:, gl * SSM_KDIM:(gl + 1) * SSM_KDIM].astype(BF16)
        return carry

    lax.fori_loop(0, nrows // rbk, body, 0)


def _regroup_out(y_scr, pout_ref, y_ref, nrows):
    t, rbk = SSM_CHUNK, min(SSM_REGROUP_ROWS, nrows)

    def body(rb, carry):
        rows = pl.ds(pl.multiple_of(rb * rbk, rbk), rbk)
        rest = jnp.concatenate([y_scr[gl, rows, :] for gl in range(SSM_TILE_GROUPS)], axis=1)
        out = None
        for _ in range(3):
            term = rest.astype(BF16)
            rest = rest - term.astype(F32)
            part = jnp.dot(term, pout_ref[...], preferred_element_type=F32)
            out = part if out is None else out + part
        for tt in range(t):
            y_ref[pl.ds(rb * (rbk * t) + tt, rbk, stride=t), :] = out[:, tt * V7X_LANES:(tt + 1) * V7X_LANES]
        return carry

    lax.fori_loop(0, nrows // rbk, body, 0)


def _ssm_core_kernel(u_ref, pin_ref, pout_ref, win_ref, wout_ref, a_ref, aseg_ref, h0_ref, y_ref, hfin_ref,
                     lhs_scr, y_scr, s_re, s_im, tf_re, tb_re, tf_im, tb_im, *, nseq, nc, nsub):
    g = pl.program_id(1)
    half = SSM_LANES // 2
    pitch = nc + SSM_PITCH_PAD
    nrows = nseq * nc
    nslab = nseq // V7X_SUBLANES

    @pl.when(g == 0)
    def _():
        _regroup_in(u_ref, pin_ref, lhs_scr, nrows)

    lhs = lhs_scr[g]
    s = jnp.dot(lhs, win_ref[0], preferred_element_type=F32)
    for p in range(nseq):
        s_re[p * pitch:p * pitch + nc, :] = s[p * nc:(p + 1) * nc, 0:half]
        s_im[p * pitch:p * pitch + nc, :] = s[p * nc:(p + 1) * nc, half:]

    ar, ai = a_ref[0][:, 0:half], a_ref[0][:, half:]
    fwd = lax.broadcasted_iota(jnp.int32, (V7X_SUBLANES, half), 1) < S5_STATE

    def slab(q, c):
        return pl.ds(q * V7X_SUBLANES * pitch + c, V7X_SUBLANES, stride=pitch)

    def scan(init, store):
        hs = list(init)
        for c in range(nc):
            cb = nc - 1 - c
            for q in range(nslab):
                h_re, h_im = hs[q]
                x_re = jnp.where(fwd, s_re[slab(q, c), :], s_re[slab(q, cb), :])
                x_im = jnp.where(fwd, s_im[slab(q, c), :], s_im[slab(q, cb), :])
                if store:
                    tf_re[slab(q, c), :] = h_re
                    tb_re[slab(q, cb), :] = h_re
                    tf_im[slab(q, c), :] = h_im
                    tb_im[slab(q, cb), :] = h_im
                hs[q] = _cmul_add(ar, ai, h_re, h_im, x_re, x_im)
        return hs

    rows8 = lambda q: slice(q * V7X_SUBLANES, (q + 1) * V7X_SUBLANES)
    init = [(h0_ref[0, rows8(q), 0:half], h0_ref[0, rows8(q), half:]) for q in range(nslab)]
    if nsub > 1:
        zero = jnp.zeros((V7X_SUBLANES, half), F32)
        ends = scan([(zero, zero)] * nslab, False)
        gr, gi = aseg_ref[0][:, 0:half], aseg_ref[0][:, half:]
        seg = lax.broadcasted_iota(jnp.int32, (V7X_SUBLANES, half), 0) % nsub
        carried = jnp.where(fwd, seg, nsub - 1 - seg) != 0
        shift = lambda x: jnp.where(fwd, pltpu.roll(x, 1, axis=0), pltpu.roll(x, V7X_SUBLANES - 1, axis=0))
        for q in range(nslab):
            (h0_re, h0_im), (f_re, f_im) = init[q], ends[q]
            e_re, e_im = h0_re, h0_im
            for _ in range(nsub - 1):
                x_re, x_im = _cmul_add(gr, gi, e_re, e_im, f_re, f_im)
                e_re = h0_re + jnp.where(carried, shift(x_re), 0.0)
                e_im = h0_im + jnp.where(carried, shift(x_im), 0.0)
            init[q] = (e_re, e_im)
    hs = scan(init, True)
    for q in range(nslab):
        hfin_ref[0, rows8(q), 0:half] = hs[q][0]
        hfin_ref[0, rows8(q), half:] = hs[q][1]

    gather = lambda ref: jnp.concatenate([ref[p * pitch:p * pitch + nc, :] for p in range(nseq)], axis=0)
    fwd_rows = lax.broadcasted_iota(jnp.int32, (nrows, half), 1) < S5_STATE
    hp = jnp.concatenate([jnp.where(fwd_rows, gather(tf_re), gather(tb_re)),
                          jnp.where(fwd_rows, gather(tf_im), gather(tb_im))], axis=1).astype(BF16)
    y_scr[g] = (jnp.dot(lhs, wout_ref[0, 0:SSM_KDIM, :], preferred_element_type=F32)
                + jnp.dot(hp, wout_ref[0, SSM_KDIM:, :], preferred_element_type=F32))

    @pl.when(g == SSM_TILE_GROUPS - 1)
    def _():
        _regroup_out(y_scr, pout_ref, y_ref, nrows)


def _ssm_core(u, w_in, w_out, a_chunk, a_seg, h0, nseq, nc, nsub):
    r, d = u.shape
    nrows = nseq * nc
    assert r == nrows * SSM_CHUNK and nseq % V7X_SUBLANES == 0 and V7X_SUBLANES % nsub == 0
    assert nrows % min(SSM_REGROUP_ROWS, nrows) == 0 and nc % V7X_SUBLANES == 0
    ntile = d // V7X_LANES
    grp = lambda j, g: (j * SSM_TILE_GROUPS + g, 0, 0)
    scan_rows = nseq * (nc + SSM_PITCH_PAD)
    p_in, p_out = _regroup_perms()
    perm_spec = pl.BlockSpec(p_in.shape, lambda j, g: (0, 0), pipeline_mode=pl.Buffered(1))
    return pl.pallas_call(
        functools.partial(_ssm_core_kernel, nseq=nseq, nc=nc, nsub=nsub),
        grid=(ntile, SSM_TILE_GROUPS),
        in_specs=[pl.BlockSpec((r, V7X_LANES), lambda j, g: (0, j)),
                  perm_spec, perm_spec,
                  pl.BlockSpec((1,) + w_in.shape[1:], grp),
                  pl.BlockSpec((1,) + w_out.shape[1:], grp),
                  pl.BlockSpec((1, 1, SSM_LANES), grp),
                  pl.BlockSpec((1, 1, SSM_LANES), grp),
                  pl.BlockSpec((1, nseq, SSM_LANES), grp)],
        out_specs=[pl.BlockSpec((r, V7X_LANES), lambda j, g: (0, j)),
                   pl.BlockSpec((1, nseq, SSM_LANES), grp)],
        out_shape=[jax.ShapeDtypeStruct((r, d), F32),
                   jax.ShapeDtypeStruct((d // S5_GROUP_CH, nseq, SSM_LANES), F32)],
        scratch_shapes=[pltpu.VMEM((SSM_TILE_GROUPS, nrows, SSM_KDIM), BF16),
                        pltpu.VMEM((SSM_TILE_GROUPS, nrows, SSM_KDIM), F32)]
                       + [pltpu.VMEM((scan_rows, V7X_LANES), F32)] * 6,
        compiler_params=_cparams(2),
    )(u, p_in, p_out, w_in, w_out, a_chunk, a_seg, h0)


def _glu_kernel(y_ref, u_ref, d_ref, w_ref, z_ref):
    v = y_ref[...] + d_ref[...] * u_ref[...]
    z = 0.5 * v * (1.0 + jnp.tanh(math.sqrt(2.0 / math.pi) * (v + 0.044715 * (v * v * v))))
    gl = jnp.dot(z.astype(BF16), w_ref[...], preferred_element_type=F32)
    z_ref[...] = (z * _sigmoid(gl)).astype(BF16)


def _glu(y, u, d, w, tm):
    r, dm = y.shape
    return pl.pallas_call(
        _glu_kernel,
        grid=(r // tm,),
        in_specs=[pl.BlockSpec((tm, dm), lambda i: (i, 0)),
                  pl.BlockSpec((tm, dm), lambda i: (i, 0)),
                  pl.BlockSpec((1, dm), lambda i: (0, 0)),
                  pl.BlockSpec(w.shape, lambda i: (0, 0), pipeline_mode=pl.Buffered(1))],
        out_specs=pl.BlockSpec((tm, dm), lambda i: (i, 0)),
        out_shape=jax.ShapeDtypeStruct((r, dm), BF16),
        compiler_params=_cparams(1),
    )(y, u, d, w)


def _rope_tables(length):
    rows = length // GRID_W
    row = jnp.repeat(jnp.arange(rows, dtype=F32), GRID_W)
    col = jnp.tile(jnp.arange(GRID_W, dtype=F32), rows)
    inv = ROPE_BASE ** (-jnp.arange(ROPE_PAIRS_PER_AXIS, dtype=F32) / ROPE_PAIRS_PER_AXIS)
    ang = jnp.concatenate([row[:, None] * inv, col[:, None] * inv], axis=-1)
    cos, sin = jnp.cos(ang), jnp.sin(ang)
    return jnp.concatenate([cos, cos], axis=-1), jnp.concatenate([-sin, sin], axis=-1)


def kernel(x_prompt, x_sample, c, cache_a_k, cache_a_v, cache_b_k, cache_b_v, state_ssm, c_ctx, ada_w, ada_b, norm_g, mlp_w1, mlp_w2, attn_w_in, attn_w_out, attn_qk_norm, diff_lambda, diff_subln, ssm_w_in, ssm_a_re, ssm_a_im, ssm_log_dt, ssm_b_re, ssm_b_im, ssm_c_re, ssm_c_im, ssm_d, ssm_glu_w, ssm_w_out):
    bp, lp, d = x_prompt.shape
    bs, ls, _ = x_sample.shape
    depth = ada_w.shape[0]
    past = cache_a_k.shape[2]
    nsub = SSM_SEGMENTS
    assert bs + 1 <= 8 and bp % V7X_SUBLANES == 0
    assert lp % SSM_CHUNK == 0 and ls % (SSM_CHUNK * nsub) == 0 and ls % GRID_W == 0

    cond8 = jnp.zeros((8, d), F32).at[0].set(c_ctx).at[1:1 + bs].set(c)
    mods = _modulation(cond8, ada_w, ada_b).reshape(depth, 8 * N_MOD, 1, d)

    xp = x_prompt.reshape(bp * lp, d)
    xs = x_sample.reshape(bs * ls, d)
    tm = min(512, bp * lp, ls)
    tm_big = min(1024, bp * lp, ls)
    row_p = lambda i: 0
    row_s = lambda i: 1 + i // (ls // tm)
    row_s_big = lambda i: 1 + i // (ls // tm_big)
    w1_all = mlp_w1.astype(BF16)
    w2_all = mlp_w2.astype(BF16)
    new_ak = new_av = new_bk = new_bv = new_ssm = None
    for l in range(depth):
        mod = mods[l]
        g = norm_g[l][:, None, :]
        i = l // 2
        if l % 2 == 0:
            lam_init = 0.8 - 0.6 * math.exp(-0.3 * l)
            w_in = attn_w_in[i].astype(BF16)
            w_out = attn_w_out[i].astype(BF16)
            qkv, ka, va, kb, vb = _qkv_proj(xp, g[0], mod, row_p, w_in, attn_qk_norm[i], None, True, tm_big)
            new_ak = ka.reshape(bp, 1, lp, A_KV_HEADS, HEAD_DIM)
            new_av = va.reshape(bp, 1, lp, A_KV_HEADS, HEAD_DIM)
            new_bk = kb.reshape(bp, 1, lp, B_HEADS, 2, HEAD_DIM)
            new_bv = vb.reshape(bp, 1, lp, B_HEADS, B_V_DIM)
            oa, ob = _attention(qkv, bp, None, diff_lambda[i], diff_subln[i][None, :], lam_init,
                                tq_a=lp, tq_b=lp, tk=512)
            xp = _outproj([oa, ob], w_out, xp, g[1], mod, row_p, tm)
            (qkv,) = _qkv_proj(xs, g[0], mod, row_s_big, w_in, attn_qk_norm[i], _rope_tables(ls), False, tm_big)
            caches = (cache_a_k[:, i].reshape(bs * past, A_KV_HEADS * HEAD_DIM),
                      cache_a_v[:, i].reshape(bs * past, A_KV_HEADS * HEAD_DIM),
                      cache_b_k[:, i].reshape(bs * past, B_HEADS * 2 * HEAD_DIM),
                      cache_b_v[:, i].reshape(bs * past, B_HEADS * B_V_DIM))
            oa, ob = _attention(qkv, bs, caches, diff_lambda[i], diff_subln[i][None, :], lam_init,
                                tq_a=min(256, ls), tq_b=min(512, ls), tk=512)
            xs = _outproj([oa, ob], w_out, xs, g[1], mod, row_s, tm)
        else:
            w_in = ssm_w_in[i].astype(BF16)
            u_p = _ssm_in(xp, g[0], mod, row_p, w_in, tm_big, 1024)
            u_s = _ssm_in(xs, g[0], mod, row_s_big, w_in, tm_big, 1024)
            ng = d // S5_GROUP_CH
            nc_p, nc_s = lp // SSM_CHUNK, ls // SSM_CHUNK // nsub
            wi, wo, a_chunk, a_seg = _ssm_params(ssm_a_re[i], ssm_a_im[i], ssm_log_dt[i], ssm_b_re[i], ssm_b_im[i],
                                                 ssm_c_re[i], ssm_c_im[i], nc_s)
            st = state_ssm[:, i].astype(F32)
            h0 = jnp.zeros((ng, bs, nsub, 4, S5_STATE), F32)
            h0 = h0.at[:, :, 0, 0].set(jnp.transpose(st[:, 0, :, :, 0], (1, 0, 2)))
            h0 = h0.at[:, :, 0, 2].set(jnp.transpose(st[:, 0, :, :, 1], (1, 0, 2)))
            h0 = h0.at[:, :, nsub - 1, 1].set(jnp.transpose(st[:, 1, :, :, 0], (1, 0, 2)))
            h0 = h0.at[:, :, nsub - 1, 3].set(jnp.transpose(st[:, 1, :, :, 1], (1, 0, 2)))
            h0 = h0.reshape(ng, bs * nsub, SSM_LANES)
            y_p, hfin = _ssm_core(u_p, wi, wo, a_chunk, a_seg, jnp.zeros((ng, bp, SSM_LANES), F32), bp, nc_p, 1)
            y_s, _ = _ssm_core(u_s, wi, wo, a_chunk, a_seg, h0, bs * nsub, nc_s, nsub)
            new_ssm = jnp.transpose(hfin.reshape(ng, bp, 2, 2, S5_STATE), (1, 3, 0, 4, 2))
            glu_w = ssm_glu_w[i].astype(BF16)
            w_out = ssm_w_out[i].astype(BF16)
            dvec = ssm_d[i][None, :]
            z_p = _glu(y_p, u_p, dvec, glu_w, tm)
            z_s = _glu(y_s, u_s, dvec, glu_w, tm)
            xp = _outproj([z_p], w_out, xp, g[1], mod, row_p, tm)
            xs = _outproj([z_s], w_out, xs, g[1], mod, row_s, tm)
        xp = _mlp(xp, g[2], g[3], mod, row_p, w1_all, w2_all, l, tm_big, 512)
        xs = _mlp(xs, g[2], g[3], mod, row_s_big, w1_all, w2_all, l, tm_big, 512)
    return (xp.reshape(bp, lp, d), xs.reshape(bs, ls, d), new_ak, new_av, new_bk, new_bv,
            new_ssm[:, None])
```

```python
import functools
import math

import jax
import jax.numpy as jnp
from jax import lax
from jax.experimental import pallas as pl
from jax.experimental.pallas import tpu as pltpu

F32 = jnp.float32
BF16 = jnp.bfloat16
EPS = 1e-6

HEAD_DIM = 128
A_HEADS = 8
A_KV_HEADS = 2
A_GROUP = A_HEADS // A_KV_HEADS
B_HEADS = 4
B_V_DIM = 2 * HEAD_DIM
GRID_W = 64
ROPE_BASE = 10000.0
ROPE_PAIRS_PER_AXIS = HEAD_DIM // 4
N_MOD = 6
S5_GROUP_CH = 16
S5_STATE = 64

V7X_LANES = 128
V7X_SUBLANES = 8
V7X_VMEM_LIMIT_BYTES = 56 * 1024 * 1024

SSM_CHUNK = V7X_SUBLANES
SSM_TILE_GROUPS = V7X_LANES // S5_GROUP_CH
SSM_KDIM = SSM_CHUNK * S5_GROUP_CH
SSM_LANES = 4 * S5_STATE
SSM_SEGMENTS = 8
SSM_REGROUP_ROWS = 256
SSM_PITCH_PAD = 8

MLP_ACC_CHUNK = 512
QKV_TN = 512


def _cparams(n_axes):
    return pltpu.CompilerParams(dimension_semantics=("arbitrary",) * n_axes,
                                vmem_limit_bytes=V7X_VMEM_LIMIT_BYTES)


def _sigmoid(x):
    return 1.0 / (1.0 + jnp.exp(-x))


def _rms(x, g):
    return x * lax.rsqrt(jnp.mean(x * x, axis=-1, keepdims=True) + EPS) * g


def _mod_spec(d, slot, row_fn):
    return pl.BlockSpec((1, 1, d), lambda i, *_: (row_fn(i) * N_MOD + slot, 0, 0))


def _mod_kernel(c_ref, w_ref, b_ref, o_ref):
    c = c_ref[...]
    s = (c * _sigmoid(c)).astype(BF16)
    o_ref[0] = jnp.dot(s, w_ref[0].astype(BF16), preferred_element_type=F32) + b_ref[0]


def _modulation(cond8, ada_w, ada_b):
    depth, d, n = ada_w.shape
    tn = 1024
    return pl.pallas_call(
        _mod_kernel,
        grid=(depth, n // tn),
        in_specs=[pl.BlockSpec((8, d), lambda l, j: (0, 0)),
                  pl.BlockSpec((1, d, tn), lambda l, j: (l, 0, j)),
                  pl.BlockSpec((1, 1, tn), lambda l, j: (l, 0, j))],
        out_specs=pl.BlockSpec((1, 8, tn), lambda l, j: (l, 0, j)),
        out_shape=jax.ShapeDtypeStruct((depth, 8, n), F32),
        compiler_params=_cparams(2),
    )(cond8, ada_w, ada_b.reshape(depth, 1, n))


def _split_heads(y, fn):
    return jnp.concatenate([fn(y[:, h * HEAD_DIM:(h + 1) * HEAD_DIM]) for h in range(y.shape[1] // HEAD_DIM)], axis=1)


def _qkv_kernel(*refs, rope, caches):
    x_ref, g_ref, sh_ref, sc_ref, w_ref, qkg_ref = refs[:6]
    refs = refs[6:]
    if rope:
        cos_ref, sin_ref = refs[:2]
        refs = refs[2:]
    qkv_ref = refs[0]
    if caches:
        ka_ref, va_ref, kb_ref, vb_ref = refs[1:5]
    n_scr = refs[-1]
    j = pl.program_id(1)

    @pl.when(j == 0)
    def _():
        n = _rms(x_ref[...], g_ref[...]) * (1.0 + sc_ref[0]) + sh_ref[0]
        n_scr[...] = n.astype(BF16)

    def rot(y):
        if not rope:
            return y
        return y * cos_ref[...] + pltpu.roll(y, HEAD_DIM // 2, axis=1) * sin_ref[...]

    def acc():
        return jnp.dot(n_scr[...], w_ref[...], preferred_element_type=F32)

    q_scale = HEAD_DIM ** -0.5 * math.log2(math.e)
    kv_w = A_KV_HEADS * HEAD_DIM
    b_qa = A_HEADS * HEAD_DIM // QKV_TN
    b_va = b_qa + 2 * kv_w // QKV_TN
    b_qb = b_va + B_HEADS * 2 * HEAD_DIM // QKV_TN
    b_kb = b_qb + B_HEADS * 2 * HEAD_DIM // QKV_TN

    @pl.when(j < b_qa)
    def _():
        qkv_ref[...] = _split_heads(acc(), lambda y: rot(_rms(y, qkg_ref[0:1, :])) * q_scale).astype(BF16)

    @pl.when((j >= b_qa) & (j < b_va))
    def _():
        y = acc()
        kn = _split_heads(y[:, :kv_w], lambda t: _rms(t, qkg_ref[1:2, :]))
        if caches:
            ka_ref[...] = kn
            va_ref[...] = y[:, kv_w:]
        qkv_ref[:, :kv_w] = _split_heads(kn, rot).astype(BF16)
        qkv_ref[:, kv_w:] = y[:, kv_w:].astype(BF16)

    @pl.when((j >= b_va) & (j < b_qb))
    def _():
        qkv_ref[...] = _split_heads(acc(), lambda y: rot(y) * q_scale).astype(BF16)

    @pl.when((j >= b_qb) & (j < b_kb))
    def _():
        y = acc()
        if caches:
            kb_ref[...] = y
        qkv_ref[...] = _split_heads(y, rot).astype(BF16)

    @pl.when(j >= b_kb)
    def _():
        y = acc()
        if caches:
            vb_ref[...] = y
        qkv_ref[...] = y.astype(BF16)


def _qkv_proj(x, g, mod, row_fn, w, qk_g, rope_tabs, caches, tm):
    r, d = x.shape
    n = w.shape[1]
    tn = QKV_TN
    kv_w = A_KV_HEADS * HEAD_DIM
    assert 2 * kv_w == tn and r % tm == 0 and n % tn == 0
    assert not (caches and rope_tabs is not None)
    in_specs = [pl.BlockSpec((tm, d), lambda i, j: (i, 0)),
                pl.BlockSpec((1, d), lambda i, j: (0, 0)),
                _mod_spec(d, 0, row_fn), _mod_spec(d, 1, row_fn),
                pl.BlockSpec((d, tn), lambda i, j: (0, j)),
                pl.BlockSpec((2, HEAD_DIM), lambda i, j: (0, 0))]
    args = [x, g, mod, mod, w, qk_g]
    if rope_tabs is not None:
        nblk = rope_tabs[0].shape[0] // tm
        in_specs += [pl.BlockSpec((tm, HEAD_DIM), lambda i, j: (i % nblk, 0))] * 2
        args += list(rope_tabs)
    out_specs = [pl.BlockSpec((tm, tn), lambda i, j: (i, j))]
    out_shape = [jax.ShapeDtypeStruct((r, n), BF16)]
    if caches:
        b_kb = (A_HEADS + 2 * A_KV_HEADS + 2 * B_HEADS) * HEAD_DIM // tn
        nb = B_HEADS * 2 * HEAD_DIM // tn
        out_specs += [pl.BlockSpec((tm, kv_w), lambda i, j: (i, 0)),
                      pl.BlockSpec((tm, kv_w), lambda i, j: (i, 0)),
                      pl.BlockSpec((tm, tn), lambda i, j: (i, jnp.clip(j - b_kb, 0, nb - 1))),
                      pl.BlockSpec((tm, tn), lambda i, j: (i, jnp.clip(j - b_kb - nb, 0, nb - 1)))]
        out_shape += [jax.ShapeDtypeStruct((r, kv_w), F32), jax.ShapeDtypeStruct((r, kv_w), F32),
                      jax.ShapeDtypeStruct((r, nb * tn), F32), jax.ShapeDtypeStruct((r, nb * tn), F32)]
    return pl.pallas_call(
        functools.partial(_qkv_kernel, rope=rope_tabs is not None, caches=caches),
        grid=(r // tm, n // tn),
        in_specs=in_specs, out_specs=out_specs, out_shape=out_shape,
        scratch_shapes=[pltpu.VMEM((tm, d), BF16)],
        compiler_params=_cparams(2),
    )(*args)


def _lane_tiles(x, op, acc):
    for j in range(0, x.shape[1], V7X_LANES):
        t = x[:, j:j + V7X_LANES]
        acc = t if acc is None else op(acc, t)
    return acc


def _softmax_pv(q, k_chunks, v_chunks, s_scr, rb):
    m_rows = q.shape[0]
    outs = []
    for r0 in range(0, m_rows, rb):
        qr = q[r0:r0 + rb]
        m_part = None
        off = 0
        for kc in k_chunks:
            k = kc()
            s = lax.dot_general(qr, k, (((1,), (1,)), ((), ())), preferred_element_type=F32)
            s_scr[r0:r0 + rb, off:off + k.shape[0]] = s
            m_part = _lane_tiles(s, jnp.maximum, m_part)
            off += k.shape[0]
        m = jnp.max(m_part, axis=-1, keepdims=True)
        l_part = acc = None
        off = 0
        for vc in v_chunks:
            v = vc()
            p = jnp.exp2(s_scr[r0:r0 + rb, off:off + v.shape[0]] - m)
            l_part = _lane_tiles(p, jnp.add, l_part)
            pv = jnp.dot(p.astype(BF16), v, preferred_element_type=F32)
            acc = pv if acc is None else acc + pv
            off += v.shape[0]
        outs.append(acc / jnp.sum(l_part, axis=-1, keepdims=True))
    return jnp.concatenate(outs, axis=0)


def _chunks(ref, c0, width, tk, cache_ref=None):
    length = ref.shape[0]
    out = [(lambda s=s: ref[s:s + tk, c0:c0 + width]) for s in range(0, length, tk)]
    if cache_ref is not None:
        out.append(lambda: cache_ref[:, c0:c0 + width].astype(BF16))
    return out


def _gqa_kernel(*refs, has_cache, tk, rb):
    if has_cache:
        q_ref, k_ref, v_ref, ck_ref, cv_ref, o_ref, s_scr = refs
    else:
        q_ref, k_ref, v_ref, o_ref, s_scr = refs
        ck_ref = cv_ref = None
    tq = q_ref.shape[0]
    q = jnp.concatenate([q_ref[:, g * HEAD_DIM:(g + 1) * HEAD_DIM] for g in range(A_GROUP)], axis=0)
    o = _softmax_pv(q, _chunks(k_ref, 0, HEAD_DIM, tk, ck_ref), _chunks(v_ref, 0, HEAD_DIM, tk, cv_ref),
                    s_scr, rb)
    o_ref[...] = jnp.concatenate([o[g * tq:(g + 1) * tq] for g in range(A_GROUP)], axis=1).astype(BF16)


def _diff_kernel(*refs, has_cache, tk, rb, lam_init):
    if has_cache:
        q_ref, k_ref, v_ref, ck_ref, cv_ref, lam_ref, sg_ref, o_ref, s_scr = refs
    else:
        q_ref, k_ref, v_ref, lam_ref, sg_ref, o_ref, s_scr = refs
        ck_ref = cv_ref = None
    lp = lam_ref[...]
    lam = (jnp.exp(jnp.sum(lp[0:1] * lp[1:2], axis=-1, keepdims=True))
           - jnp.exp(jnp.sum(lp[2:3] * lp[3:4], axis=-1, keepdims=True)) + lam_init)
    outs = []
    for m in range(2):
        outs.append(_softmax_pv(q_ref[:, m * HEAD_DIM:(m + 1) * HEAD_DIM],
                                _chunks(k_ref, m * HEAD_DIM, HEAD_DIM, tk, ck_ref),
                                _chunks(v_ref, 0, B_V_DIM, tk, cv_ref), s_scr, rb))
    o = outs[0] - lam * outs[1]
    o_ref[...] = (_rms(o, sg_ref[...]) * (1.0 - lam_init)).astype(BF16)


def _attention(qkv, nb, caches, lam_p, subln_g, lam_init, tq_a, tq_b, tk):
    r = qkv.shape[0]
    length = r // nb
    has_cache = caches is not None
    lk = length + (caches[0].shape[0] // nb if has_cache else 0)
    tk = min(tk, length)
    ka0 = A_HEADS
    va0 = ka0 + A_KV_HEADS
    qb0 = (A_HEADS + 2 * A_KV_HEADS) * HEAD_DIM // B_V_DIM
    kb0 = qb0 + B_HEADS
    vb0 = kb0 + B_HEADS

    nq = length // tq_a
    wq = A_GROUP * HEAD_DIM
    in_specs = [pl.BlockSpec((tq_a, wq), lambda b, h, i: (b * nq + i, h)),
                pl.BlockSpec((length, HEAD_DIM), lambda b, h, i: (b, ka0 + h)),
                pl.BlockSpec((length, HEAD_DIM), lambda b, h, i: (b, va0 + h))]
    args = [qkv, qkv, qkv]
    if has_cache:
        pc = caches[0].shape[0] // nb
        in_specs += [pl.BlockSpec((pc, HEAD_DIM), lambda b, h, i: (b, h))] * 2
        args += [caches[0], caches[1]]
    rb_a = min(256, A_GROUP * tq_a)
    oa = pl.pallas_call(
        functools.partial(_gqa_kernel, has_cache=has_cache, tk=tk, rb=rb_a),
        grid=(nb, A_KV_HEADS, nq),
        in_specs=in_specs,
        out_specs=pl.BlockSpec((tq_a, wq), lambda b, h, i: (b * nq + i, h)),
        out_shape=jax.ShapeDtypeStruct((r, A_HEADS * HEAD_DIM), BF16),
        scratch_shapes=[pltpu.VMEM((A_GROUP * tq_a, lk), F32)],
        compiler_params=_cparams(3),
    )(*args)

    nq = length // tq_b
    in_specs = [pl.BlockSpec((tq_b, B_V_DIM), lambda b, h, i: (b * nq + i, qb0 + h)),
                pl.BlockSpec((length, B_V_DIM), lambda b, h, i: (b, kb0 + h)),
                pl.BlockSpec((length, B_V_DIM), lambda b, h, i: (b, vb0 + h))]
    args = [qkv, qkv, qkv]
    if has_cache:
        in_specs += [pl.BlockSpec((pc, B_V_DIM), lambda b, h, i: (b, h))] * 2
        args += [caches[2], caches[3]]
    in_specs += [pl.BlockSpec((4, HEAD_DIM), lambda b, h, i: (0, 0)),
                 pl.BlockSpec((1, B_V_DIM), lambda b, h, i: (0, 0))]
    args += [lam_p, subln_g]
    ob = pl.pallas_call(
        functools.partial(_diff_kernel, has_cache=has_cache, tk=tk, rb=min(256, tq_b), lam_init=lam_init),
        grid=(nb, B_HEADS, nq),
        in_specs=in_specs,
        out_specs=pl.BlockSpec((tq_b, B_V_DIM), lambda b, h, i: (b * nq + i, h)),
        out_shape=jax.ShapeDtypeStruct((r, B_HEADS * B_V_DIM), BF16),
        scratch_shapes=[pltpu.VMEM((tq_b, lk), F32)],
        compiler_params=_cparams(3),
    )(*args)
    return oa, ob


def _outproj_kernel(*refs, n_in):
    o_refs = refs[:n_in]
    w_refs = refs[n_in:2 * n_in]
    x_ref, g_ref, gate_ref, out_ref = refs[2 * n_in:]
    y = None
    for o_ref, w_ref in zip(o_refs, w_refs):
        t = jnp.dot(o_ref[...], w_ref[...], preferred_element_type=F32)
        y = t if y is None else y + t
    out_ref[...] = x_ref[...] + gate_ref[0] * _rms(y, g_ref[...])


def _outproj(os_, w, x, g, mod, row_fn, tm):
    r, d = x.shape
    n_in = len(os_)
    kw = os_[0].shape[1]
    assert all(o.shape[1] == kw for o in os_) and w.shape == (n_in * kw, d)
    in_specs = ([pl.BlockSpec((tm, kw), lambda i: (i, 0)) for _ in os_]
                + [pl.BlockSpec((kw, d), lambda i, k=k: (k, 0), pipeline_mode=pl.Buffered(1)) for k in range(n_in)]
                + [pl.BlockSpec((tm, d), lambda i: (i, 0)),
                   pl.BlockSpec((1, d), lambda i: (0, 0)),
                   _mod_spec(d, 2, row_fn)])
    return pl.pallas_call(
        functools.partial(_outproj_kernel, n_in=n_in),
        grid=(r // tm,),
        in_specs=in_specs,
        out_specs=pl.BlockSpec((tm, d), lambda i: (i, 0)),
        out_shape=jax.ShapeDtypeStruct((r, d), F32),
        compiler_params=_cparams(1),
    )(*os_, *([w] * n_in), x, g, mod)


def _mlp_kernel(x_ref, g2_ref, sh_ref, sc_ref, gate_ref, g3_ref, w1_ref, w2_ref, out_ref, n_scr):
    f = pl.program_id(1)

    @pl.when(f == 0)
    def _():
        n = _rms(x_ref[...], g2_ref[...]) * (1.0 + sc_ref[0]) + sh_ref[0]
        n_scr[...] = n.astype(BF16)
        out_ref[...] = jnp.zeros_like(out_ref)

    h = jnp.maximum(jnp.dot(n_scr[...], w1_ref[...], preferred_element_type=F32), 0.0)
    hb = (h * h).astype(BF16)
    d = out_ref.shape[1]
    for c0 in range(0, d, MLP_ACC_CHUNK):
        out_ref[:, c0:c0 + MLP_ACC_CHUNK] += jnp.dot(hb, w2_ref[:, c0:c0 + MLP_ACC_CHUNK],
                                                     preferred_element_type=F32)

    @pl.when(f == pl.num_programs(1) - 1)
    def _():
        out_ref[...] = x_ref[...] + gate_ref[0] * _rms(out_ref[...], g3_ref[...])


def _mlp(x, g2, g3, mod, row_fn, w1, w2, layer, tm, tf):
    r, d = x.shape
    dff = w1.shape[2]
    return pl.pallas_call(
        _mlp_kernel,
        grid=(r // tm, dff // tf),
        in_specs=[pl.BlockSpec((tm, d), lambda i, f: (i, 0)),
                  pl.BlockSpec((1, d), lambda i, f: (0, 0)),
                  _mod_spec(d, 3, row_fn), _mod_spec(d, 4, row_fn), _mod_spec(d, 5, row_fn),
                  pl.BlockSpec((1, d), lambda i, f: (0, 0)),
                  pl.BlockSpec((None, d, tf), lambda i, f: (layer, 0, f)),
                  pl.BlockSpec((None, tf, d), lambda i, f: (layer, f, 0))],
        out_specs=pl.BlockSpec((tm, d), lambda i, f: (i, 0)),
        out_shape=jax.ShapeDtypeStruct((r, d), F32),
        scratch_shapes=[pltpu.VMEM((tm, d), BF16)],
        compiler_params=_cparams(2),
    )(x, g2, mod, mod, mod, g3, w1, w2)


def _ssm_in_kernel(x_ref, g_ref, sh_ref, sc_ref, w_ref, u_ref, n_scr):
    @pl.when(pl.program_id(1) == 0)
    def _():
        n = _rms(x_ref[...], g_ref[...]) * (1.0 + sc_ref[0]) + sh_ref[0]
        n_scr[...] = n.astype(BF16)

    u_ref[...] = jnp.dot(n_scr[...], w_ref[...], preferred_element_type=F32)


def _ssm_in(x, g, mod, row_fn, w, tm, tn):
    r, d = x.shape
    n = w.shape[1]
    return pl.pallas_call(
        _ssm_in_kernel,
        grid=(r // tm, n // tn),
        in_specs=[pl.BlockSpec((tm, d), lambda i, j: (i, 0)),
                  pl.BlockSpec((1, d), lambda i, j: (0, 0)),
                  _mod_spec(d, 0, row_fn), _mod_spec(d, 1, row_fn),
                  pl.BlockSpec((d, tn), lambda i, j: (0, j))],
        out_specs=pl.BlockSpec((tm, tn), lambda i, j: (i, j)),
        out_shape=jax.ShapeDtypeStruct((r, n), F32),
        scratch_shapes=[pltpu.VMEM((tm, d), BF16)],
        compiler_params=_cparams(2),
    )(x, g, mod, mod, w)


def _ssm_params(a_re, a_im, log_dt, b_re, b_im, c_re, c_im, seg_chunks):
    g = a_re.shape[1]
    both = lambda x: jnp.transpose(x.astype(F32), (1, 0, 2)).reshape(g, 1, 2 * S5_STATE)
    dt = jnp.exp(log_dt.astype(F32))[..., None]
    b_t = lambda x: jnp.transpose(x.astype(F32), (1, 3, 0, 2)).reshape(g, S5_GROUP_CH, 2 * S5_STATE)
    c_t = lambda x: jnp.transpose(x.astype(F32), (1, 2, 0, 3)).reshape(g, S5_GROUP_CH, 2 * S5_STATE)
    args = (both(a_re * dt), both(a_im * dt), both(a_re), both(a_im), b_t(b_re), b_t(b_im), c_t(c_re), c_t(c_im))
    blk = lambda a: pl.BlockSpec((1,) + a.shape[1:], lambda i: (i, 0, 0))
    out_shape = [jax.ShapeDtypeStruct((g, SSM_KDIM, SSM_LANES), BF16),
                 jax.ShapeDtypeStruct((g, SSM_KDIM, SSM_KDIM), BF16),
                 jax.ShapeDtypeStruct((g, SSM_KDIM, SSM_LANES), BF16),
                 jax.ShapeDtypeStruct((g, 1, SSM_LANES), F32),
                 jax.ShapeDtypeStruct((g, 1, SSM_LANES), F32)]
    return pl.pallas_call(
        functools.partial(_ssm_params_kernel, seg_chunks=seg_chunks),
        grid=(g,),
        in_specs=[blk(a) for a in args],
        out_specs=[blk(o) for o in out_shape],
        out_shape=out_shape,
        compiler_params=_cparams(1),
    )(*args)


def _ssm_params_kernel(lr_ref, li_ref, ar_ref, ai_ref, btr_ref, bti_ref, ccr_ref, cci_ref,
                       win_ref, toep_ref, cout_ref, achunk_ref, aseg_ref, *, seg_chunks):
    t, nch, half = SSM_CHUNK, S5_GROUP_CH, 2 * S5_STATE
    hi = lax.Precision.HIGHEST
    lr, li = lr_ref[0], li_ref[0]

    def power(k):
        mag = jnp.exp(lr * k)
        return mag * jnp.cos(li * k), mag * jnp.sin(li * k)

    pw = [power(float(k)) for k in range(t + 1)]
    fwd1 = lax.broadcasted_iota(jnp.int32, (1, half), 1) < S5_STATE
    pick = lambda kf, kb: (jnp.where(fwd1, pw[kf][0], pw[kb][0]), jnp.where(fwd1, pw[kf][1], pw[kb][1]))
    ar, ai = ar_ref[0], ai_ref[0]
    xr, xi = pw[1][0] - 1.0, pw[1][1]
    den = ar * ar + ai * ai
    qr, qi = (xr * ar + xi * ai) / den, (xi * ar - xr * ai) / den
    bbr = qr * btr_ref[0] - qi * bti_ref[0]
    bbi = qr * bti_ref[0] + qi * btr_ref[0]
    ccr, cci = ccr_ref[0], cci_ref[0]

    for s in range(t):
        rows = slice(s * nch, (s + 1) * nch)
        er, ei = pick(t - 1 - s, s)
        win_ref[0, rows, 0:half] = (er * bbr - ei * bbi).astype(BF16)
        win_ref[0, rows, half:] = (er * bbi + ei * bbr).astype(BF16)
        er, ei = pick(s + 1, t - s)
        cout_ref[0, rows, 0:half] = (ccr * er - cci * ei).astype(BF16)
        cout_ref[0, rows, half:] = (-(ccr * ei + cci * er)).astype(BF16)

    fwd = lax.broadcasted_iota(jnp.int32, (nch, half), 1) < S5_STATE
    cp = [(ccr * pw[k][0] - cci * pw[k][1], ccr * pw[k][1] + cci * pw[k][0]) for k in range(t)]
    stack = lambda ks, j: jnp.concatenate([cp[k][j] for k in ks], axis=0)
    nt = lambda x, y: lax.dot_general(x, y, (((1,), (1,)), ((), ())), precision=hi, preferred_element_type=F32)
    zero = jnp.zeros_like(bbr)
    order, rev = list(range(t)), list(range(t - 1, -1, -1))
    taps_f = (nt(jnp.where(fwd, bbr, zero), stack(order, 0)) - nt(jnp.where(fwd, bbi, zero), stack(order, 1)))
    taps_b = (nt(jnp.where(fwd, zero, bbr), stack(rev, 0)) - nt(jnp.where(fwd, zero, bbi), stack(rev, 1)))
    lane = lax.broadcasted_iota(jnp.int32, (nch, SSM_KDIM), 1)
    for s in range(t):
        tf = taps_f if s == 0 else pltpu.roll(taps_f, s * nch, axis=1)
        tb = taps_b if s == t - 1 else pltpu.roll(taps_b, (s + 1) * nch, axis=1)
        blk = jnp.where(lane >= s * nch, tf, 0.0) + jnp.where(lane < (s + 1) * nch, tb, 0.0)
        toep_ref[0, s * nch:(s + 1) * nch, :] = blk.astype(BF16)

    for ref, k in ((achunk_ref, t), (aseg_ref, t * seg_chunks)):
        dr, di = power(float(k))
        ref[0, :, 0:half] = dr
        ref[0, :, half:] = di


def _cmul_add(ar, ai, h_re, h_im, s_re, s_im):
    return ar * h_re - ai * h_im + s_re, ar * h_im + ai * h_re + s_im


def _regroup_perms():
    n = SSM_CHUNK * V7X_LANES
    idx = jnp.arange(n)
    tok, grp, ch = idx // V7X_LANES, idx % V7X_LANES // S5_GROUP_CH, idx % S5_GROUP_CH
    dst = grp * SSM_KDIM + tok * S5_GROUP_CH + ch
    p_in = (dst[:, None] == idx[None, :]).astype(BF16)
    return p_in, p_in.T


def _regroup_in(u_ref, pin_ref, lhs_scr, nrows):
    t, rbk = SSM_CHUNK, min(SSM_REGROUP_ROWS, nrows)

    def body(rb, carry):
        pieces = [u_ref[pl.ds(rb * (rbk * t) + tt, rbk, stride=t), :].astype(BF16) for tt in range(t)]
        out = jnp.dot(jnp.concatenate(pieces, axis=1), pin_ref[...], preferred_element_type=F32)
        rows = pl.ds(pl.multiple_of(rb * rbk, rbk), rbk)
        for gl in range(SSM_TILE_GROUPS):
            lhs_scr[gl, rows, :] = out[:, gl * SSM_KDIM:(gl + 1) * SSM_KDIM].astype(BF16)
        return carry

    lax.fori_loop(0, nrows // rbk, body, 0)


def _regroup_out(y_scr, pout_ref, y_ref, nrows):
    t, rbk = SSM_CHUNK, min(SSM_REGROUP_ROWS, nrows)

    def body(rb, carry):
        rows = pl.ds(pl.multiple_of(rb * rbk, rbk), rbk)
        rest = jnp.concatenate([y_scr[gl, rows, :] for gl in range(SSM_TILE_GROUPS)], axis=1)
        out = None
        for _ in range(3):
            term = rest.astype(BF16)
            rest = rest - term.astype(F32)
            part = jnp.dot(term, pout_ref[...], preferred_element_type=F32)
            out = part if out is None else out + part
        for tt in range(t):
            y_ref[pl.ds(rb * (rbk * t) + tt, rbk, stride=t), :] = out[:, tt * V7X_LANES:(tt + 1) * V7X_LANES]
        return carry

    lax.fori_loop(0, nrows // rbk, body, 0)


def _ssm_core_kernel(u_ref, pin_ref, pout_ref, win_ref, toep_ref, cout_ref, a_ref, aseg_ref, h0_ref, y_ref, hfin_ref,
                     lhs_scr, y_scr, s_re, s_im, tf_re, tb_re, tf_im, tb_im, *, nseq, nc, nsub):
    g = pl.program_id(1)
    half = SSM_LANES // 2
    pitch = nc + SSM_PITCH_PAD
    nrows = nseq * nc
    nslab = nseq // V7X_SUBLANES

    @pl.when(g == 0)
    def _():
        _regroup_in(u_ref, pin_ref, lhs_scr, nrows)

    lhs = lhs_scr[g]
    s = jnp.dot(lhs, win_ref[0], preferred_element_type=F32)
    for p in range(nseq):
        s_re[p * pitch:p * pitch + nc, :] = s[p * nc:(p + 1) * nc, 0:half]
        s_im[p * pitch:p * pitch + nc, :] = s[p * nc:(p + 1) * nc, half:]

    ar, ai = a_ref[0][:, 0:half], a_ref[0][:, half:]
    fwd = lax.broadcasted_iota(jnp.int32, (V7X_SUBLANES, half), 1) < S5_STATE

    def slab(q, c):
        return pl.ds(q * V7X_SUBLANES * pitch + c, V7X_SUBLANES, stride=pitch)

    def scan(init, store):
        hs = list(init)
        for c in range(nc):
            cb = nc - 1 - c
            for q in range(nslab):
                h_re, h_im = hs[q]
                x_re = jnp.where(fwd, s_re[slab(q, c), :], s_re[slab(q, cb), :])
                x_im = jnp.where(fwd, s_im[slab(q, c), :], s_im[slab(q, cb), :])
                if store:
                    tf_re[slab(q, c), :] = h_re
                    tb_re[slab(q, cb), :] = h_re
                    tf_im[slab(q, c), :] = h_im
                    tb_im[slab(q, cb), :] = h_im
                hs[q] = _cmul_add(ar, ai, h_re, h_im, x_re, x_im)
        return hs

    rows8 = lambda q: slice(q * V7X_SUBLANES, (q + 1) * V7X_SUBLANES)
    init = [(h0_ref[0, rows8(q), 0:half], h0_ref[0, rows8(q), half:]) for q in range(nslab)]
    if nsub > 1:
        zero = jnp.zeros((V7X_SUBLANES, half), F32)
        ends = scan([(zero, zero)] * nslab, False)
        gr, gi = aseg_ref[0][:, 0:half], aseg_ref[0][:, half:]
        seg = lax.broadcasted_iota(jnp.int32, (V7X_SUBLANES, half), 0) % nsub
        carried = jnp.where(fwd, seg, nsub - 1 - seg) != 0
        shift = lambda x: jnp.where(fwd, pltpu.roll(x, 1, axis=0), pltpu.roll(x, V7X_SUBLANES - 1, axis=0))
        for q in range(nslab):
            (h0_re, h0_im), (f_re, f_im) = init[q], ends[q]
            e_re, e_im = h0_re, h0_im
            for _ in range(nsub - 1):
                x_re, x_im = _cmul_add(gr, gi, e_re, e_im, f_re, f_im)
                e_re = h0_re + jnp.where(carried, shift(x_re), 0.0)
                e_im = h0_im + jnp.where(carried, shift(x_im), 0.0)
            init[q] = (e_re, e_im)
    hs = scan(init, True)
    for q in range(nslab):
        hfin_ref[0, rows8(q), 0:half] = hs[q][0]
        hfin_ref[0, rows8(q), half:] = hs[q][1]

    gather = lambda ref: jnp.concatenate([ref[p * pitch:p * pitch + nc, :] for p in range(nseq)], axis=0)
    fwd_rows = lax.broadcasted_iota(jnp.int32, (nrows, half), 1) < S5_STATE
    hp = jnp.concatenate([jnp.where(fwd_rows, gather(tf_re), gather(tb_re)),
                          jnp.where(fwd_rows, gather(tf_im), gather(tb_im))], axis=1).astype(BF16)
    y_scr[g] = (jnp.dot(lhs, toep_ref[0], preferred_element_type=F32)
                + lax.dot_general(hp, cout_ref[0], (((1,), (1,)), ((), ())), preferred_element_type=F32))

    @pl.when(g == SSM_TILE_GROUPS - 1)
    def _():
        _regroup_out(y_scr, pout_ref, y_ref, nrows)


def _ssm_core(u, ops, h0, nseq, nc, nsub):
    r, d = u.shape
    nrows = nseq * nc
    assert r == nrows * SSM_CHUNK and nseq % V7X_SUBLANES == 0 and V7X_SUBLANES % nsub == 0
    assert nrows % min(SSM_REGROUP_ROWS, nrows) == 0 and nc % V7X_SUBLANES == 0
    ntile = d // V7X_LANES
    grp = lambda j, g: (j * SSM_TILE_GROUPS + g, 0, 0)
    scan_rows = nseq * (nc + SSM_PITCH_PAD)
    p_in, p_out = _regroup_perms()
    perm_spec = pl.BlockSpec(p_in.shape, lambda j, g: (0, 0), pipeline_mode=pl.Buffered(1))
    return pl.pallas_call(
        functools.partial(_ssm_core_kernel, nseq=nseq, nc=nc, nsub=nsub),
        grid=(ntile, SSM_TILE_GROUPS),
        in_specs=[pl.BlockSpec((r, V7X_LANES), lambda j, g: (0, j)),
                  perm_spec, perm_spec]
                 + [pl.BlockSpec((1,) + o.shape[1:], grp) for o in ops]
                 + [pl.BlockSpec((1, nseq, SSM_LANES), grp)],
        out_specs=[pl.BlockSpec((r, V7X_LANES), lambda j, g: (0, j)),
                   pl.BlockSpec((1, nseq, SSM_LANES), grp)],
        out_shape=[jax.ShapeDtypeStruct((r, d), F32),
                   jax.ShapeDtypeStruct((d // S5_GROUP_CH, nseq, SSM_LANES), F32)],
        scratch_shapes=[pltpu.VMEM((SSM_TILE_GROUPS, nrows, SSM_KDIM), BF16),
                        pltpu.VMEM((SSM_TILE_GROUPS, nrows, SSM_KDIM), F32)]
                       + [pltpu.VMEM((scan_rows, V7X_LANES), F32)] * 6,
        compiler_params=_cparams(2),
    )(u, p_in, p_out, *ops, h0)


def _glu_kernel(y_ref, u_ref, d_ref, w_ref, z_ref):
    v = y_ref[...] + d_ref[...] * u_ref[...]
    z = 0.5 * v * (1.0 + jnp.tanh(math.sqrt(2.0 / math.pi) * (v + 0.044715 * (v * v * v))))
    gl = jnp.dot(z.astype(BF16), w_ref[...], preferred_element_type=F32)
    z_ref[...] = (z * _sigmoid(gl)).astype(BF16)


def _glu(y, u, d, w, tm):
    r, dm = y.shape
    return pl.pallas_call(
        _glu_kernel,
        grid=(r // tm,),
        in_specs=[pl.BlockSpec((tm, dm), lambda i: (i, 0)),
                  pl.BlockSpec((tm, dm), lambda i: (i, 0)),
                  pl.BlockSpec((1, dm), lambda i: (0, 0)),
                  pl.BlockSpec(w.shape, lambda i: (0, 0), pipeline_mode=pl.Buffered(1))],
        out_specs=pl.BlockSpec((tm, dm), lambda i: (i, 0)),
        out_shape=jax.ShapeDtypeStruct((r, dm), BF16),
        compiler_params=_cparams(1),
    )(y, u, d, w)


def _rope_tables(length):
    rows = length // GRID_W
    row = jnp.repeat(jnp.arange(rows, dtype=F32), GRID_W)
    col = jnp.tile(jnp.arange(GRID_W, dtype=F32), rows)
    inv = ROPE_BASE ** (-jnp.arange(ROPE_PAIRS_PER_AXIS, dtype=F32) / ROPE_PAIRS_PER_AXIS)
    ang = jnp.concatenate([row[:, None] * inv, col[:, None] * inv], axis=-1)
    cos, sin = jnp.cos(ang), jnp.sin(ang)
    return jnp.concatenate([cos, cos], axis=-1), jnp.concatenate([-sin, sin], axis=-1)


def kernel(x_prompt, x_sample, c, cache_a_k, cache_a_v, cache_b_k, cache_b_v, state_ssm, c_ctx, ada_w, ada_b, norm_g, mlp_w1, mlp_w2, attn_w_in, attn_w_out, attn_qk_norm, diff_lambda, diff_subln, ssm_w_in, ssm_a_re, ssm_a_im, ssm_log_dt, ssm_b_re, ssm_b_im, ssm_c_re, ssm_c_im, ssm_d, ssm_glu_w, ssm_w_out):
    bp, lp, d = x_prompt.shape
    bs, ls, _ = x_sample.shape
    depth = ada_w.shape[0]
    past = cache_a_k.shape[2]
    nsub = SSM_SEGMENTS
    assert bs + 1 <= 8 and bp % V7X_SUBLANES == 0
    assert lp % SSM_CHUNK == 0 and ls % (SSM_CHUNK * nsub) == 0 and ls % GRID_W == 0

    cond8 = jnp.zeros((8, d), F32).at[0].set(c_ctx).at[1:1 + bs].set(c)
    mods = _modulation(cond8, ada_w, ada_b).reshape(depth, 8 * N_MOD, 1, d)

    xp = x_prompt.reshape(bp * lp, d)
    xs = x_sample.reshape(bs * ls, d)
    tm = min(512, bp * lp, ls)
    tm_big = min(1024, bp * lp, ls)
    row_p = lambda i: 0
    row_s = lambda i: 1 + i // (ls // tm)
    row_s_big = lambda i: 1 + i // (ls // tm_big)
    w1_all = mlp_w1.astype(BF16)
    w2_all = mlp_w2.astype(BF16)
    new_ak = new_av = new_bk = new_bv = new_ssm = None
    for l in range(depth):
        mod = mods[l]
        g = norm_g[l][:, None, :]
        i = l // 2
        if l % 2 == 0:
            lam_init = 0.8 - 0.6 * math.exp(-0.3 * l)
            w_in = attn_w_in[i].astype(BF16)
            w_out = attn_w_out[i].astype(BF16)
            qkv, ka, va, kb, vb = _qkv_proj(xp, g[0], mod, row_p, w_in, attn_qk_norm[i], None, True, tm_big)
            new_ak = ka.reshape(bp, 1, lp, A_KV_HEADS, HEAD_DIM)
            new_av = va.reshape(bp, 1, lp, A_KV_HEADS, HEAD_DIM)
            new_bk = kb.reshape(bp, 1, lp, B_HEADS, 2, HEAD_DIM)
            new_bv = vb.reshape(bp, 1, lp, B_HEADS, B_V_DIM)
            oa, ob = _attention(qkv, bp, None, diff_lambda[i], diff_subln[i][None, :], lam_init,
                                tq_a=lp, tq_b=lp, tk=512)
            xp = _outproj([oa, ob], w_out, xp, g[1], mod, row_p, tm)
            (qkv,) = _qkv_proj(xs, g[0], mod, row_s_big, w_in, attn_qk_norm[i], _rope_tables(ls), False, tm_big)
            caches = (cache_a_k[:, i].reshape(bs * past, A_KV_HEADS * HEAD_DIM),
                      cache_a_v[:, i].reshape(bs * past, A_KV_HEADS * HEAD_DIM),
                      cache_b_k[:, i].reshape(bs * past, B_HEADS * 2 * HEAD_DIM),
                      cache_b_v[:, i].reshape(bs * past, B_HEADS * B_V_DIM))
            oa, ob = _attention(qkv, bs, caches, diff_lambda[i], diff_subln[i][None, :], lam_init,
                                tq_a=min(256, ls), tq_b=min(512, ls), tk=512)
            xs = _outproj([oa, ob], w_out, xs, g[1], mod, row_s, tm)
        else:
            w_in = ssm_w_in[i].astype(BF16)
            u_p = _ssm_in(xp, g[0], mod, row_p, w_in, tm_big, 1024)
            u_s = _ssm_in(xs, g[0], mod, row_s_big, w_in, tm_big, 1024)
            ng = d // S5_GROUP_CH
            nc_p, nc_s = lp // SSM_CHUNK, ls // SSM_CHUNK // nsub
            ops = _ssm_params(ssm_a_re[i], ssm_a_im[i], ssm_log_dt[i], ssm_b_re[i], ssm_b_im[i],
                                                 ssm_c_re[i], ssm_c_im[i], nc_s)
            st = state_ssm[:, i].astype(F32)
            h0 = jnp.zeros((ng, bs, nsub, 4, S5_STATE), F32)
            h0 = h0.at[:, :, 0, 0].set(jnp.transpose(st[:, 0, :, :, 0], (1, 0, 2)))
            h0 = h0.at[:, :, 0, 2].set(jnp.transpose(st[:, 0, :, :, 1], (1, 0, 2)))
            h0 = h0.at[:, :, nsub - 1, 1].set(jnp.transpose(st[:, 1, :, :, 0], (1, 0, 2)))
            h0 = h0.at[:, :, nsub - 1, 3].set(jnp.transpose(st[:, 1, :, :, 1], (1, 0, 2)))
            h0 = h0.reshape(ng, bs * nsub, SSM_LANES)
            y_p, hfin = _ssm_core(u_p, ops, jnp.zeros((ng, bp, SSM_LANES), F32), bp, nc_p, 1)
            y_s, _ = _ssm_core(u_s, ops, h0, bs * nsub, nc_s, nsub)
            new_ssm = jnp.transpose(hfin.reshape(ng, bp, 2, 2, S5_STATE), (1, 3, 0, 4, 2))
            glu_w = ssm_glu_w[i].astype(BF16)
            w_out = ssm_w_out[i].astype(BF16)
            dvec = ssm_d[i][None, :]
            z_p = _glu(y_p, u_p, dvec, glu_w, tm)
            z_s = _glu(y_s, u_s, dvec, glu_w, tm)
            xp = _outproj([z_p], w_out, xp, g[1], mod, row_p, tm)
            xs = _outproj([z_s], w_out, xs, g[1], mod, row_s, tm)
        xp = _mlp(xp, g[2], g[3], mod, row_p, w1_all, w2_all, l, tm_big, 512)
        xs = _mlp(xs, g[2], g[3], mod, row_s_big, w1_all, w2_all, l, tm_big, 512)
    return (xp.reshape(bp, lp, d), xs.reshape(bs, ls, d), new_ak, new_av, new_bk, new_bv,
            new_ssm[:, None])
```

```python
import functools
import math

import jax
import jax.numpy as jnp
from jax import lax
from jax.experimental import pallas as pl
from jax.experimental.pallas import tpu as pltpu

F32 = jnp.float32
BF16 = jnp.bfloat16
EPS = 1e-6

HEAD_DIM = 128
A_HEADS = 8
A_KV_HEADS = 2
A_GROUP = A_HEADS // A_KV_HEADS
B_HEADS = 4
B_V_DIM = 2 * HEAD_DIM
GRID_W = 64
ROPE_BASE = 10000.0
ROPE_PAIRS_PER_AXIS = HEAD_DIM // 4
N_MOD = 6
S5_GROUP_CH = 16
S5_STATE = 64

V7X_LANES = 128
V7X_SUBLANES = 8
V7X_VMEM_LIMIT_BYTES = 56 * 1024 * 1024

SSM_CHUNK = V7X_SUBLANES
SSM_TILE_GROUPS = V7X_LANES // S5_GROUP_CH
SSM_KDIM = SSM_CHUNK * S5_GROUP_CH
SSM_LANES = 4 * S5_STATE
SSM_GROUPS_PER_STEP = 2
SSM_SEGMENTS = 8
SSM_REGROUP_ROWS = 256
SSM_PITCH_PAD = 8

MLP_ACC_CHUNK = 512
QKV_TN = 512


def _cparams(n_axes):
    return pltpu.CompilerParams(dimension_semantics=("arbitrary",) * n_axes,
                                vmem_limit_bytes=V7X_VMEM_LIMIT_BYTES)


def _sigmoid(x):
    return 1.0 / (1.0 + jnp.exp(-x))


def _rms(x, g):
    return x * lax.rsqrt(jnp.mean(x * x, axis=-1, keepdims=True) + EPS) * g


def _mod_spec(d, slot, row_fn):
    return pl.BlockSpec((1, 1, d), lambda i, *_: (row_fn(i) * N_MOD + slot, 0, 0))


def _mod_kernel(c_ref, w_ref, b_ref, o_ref):
    c = c_ref[...]
    s = (c * _sigmoid(c)).astype(BF16)
    o_ref[0] = jnp.dot(s, w_ref[0].astype(BF16), preferred_element_type=F32) + b_ref[0]


def _modulation(cond8, ada_w, ada_b):
    depth, d, n = ada_w.shape
    tn = 1024
    return pl.pallas_call(
        _mod_kernel,
        grid=(depth, n // tn),
        in_specs=[pl.BlockSpec((8, d), lambda l, j: (0, 0)),
                  pl.BlockSpec((1, d, tn), lambda l, j: (l, 0, j)),
                  pl.BlockSpec((1, 1, tn), lambda l, j: (l, 0, j))],
        out_specs=pl.BlockSpec((1, 8, tn), lambda l, j: (l, 0, j)),
        out_shape=jax.ShapeDtypeStruct((depth, 8, n), F32),
        compiler_params=_cparams(2),
    )(cond8, ada_w, ada_b.reshape(depth, 1, n))


def _split_heads(y, fn):
    return jnp.concatenate([fn(y[:, h * HEAD_DIM:(h + 1) * HEAD_DIM]) for h in range(y.shape[1] // HEAD_DIM)], axis=1)


def _qkv_kernel(*refs, rope, caches):
    x_ref, g_ref, sh_ref, sc_ref, w_ref, qkg_ref = refs[:6]
    refs = refs[6:]
    if rope:
        cos_ref, sin_ref = refs[:2]
        refs = refs[2:]
    qkv_ref = refs[0]
    if caches:
        ka_ref, va_ref, kb_ref, vb_ref = refs[1:5]
    n_scr = refs[-1]
    j = pl.program_id(1)

    @pl.when(j == 0)
    def _():
        n = _rms(x_ref[...], g_ref[...]) * (1.0 + sc_ref[0]) + sh_ref[0]
        n_scr[...] = n.astype(BF16)

    def rot(y):
        if not rope:
            return y
        return y * cos_ref[...] + pltpu.roll(y, HEAD_DIM // 2, axis=1) * sin_ref[...]

    kv_w = A_KV_HEADS * HEAD_DIM
    halves = (slice(0, kv_w), slice(kv_w, 2 * kv_w))

    def acc(cols):
        return jnp.dot(n_scr[...], w_ref[:, cols], preferred_element_type=F32)

    q_scale = HEAD_DIM ** -0.5 * math.log2(math.e)
    b_qa = A_HEADS * HEAD_DIM // QKV_TN
    b_va = b_qa + 2 * kv_w // QKV_TN
    b_qb = b_va + B_HEADS * 2 * HEAD_DIM // QKV_TN
    b_kb = b_qb + B_HEADS * 2 * HEAD_DIM // QKV_TN

    @pl.when(j < b_qa)
    def _():
        for cols in halves:
            qkv_ref[:, cols] = _split_heads(acc(cols), lambda y: rot(_rms(y, qkg_ref[0:1, :])) * q_scale).astype(BF16)

    @pl.when((j >= b_qa) & (j < b_va))
    def _():
        kn = _split_heads(acc(halves[0]), lambda t: _rms(t, qkg_ref[1:2, :]))
        v = acc(halves[1])
        if caches:
            ka_ref[...] = kn
            va_ref[...] = v
        qkv_ref[:, halves[0]] = _split_heads(kn, rot).astype(BF16)
        qkv_ref[:, halves[1]] = v.astype(BF16)

    @pl.when((j >= b_va) & (j < b_qb))
    def _():
        for cols in halves:
            qkv_ref[:, cols] = _split_heads(acc(cols), lambda y: rot(y) * q_scale).astype(BF16)

    @pl.when((j >= b_qb) & (j < b_kb))
    def _():
        for cols in halves:
            y = acc(cols)
            if caches:
                kb_ref[:, cols] = y
            qkv_ref[:, cols] = _split_heads(y, rot).astype(BF16)

    @pl.when(j >= b_kb)
    def _():
        for cols in halves:
            y = acc(cols)
            if caches:
                vb_ref[:, cols] = y
            qkv_ref[:, cols] = y.astype(BF16)


def _qkv_proj(x, g, mod, row_fn, w, qk_g, rope_tabs, caches, tm):
    r, d = x.shape
    n = w.shape[1]
    tn = QKV_TN
    kv_w = A_KV_HEADS * HEAD_DIM
    assert 2 * kv_w == tn and r % tm == 0 and n % tn == 0
    assert not (caches and rope_tabs is not None)
    in_specs = [pl.BlockSpec((tm, d), lambda i, j: (i, 0)),
                pl.BlockSpec((1, d), lambda i, j: (0, 0)),
                _mod_spec(d, 0, row_fn), _mod_spec(d, 1, row_fn),
                pl.BlockSpec((d, tn), lambda i, j: (0, j)),
                pl.BlockSpec((2, HEAD_DIM), lambda i, j: (0, 0))]
    args = [x, g, mod, mod, w, qk_g]
    if rope_tabs is not None:
        nblk = rope_tabs[0].shape[0] // tm
        in_specs += [pl.BlockSpec((tm, HEAD_DIM), lambda i, j: (i % nblk, 0))] * 2
        args += list(rope_tabs)
    out_specs = [pl.BlockSpec((tm, tn), lambda i, j: (i, j))]
    out_shape = [jax.ShapeDtypeStruct((r, n), BF16)]
    if caches:
        b_kb = (A_HEADS + 2 * A_KV_HEADS + 2 * B_HEADS) * HEAD_DIM // tn
        nb = B_HEADS * 2 * HEAD_DIM // tn
        out_specs += [pl.BlockSpec((tm, kv_w), lambda i, j: (i, 0)),
                      pl.BlockSpec((tm, kv_w), lambda i, j: (i, 0)),
                      pl.BlockSpec((tm, tn), lambda i, j: (i, jnp.clip(j - b_kb, 0, nb - 1))),
                      pl.BlockSpec((tm, tn), lambda i, j: (i, jnp.clip(j - b_kb - nb, 0, nb - 1)))]
        out_shape += [jax.ShapeDtypeStruct((r, kv_w), F32), jax.ShapeDtypeStruct((r, kv_w), F32),
                      jax.ShapeDtypeStruct((r, nb * tn), F32), jax.ShapeDtypeStruct((r, nb * tn), F32)]
    return pl.pallas_call(
        functools.partial(_qkv_kernel, rope=rope_tabs is not None, caches=caches),
        grid=(r // tm, n // tn),
        in_specs=in_specs, out_specs=out_specs, out_shape=out_shape,
        scratch_shapes=[pltpu.VMEM((tm, d), BF16)],
        compiler_params=_cparams(2),
    )(*args)


def _lane_tiles(x, op, acc):
    for j in range(0, x.shape[1], V7X_LANES):
        t = x[:, j:j + V7X_LANES]
        acc = t if acc is None else op(acc, t)
    return acc


def _softmax_pv(q, k_chunks, v_chunks, s_scr, rb):
    m_rows = q.shape[0]
    outs = []
    for r0 in range(0, m_rows, rb):
        qr = q[r0:r0 + rb]
        m_part = None
        off = 0
        for kc in k_chunks:
            k = kc()
            s = lax.dot_general(qr, k, (((1,), (1,)), ((), ())), preferred_element_type=F32)
            s_scr[r0:r0 + rb, off:off + k.shape[0]] = s
            m_part = _lane_tiles(s, jnp.maximum, m_part)
            off += k.shape[0]
        m = jnp.max(m_part, axis=-1, keepdims=True)
        l_part = acc = None
        off = 0
        for vc in v_chunks:
            v = vc()
            p = jnp.exp2(s_scr[r0:r0 + rb, off:off + v.shape[0]] - m)
            l_part = _lane_tiles(p, jnp.add, l_part)
            pv = jnp.dot(p.astype(BF16), v, preferred_element_type=F32)
            acc = pv if acc is None else acc + pv
            off += v.shape[0]
        outs.append(acc / jnp.sum(l_part, axis=-1, keepdims=True))
    return jnp.concatenate(outs, axis=0)


def _chunks(ref, c0, width, tk, cache_ref=None):
    length = ref.shape[0]
    out = [(lambda s=s: ref[s:s + tk, c0:c0 + width]) for s in range(0, length, tk)]
    if cache_ref is not None:
        out.append(lambda: cache_ref[:, c0:c0 + width].astype(BF16))
    return out


def _gqa_kernel(*refs, has_cache, tk, rb):
    if has_cache:
        q_ref, k_ref, v_ref, ck_ref, cv_ref, o_ref, s_scr = refs
    else:
        q_ref, k_ref, v_ref, o_ref, s_scr = refs
        ck_ref = cv_ref = None
    tq = q_ref.shape[0]
    wq = A_GROUP * HEAD_DIM
    for hh in range(k_ref.shape[1] // HEAD_DIM):
        q = jnp.concatenate([q_ref[:, hh * wq + g * HEAD_DIM:hh * wq + (g + 1) * HEAD_DIM] for g in range(A_GROUP)],
                            axis=0)
        o = _softmax_pv(q, _chunks(k_ref, hh * HEAD_DIM, HEAD_DIM, tk, ck_ref),
                        _chunks(v_ref, hh * HEAD_DIM, HEAD_DIM, tk, cv_ref), s_scr, rb)
        o_ref[:, hh * wq:(hh + 1) * wq] = jnp.concatenate([o[g * tq:(g + 1) * tq] for g in range(A_GROUP)],
                                                          axis=1).astype(BF16)


def _diff_kernel(*refs, has_cache, tk, rb, lam_init):
    if has_cache:
        q_ref, k_ref, v_ref, ck_ref, cv_ref, lam_ref, sg_ref, o_ref, s_scr = refs
    else:
        q_ref, k_ref, v_ref, lam_ref, sg_ref, o_ref, s_scr = refs
        ck_ref = cv_ref = None
    lp = lam_ref[...]
    lam = (jnp.exp(jnp.sum(lp[0:1] * lp[1:2], axis=-1, keepdims=True))
           - jnp.exp(jnp.sum(lp[2:3] * lp[3:4], axis=-1, keepdims=True)) + lam_init)
    for hh in range(v_ref.shape[1] // B_V_DIM):
        c0 = hh * B_V_DIM
        outs = []
        for m in range(2):
            cm = c0 + m * HEAD_DIM
            outs.append(_softmax_pv(q_ref[:, cm:cm + HEAD_DIM], _chunks(k_ref, cm, HEAD_DIM, tk, ck_ref),
                                    _chunks(v_ref, c0, B_V_DIM, tk, cv_ref), s_scr, rb))
        o = outs[0] - lam * outs[1]
        o_ref[:, c0:c0 + B_V_DIM] = (_rms(o, sg_ref[...]) * (1.0 - lam_init)).astype(BF16)


def _attention(qkv, nb, caches, lam_p, subln_g, lam_init, tq_a, tq_b, tk, hp):
    r = qkv.shape[0]
    length = r // nb
    has_cache = caches is not None
    lk = length + (caches[0].shape[0] // nb if has_cache else 0)
    tk = min(tk, length)
    wk = hp * HEAD_DIM
    ka0, va0 = A_HEADS * HEAD_DIM // wk, (A_HEADS + A_KV_HEADS) * HEAD_DIM // wk
    wb = hp * B_V_DIM
    qb0 = (A_HEADS + 2 * A_KV_HEADS) * HEAD_DIM // wb
    kb0, vb0 = qb0 + B_HEADS // hp, qb0 + 2 * B_HEADS // hp
    assert (A_HEADS * HEAD_DIM) % wk == 0 and ((A_HEADS + 2 * A_KV_HEADS) * HEAD_DIM) % wb == 0

    nq = length // tq_a
    wq = hp * A_GROUP * HEAD_DIM
    in_specs = [pl.BlockSpec((tq_a, wq), lambda b, h, i: (b * nq + i, h)),
                pl.BlockSpec((length, wk), lambda b, h, i: (b, ka0 + h)),
                pl.BlockSpec((length, wk), lambda b, h, i: (b, va0 + h))]
    args = [qkv, qkv, qkv]
    if has_cache:
        pc = caches[0].shape[0] // nb
        in_specs += [pl.BlockSpec((pc, wk), lambda b, h, i: (b, h))] * 2
        args += [caches[0], caches[1]]
    rb_a = min(256, A_GROUP * tq_a)
    oa = pl.pallas_call(
        functools.partial(_gqa_kernel, has_cache=has_cache, tk=tk, rb=rb_a),
        grid=(nb, A_KV_HEADS // hp, nq),
        in_specs=in_specs,
        out_specs=pl.BlockSpec((tq_a, wq), lambda b, h, i: (b * nq + i, h)),
        out_shape=jax.ShapeDtypeStruct((r, A_HEADS * HEAD_DIM), BF16),
        scratch_shapes=[pltpu.VMEM((A_GROUP * tq_a, lk), F32)],
        compiler_params=_cparams(3),
    )(*args)

    nq = length // tq_b
    in_specs = [pl.BlockSpec((tq_b, wb), lambda b, h, i: (b * nq + i, qb0 + h)),
                pl.BlockSpec((length, wb), lambda b, h, i: (b, kb0 + h)),
                pl.BlockSpec((length, wb), lambda b, h, i: (b, vb0 + h))]
    args = [qkv, qkv, qkv]
    if has_cache:
        in_specs += [pl.BlockSpec((pc, wb), lambda b, h, i: (b, h))] * 2
        args += [caches[2], caches[3]]
    in_specs += [pl.BlockSpec((4, HEAD_DIM), lambda b, h, i: (0, 0)),
                 pl.BlockSpec((1, B_V_DIM), lambda b, h, i: (0, 0))]
    args += [lam_p, subln_g]
    ob = pl.pallas_call(
        functools.partial(_diff_kernel, has_cache=has_cache, tk=tk, rb=min(256, tq_b), lam_init=lam_init),
        grid=(nb, B_HEADS // hp, nq),
        in_specs=in_specs,
        out_specs=pl.BlockSpec((tq_b, wb), lambda b, h, i: (b * nq + i, h)),
        out_shape=jax.ShapeDtypeStruct((r, B_HEADS * B_V_DIM), BF16),
        scratch_shapes=[pltpu.VMEM((tq_b, lk), F32)],
        compiler_params=_cparams(3),
    )(*args)
    return oa, ob


def _outproj_kernel(*refs, n_in):
    o_refs = refs[:n_in]
    w_refs = refs[n_in:2 * n_in]
    x_ref, g_ref, gate_ref, out_ref = refs[2 * n_in:]
    y = None
    for o_ref, w_ref in zip(o_refs, w_refs):
        t = jnp.dot(o_ref[...], w_ref[...], preferred_element_type=F32)
        y = t if y is None else y + t
    out_ref[...] = x_ref[...] + gate_ref[0] * _rms(y, g_ref[...])


def _outproj(os_, w, x, g, mod, row_fn, tm):
    r, d = x.shape
    n_in = len(os_)
    kw = os_[0].shape[1]
    assert all(o.shape[1] == kw for o in os_) and w.shape == (n_in * kw, d)
    in_specs = ([pl.BlockSpec((tm, kw), lambda i: (i, 0)) for _ in os_]
                + [pl.BlockSpec((kw, d), lambda i, k=k: (k, 0), pipeline_mode=pl.Buffered(1)) for k in range(n_in)]
                + [pl.BlockSpec((tm, d), lambda i: (i, 0)),
                   pl.BlockSpec((1, d), lambda i: (0, 0)),
                   _mod_spec(d, 2, row_fn)])
    return pl.pallas_call(
        functools.partial(_outproj_kernel, n_in=n_in),
        grid=(r // tm,),
        in_specs=in_specs,
        out_specs=pl.BlockSpec((tm, d), lambda i: (i, 0)),
        out_shape=jax.ShapeDtypeStruct((r, d), F32),
        compiler_params=_cparams(1),
    )(*os_, *([w] * n_in), x, g, mod)


def _mlp_kernel(x_ref, g2_ref, sh_ref, sc_ref, gate_ref, g3_ref, w1_ref, w2_ref, out_ref, n_scr):
    f = pl.program_id(1)

    @pl.when(f == 0)
    def _():
        n = _rms(x_ref[...], g2_ref[...]) * (1.0 + sc_ref[0]) + sh_ref[0]
        n_scr[...] = n.astype(BF16)
        out_ref[...] = jnp.zeros_like(out_ref)

    h = jnp.maximum(jnp.dot(n_scr[...], w1_ref[...], preferred_element_type=F32), 0.0)
    hb = (h * h).astype(BF16)
    d = out_ref.shape[1]
    for c0 in range(0, d, MLP_ACC_CHUNK):
        out_ref[:, c0:c0 + MLP_ACC_CHUNK] += jnp.dot(hb, w2_ref[:, c0:c0 + MLP_ACC_CHUNK],
                                                     preferred_element_type=F32)

    @pl.when(f == pl.num_programs(1) - 1)
    def _():
        out_ref[...] = x_ref[...] + gate_ref[0] * _rms(out_ref[...], g3_ref[...])


def _mlp(x, g2, g3, mod, row_fn, w1, w2, layer, tm, tf):
    r, d = x.shape
    dff = w1.shape[2]
    return pl.pallas_call(
        _mlp_kernel,
        grid=(r // tm, dff // tf),
        in_specs=[pl.BlockSpec((tm, d), lambda i, f: (i, 0)),
                  pl.BlockSpec((1, d), lambda i, f: (0, 0)),
                  _mod_spec(d, 3, row_fn), _mod_spec(d, 4, row_fn), _mod_spec(d, 5, row_fn),
                  pl.BlockSpec((1, d), lambda i, f: (0, 0)),
                  pl.BlockSpec((None, d, tf), lambda i, f: (layer, 0, f)),
                  pl.BlockSpec((None, tf, d), lambda i, f: (layer, f, 0))],
        out_specs=pl.BlockSpec((tm, d), lambda i, f: (i, 0)),
        out_shape=jax.ShapeDtypeStruct((r, d), F32),
        scratch_shapes=[pltpu.VMEM((tm, d), BF16)],
        compiler_params=_cparams(2),
    )(x, g2, mod, mod, mod, g3, w1, w2)


def _ssm_in_kernel(x_ref, g_ref, sh_ref, sc_ref, w_ref, u_ref, n_scr):
    @pl.when(pl.program_id(1) == 0)
    def _():
        n = _rms(x_ref[...], g_ref[...]) * (1.0 + sc_ref[0]) + sh_ref[0]
        n_scr[...] = n.astype(BF16)

    u_ref[...] = jnp.dot(n_scr[...], w_ref[...], preferred_element_type=F32)


def _ssm_in(x, g, mod, row_fn, w, tm, tn):
    r, d = x.shape
    n = w.shape[1]
    return pl.pallas_call(
        _ssm_in_kernel,
        grid=(r // tm, n // tn),
        in_specs=[pl.BlockSpec((tm, d), lambda i, j: (i, 0)),
                  pl.BlockSpec((1, d), lambda i, j: (0, 0)),
                  _mod_spec(d, 0, row_fn), _mod_spec(d, 1, row_fn),
                  pl.BlockSpec((d, tn), lambda i, j: (0, j))],
        out_specs=pl.BlockSpec((tm, tn), lambda i, j: (i, j)),
        out_shape=jax.ShapeDtypeStruct((r, n), F32),
        scratch_shapes=[pltpu.VMEM((tm, d), BF16)],
        compiler_params=_cparams(2),
    )(x, g, mod, mod, w)


def _ssm_params(a_re, a_im, log_dt, b_re, b_im, c_re, c_im, seg_chunks):
    g = a_re.shape[1]
    both = lambda x: jnp.transpose(x.astype(F32), (1, 0, 2)).reshape(g, 1, 2 * S5_STATE)
    dt = jnp.exp(log_dt.astype(F32))[..., None]
    b_t = lambda x: jnp.transpose(x.astype(F32), (1, 3, 0, 2)).reshape(g, S5_GROUP_CH, 2 * S5_STATE)
    c_t = lambda x: jnp.transpose(x.astype(F32), (1, 2, 0, 3)).reshape(g, S5_GROUP_CH, 2 * S5_STATE)
    args = (both(a_re * dt), both(a_im * dt), both(a_re), both(a_im), b_t(b_re), b_t(b_im), c_t(c_re), c_t(c_im))
    blk = lambda a: pl.BlockSpec((1,) + a.shape[1:], lambda i: (i, 0, 0))
    out_shape = [jax.ShapeDtypeStruct((g, SSM_KDIM, SSM_LANES), BF16),
                 jax.ShapeDtypeStruct((g, SSM_KDIM, SSM_KDIM), BF16),
                 jax.ShapeDtypeStruct((g, SSM_KDIM, SSM_LANES), BF16),
                 jax.ShapeDtypeStruct((g, 1, SSM_LANES), F32),
                 jax.ShapeDtypeStruct((g, 1, SSM_LANES), F32)]
    return pl.pallas_call(
        functools.partial(_ssm_params_kernel, seg_chunks=seg_chunks),
        grid=(g,),
        in_specs=[blk(a) for a in args],
        out_specs=[blk(o) for o in out_shape],
        out_shape=out_shape,
        compiler_params=_cparams(1),
    )(*args)


def _ssm_params_kernel(lr_ref, li_ref, ar_ref, ai_ref, btr_ref, bti_ref, ccr_ref, cci_ref,
                       win_ref, toep_ref, cout_ref, achunk_ref, aseg_ref, *, seg_chunks):
    t, nch, half = SSM_CHUNK, S5_GROUP_CH, 2 * S5_STATE
    hi = lax.Precision.HIGHEST
    lr, li = lr_ref[0], li_ref[0]

    def power(k):
        mag = jnp.exp(lr * k)
        return mag * jnp.cos(li * k), mag * jnp.sin(li * k)

    pw = [power(float(k)) for k in range(t + 1)]
    fwd1 = lax.broadcasted_iota(jnp.int32, (1, half), 1) < S5_STATE
    pick = lambda kf, kb: (jnp.where(fwd1, pw[kf][0], pw[kb][0]), jnp.where(fwd1, pw[kf][1], pw[kb][1]))
    ar, ai = ar_ref[0], ai_ref[0]
    xr, xi = pw[1][0] - 1.0, pw[1][1]
    den = ar * ar + ai * ai
    qr, qi = (xr * ar + xi * ai) / den, (xi * ar - xr * ai) / den
    bbr = qr * btr_ref[0] - qi * bti_ref[0]
    bbi = qr * bti_ref[0] + qi * btr_ref[0]
    ccr, cci = ccr_ref[0], cci_ref[0]

    for s in range(t):
        rows = slice(s * nch, (s + 1) * nch)
        er, ei = pick(t - 1 - s, s)
        win_ref[0, rows, 0:half] = (er * bbr - ei * bbi).astype(BF16)
        win_ref[0, rows, half:] = (er * bbi + ei * bbr).astype(BF16)
        er, ei = pick(s + 1, t - s)
        cout_ref[0, rows, 0:half] = (ccr * er - cci * ei).astype(BF16)
        cout_ref[0, rows, half:] = (-(ccr * ei + cci * er)).astype(BF16)

    fwd = lax.broadcasted_iota(jnp.int32, (nch, half), 1) < S5_STATE
    cp = [(ccr * pw[k][0] - cci * pw[k][1], ccr * pw[k][1] + cci * pw[k][0]) for k in range(t)]
    stack = lambda ks, j: jnp.concatenate([cp[k][j] for k in ks], axis=0)
    nt = lambda x, y: lax.dot_general(x, y, (((1,), (1,)), ((), ())), precision=hi, preferred_element_type=F32)
    zero = jnp.zeros_like(bbr)
    order, rev = list(range(t)), list(range(t - 1, -1, -1))
    taps_f = (nt(jnp.where(fwd, bbr, zero), stack(order, 0)) - nt(jnp.where(fwd, bbi, zero), stack(order, 1)))
    taps_b = (nt(jnp.where(fwd, zero, bbr), stack(rev, 0)) - nt(jnp.where(fwd, zero, bbi), stack(rev, 1)))
    lane = lax.broadcasted_iota(jnp.int32, (nch, SSM_KDIM), 1)
    for s in range(t):
        tf = taps_f if s == 0 else pltpu.roll(taps_f, s * nch, axis=1)
        tb = taps_b if s == t - 1 else pltpu.roll(taps_b, (s + 1) * nch, axis=1)
        blk = jnp.where(lane >= s * nch, tf, 0.0) + jnp.where(lane < (s + 1) * nch, tb, 0.0)
        toep_ref[0, s * nch:(s + 1) * nch, :] = blk.astype(BF16)

    for ref, k in ((achunk_ref, t), (aseg_ref, t * seg_chunks)):
        dr, di = power(float(k))
        ref[0, :, 0:half] = dr
        ref[0, :, half:] = di


def _cmul_add(ar, ai, h_re, h_im, s_re, s_im):
    return ar * h_re - ai * h_im + s_re, ar * h_im + ai * h_re + s_im


def _regroup_perms():
    n = SSM_CHUNK * V7X_LANES
    idx = jnp.arange(n)
    tok, grp, ch = idx // V7X_LANES, idx % V7X_LANES // S5_GROUP_CH, idx % S5_GROUP_CH
    dst = grp * SSM_KDIM + tok * S5_GROUP_CH + ch
    p_in = (dst[:, None] == idx[None, :]).astype(BF16)
    return p_in, p_in.T


def _regroup_in(u_ref, pin_ref, lhs_scr, nrows):
    t, rbk = SSM_CHUNK, min(SSM_REGROUP_ROWS, nrows)

    def body(rb, carry):
        pieces = [u_ref[pl.ds(rb * (rbk * t) + tt, rbk, stride=t), :].astype(BF16) for tt in range(t)]
        out = jnp.dot(jnp.concatenate(pieces, axis=1), pin_ref[...], preferred_element_type=F32)
        rows = pl.ds(pl.multiple_of(rb * rbk, rbk), rbk)
        for gl in range(SSM_TILE_GROUPS):
            lhs_scr[gl, rows, :] = out[:, gl * SSM_KDIM:(gl + 1) * SSM_KDIM].astype(BF16)
        return carry

    lax.fori_loop(0, nrows // rbk, body, 0)


def _regroup_out(y_scr, pout_ref, y_ref, nrows):
    t, rbk = SSM_CHUNK, min(SSM_REGROUP_ROWS, nrows)

    def body(rb, carry):
        rows = pl.ds(pl.multiple_of(rb * rbk, rbk), rbk)
        rest = jnp.concatenate([y_scr[gl, rows, :] for gl in range(SSM_TILE_GROUPS)], axis=1)
        out = None
        for _ in range(3):
            term = rest.astype(BF16)
            rest = rest - term.astype(F32)
            part = jnp.dot(term, pout_ref[...], preferred_element_type=F32)
            out = part if out is None else out + part
        for tt in range(t):
            y_ref[pl.ds(rb * (rbk * t) + tt, rbk, stride=t), :] = out[:, tt * V7X_LANES:(tt + 1) * V7X_LANES]
        return carry

    lax.fori_loop(0, nrows // rbk, body, 0)


def _ssm_core_kernel(u_ref, pin_ref, pout_ref, win_ref, toep_ref, cout_ref, a_ref, aseg_ref, h0_ref, y_ref, hfin_ref,
                     lhs_scr, y_scr, s_re, s_im, tf_re, tb_re, tf_im, tb_im, *, nseq, nc, nsub):
    step = pl.program_id(1)
    nrows = nseq * nc

    @pl.when(step == 0)
    def _():
        _regroup_in(u_ref, pin_ref, lhs_scr, nrows)

    for k in range(SSM_GROUPS_PER_STEP):
        _ssm_group(k, step * SSM_GROUPS_PER_STEP + k, win_ref, toep_ref, cout_ref, a_ref, aseg_ref, h0_ref, hfin_ref,
                   lhs_scr, y_scr, s_re, s_im, tf_re, tb_re, tf_im, tb_im, nseq, nc, nsub)

    @pl.when(step == pl.num_programs(1) - 1)
    def _():
        _regroup_out(y_scr, pout_ref, y_ref, nrows)


def _ssm_group(k, g, win_ref, toep_ref, cout_ref, a_ref, aseg_ref, h0_ref, hfin_ref,
               lhs_scr, y_scr, s_re, s_im, tf_re, tb_re, tf_im, tb_im, nseq, nc, nsub):
    half = SSM_LANES // 2
    pitch = nc + SSM_PITCH_PAD
    nrows = nseq * nc
    nslab = nseq // V7X_SUBLANES
    base = k * nseq * pitch

    lhs = lhs_scr[g]
    s = jnp.dot(lhs, win_ref[k], preferred_element_type=F32)
    for p in range(nseq):
        s_re[base + p * pitch:base + p * pitch + nc, :] = s[p * nc:(p + 1) * nc, 0:half]
        s_im[base + p * pitch:base + p * pitch + nc, :] = s[p * nc:(p + 1) * nc, half:]

    ar, ai = a_ref[k][:, 0:half], a_ref[k][:, half:]
    fwd = lax.broadcasted_iota(jnp.int32, (V7X_SUBLANES, half), 1) < S5_STATE

    def slab(q, c):
        return pl.ds(base + q * V7X_SUBLANES * pitch + c, V7X_SUBLANES, stride=pitch)

    def scan(init, store):
        hs = list(init)
        for c in range(nc):
            cb = nc - 1 - c
            for q in range(nslab):
                h_re, h_im = hs[q]
                x_re = jnp.where(fwd, s_re[slab(q, c), :], s_re[slab(q, cb), :])
                x_im = jnp.where(fwd, s_im[slab(q, c), :], s_im[slab(q, cb), :])
                if store:
                    tf_re[slab(q, c), :] = h_re
                    tb_re[slab(q, cb), :] = h_re
                    tf_im[slab(q, c), :] = h_im
                    tb_im[slab(q, cb), :] = h_im
                hs[q] = _cmul_add(ar, ai, h_re, h_im, x_re, x_im)
        return hs

    rows8 = lambda q: slice(q * V7X_SUBLANES, (q + 1) * V7X_SUBLANES)
    init = [(h0_ref[k, rows8(q), 0:half], h0_ref[k, rows8(q), half:]) for q in range(nslab)]
    if nsub > 1:
        zero = jnp.zeros((V7X_SUBLANES, half), F32)
        ends = scan([(zero, zero)] * nslab, False)
        gr, gi = aseg_ref[k][:, 0:half], aseg_ref[k][:, half:]
        seg = lax.broadcasted_iota(jnp.int32, (V7X_SUBLANES, half), 0) % nsub
        carried = jnp.where(fwd, seg, nsub - 1 - seg) != 0
        shift = lambda x: jnp.where(fwd, pltpu.roll(x, 1, axis=0), pltpu.roll(x, V7X_SUBLANES - 1, axis=0))
        for q in range(nslab):
            (h0_re, h0_im), (f_re, f_im) = init[q], ends[q]
            e_re, e_im = h0_re, h0_im
            for _ in range(nsub - 1):
                x_re, x_im = _cmul_add(gr, gi, e_re, e_im, f_re, f_im)
                e_re = h0_re + jnp.where(carried, shift(x_re), 0.0)
                e_im = h0_im + jnp.where(carried, shift(x_im), 0.0)
            init[q] = (e_re, e_im)
    hs = scan(init, True)
    for q in range(nslab):
        hfin_ref[k, rows8(q), 0:half] = hs[q][0]
        hfin_ref[k, rows8(q), half:] = hs[q][1]

    gather = lambda ref: jnp.concatenate([ref[base + p * pitch:base + p * pitch + nc, :] for p in range(nseq)],
                                         axis=0)
    fwd_rows = lax.broadcasted_iota(jnp.int32, (nrows, half), 1) < S5_STATE
    hp = jnp.concatenate([jnp.where(fwd_rows, gather(tf_re), gather(tb_re)),
                          jnp.where(fwd_rows, gather(tf_im), gather(tb_im))], axis=1).astype(BF16)
    y_scr[g] = (jnp.dot(lhs, toep_ref[k], preferred_element_type=F32)
                + lax.dot_general(hp, cout_ref[k], (((1,), (1,)), ((), ())), preferred_element_type=F32))


def _ssm_core(u, ops, h0, nseq, nc, nsub):
    r, d = u.shape
    nrows = nseq * nc
    assert r == nrows * SSM_CHUNK and nseq % V7X_SUBLANES == 0 and V7X_SUBLANES % nsub == 0
    assert nrows % min(SSM_REGROUP_ROWS, nrows) == 0 and nc % V7X_SUBLANES == 0
    ntile = d // V7X_LANES
    gps = SSM_GROUPS_PER_STEP
    steps = SSM_TILE_GROUPS // gps
    grp = lambda j, g: (j * steps + g, 0, 0)
    scan_rows = gps * nseq * (nc + SSM_PITCH_PAD)
    p_in, p_out = _regroup_perms()
    perm_spec = pl.BlockSpec(p_in.shape, lambda j, g: (0, 0), pipeline_mode=pl.Buffered(1))
    return pl.pallas_call(
        functools.partial(_ssm_core_kernel, nseq=nseq, nc=nc, nsub=nsub),
        grid=(ntile, steps),
        in_specs=[pl.BlockSpec((r, V7X_LANES), lambda j, g: (0, j)),
                  perm_spec, perm_spec]
                 + [pl.BlockSpec((gps,) + o.shape[1:], grp) for o in ops]
                 + [pl.BlockSpec((gps, nseq, SSM_LANES), grp)],
        out_specs=[pl.BlockSpec((r, V7X_LANES), lambda j, g: (0, j)),
                   pl.BlockSpec((gps, nseq, SSM_LANES), grp)],
        out_shape=[jax.ShapeDtypeStruct((r, d), F32),
                   jax.ShapeDtypeStruct((d // S5_GROUP_CH, nseq, SSM_LANES), F32)],
        scratch_shapes=[pltpu.VMEM((SSM_TILE_GROUPS, nrows, SSM_KDIM), BF16),
                        pltpu.VMEM((SSM_TILE_GROUPS, nrows, SSM_KDIM), F32)]
                       + [pltpu.VMEM((scan_rows, V7X_LANES), F32)] * 6,
        compiler_params=_cparams(2),
    )(u, p_in, p_out, *ops, h0)


def _glu_kernel(y_ref, u_ref, d_ref, w_ref, z_ref):
    v = y_ref[...] + d_ref[...] * u_ref[...]
    z = 0.5 * v * (1.0 + jnp.tanh(math.sqrt(2.0 / math.pi) * (v + 0.044715 * (v * v * v))))
    gl = jnp.dot(z.astype(BF16), w_ref[...], preferred_element_type=F32)
    z_ref[...] = (z * _sigmoid(gl)).astype(BF16)


def _glu(y, u, d, w, tm):
    r, dm = y.shape
    return pl.pallas_call(
        _glu_kernel,
        grid=(r // tm,),
        in_specs=[pl.BlockSpec((tm, dm), lambda i: (i, 0)),
                  pl.BlockSpec((tm, dm), lambda i: (i, 0)),
                  pl.BlockSpec((1, dm), lambda i: (0, 0)),
                  pl.BlockSpec(w.shape, lambda i: (0, 0), pipeline_mode=pl.Buffered(1))],
        out_specs=pl.BlockSpec((tm, dm), lambda i: (i, 0)),
        out_shape=jax.ShapeDtypeStruct((r, dm), BF16),
        compiler_params=_cparams(1),
    )(y, u, d, w)


def _rope_tables(length):
    rows = length // GRID_W
    row = jnp.repeat(jnp.arange(rows, dtype=F32), GRID_W)
    col = jnp.tile(jnp.arange(GRID_W, dtype=F32), rows)
    inv = ROPE_BASE ** (-jnp.arange(ROPE_PAIRS_PER_AXIS, dtype=F32) / ROPE_PAIRS_PER_AXIS)
    ang = jnp.concatenate([row[:, None] * inv, col[:, None] * inv], axis=-1)
    cos, sin = jnp.cos(ang), jnp.sin(ang)
    return jnp.concatenate([cos, cos], axis=-1), jnp.concatenate([-sin, sin], axis=-1)


def kernel(x_prompt, x_sample, c, cache_a_k, cache_a_v, cache_b_k, cache_b_v, state_ssm, c_ctx, ada_w, ada_b, norm_g, mlp_w1, mlp_w2, attn_w_in, attn_w_out, attn_qk_norm, diff_lambda, diff_subln, ssm_w_in, ssm_a_re, ssm_a_im, ssm_log_dt, ssm_b_re, ssm_b_im, ssm_c_re, ssm_c_im, ssm_d, ssm_glu_w, ssm_w_out):
    bp, lp, d = x_prompt.shape
    bs, ls, _ = x_sample.shape
    depth = ada_w.shape[0]
    past = cache_a_k.shape[2]
    nsub = SSM_SEGMENTS
    assert bs + 1 <= 8 and bp % V7X_SUBLANES == 0
    assert lp % SSM_CHUNK == 0 and ls % (SSM_CHUNK * nsub) == 0 and ls % GRID_W == 0

    cond8 = jnp.zeros((8, d), F32).at[0].set(c_ctx).at[1:1 + bs].set(c)
    mods = _modulation(cond8, ada_w, ada_b).reshape(depth, 8 * N_MOD, 1, d)

    xp = x_prompt.reshape(bp * lp, d)
    xs = x_sample.reshape(bs * ls, d)
    tm = min(512, bp * lp, ls)
    tm_big = min(1024, bp * lp, ls)
    row_p = lambda i: 0
    row_s = lambda i: 1 + i // (ls // tm)
    row_s_big = lambda i: 1 + i // (ls // tm_big)
    w1_all = mlp_w1.astype(BF16)
    w2_all = mlp_w2.astype(BF16)
    new_ak = new_av = new_bk = new_bv = new_ssm = None
    for l in range(depth):
        mod = mods[l]
        g = norm_g[l][:, None, :]
        i = l // 2
        if l % 2 == 0:
            lam_init = 0.8 - 0.6 * math.exp(-0.3 * l)
            w_in = attn_w_in[i].astype(BF16)
            w_out = attn_w_out[i].astype(BF16)
            qkv, ka, va, kb, vb = _qkv_proj(xp, g[0], mod, row_p, w_in, attn_qk_norm[i], None, True, tm_big)
            new_ak = ka.reshape(bp, 1, lp, A_KV_HEADS, HEAD_DIM)
            new_av = va.reshape(bp, 1, lp, A_KV_HEADS, HEAD_DIM)
            new_bk = kb.reshape(bp, 1, lp, B_HEADS, 2, HEAD_DIM)
            new_bv = vb.reshape(bp, 1, lp, B_HEADS, B_V_DIM)
            oa, ob = _attention(qkv, bp, None, diff_lambda[i], diff_subln[i][None, :], lam_init,
                                tq_a=lp, tq_b=lp, tk=512, hp=2)
            xp = _outproj([oa, ob], w_out, xp, g[1], mod, row_p, tm)
            (qkv,) = _qkv_proj(xs, g[0], mod, row_s_big, w_in, attn_qk_norm[i], _rope_tables(ls), False, tm_big)
            caches = (cache_a_k[:, i].reshape(bs * past, A_KV_HEADS * HEAD_DIM),
                      cache_a_v[:, i].reshape(bs * past, A_KV_HEADS * HEAD_DIM),
                      cache_b_k[:, i].reshape(bs * past, B_HEADS * 2 * HEAD_DIM),
                      cache_b_v[:, i].reshape(bs * past, B_HEADS * B_V_DIM))
            oa, ob = _attention(qkv, bs, caches, diff_lambda[i], diff_subln[i][None, :], lam_init,
                                tq_a=min(256, ls), tq_b=min(512, ls), tk=512, hp=1)
            xs = _outproj([oa, ob], w_out, xs, g[1], mod, row_s, tm)
        else:
            w_in = ssm_w_in[i].astype(BF16)
            u_p = _ssm_in(xp, g[0], mod, row_p, w_in, tm_big, 1024)
            u_s = _ssm_in(xs, g[0], mod, row_s_big, w_in, tm_big, 1024)
            ng = d // S5_GROUP_CH
            nc_p, nc_s = lp // SSM_CHUNK, ls // SSM_CHUNK // nsub
            ops = _ssm_params(ssm_a_re[i], ssm_a_im[i], ssm_log_dt[i], ssm_b_re[i], ssm_b_im[i],
                                                 ssm_c_re[i], ssm_c_im[i], nc_s)
            st = state_ssm[:, i].astype(F32)
            h0 = jnp.zeros((ng, bs, nsub, 4, S5_STATE), F32)
            h0 = h0.at[:, :, 0, 0].set(jnp.transpose(st[:, 0, :, :, 0], (1, 0, 2)))
            h0 = h0.at[:, :, 0, 2].set(jnp.transpose(st[:, 0, :, :, 1], (1, 0, 2)))
            h0 = h0.at[:, :, nsub - 1, 1].set(jnp.transpose(st[:, 1, :, :, 0], (1, 0, 2)))
            h0 = h0.at[:, :, nsub - 1, 3].set(jnp.transpose(st[:, 1, :, :, 1], (1, 0, 2)))
            h0 = h0.reshape(ng, bs * nsub, SSM_LANES)
            y_p, hfin = _ssm_core(u_p, ops, jnp.zeros((ng, bp, SSM_LANES), F32), bp, nc_p, 1)
            y_s, _ = _ssm_core(u_s, ops, h0, bs * nsub, nc_s, nsub)
            new_ssm = jnp.transpose(hfin.reshape(ng, bp, 2, 2, S5_STATE), (1, 3, 0, 4, 2))
            glu_w = ssm_glu_w[i].astype(BF16)
            w_out = ssm_w_out[i].astype(BF16)
            dvec = ssm_d[i][None, :]
            z_p = _glu(y_p, u_p, dvec, glu_w, tm)
            z_s = _glu(y_s, u_s, dvec, glu_w, tm)
            xp = _outproj([z_p], w_out, xp, g[1], mod, row_p, tm)
            xs = _outproj([z_s], w_out, xs, g[1], mod, row_s, tm)
        xp = _mlp(xp, g[2], g[3], mod, row_p, w1_all, w2_all, l, tm_big, 512)
        xs = _mlp(xs, g[2], g[3], mod, row_s_big, w1_all, w2_all, l, tm_big, 512)
    return (xp.reshape(bp, lp, d), xs.reshape(bs, ls, d), new_ak, new_av, new_bk, new_bv,
            new_ssm[:, None])
```

```python
import functools
import math

import jax
import jax.numpy as jnp
from jax import lax
from jax.experimental import pallas as pl
from jax.experimental.pallas import tpu as pltpu

F32 = jnp.float32
BF16 = jnp.bfloat16
EPS = 1e-6

HEAD_DIM = 128
A_HEADS = 8
A_KV_HEADS = 2
A_GROUP = A_HEADS // A_KV_HEADS
B_HEADS = 4
B_V_DIM = 2 * HEAD_DIM
GRID_W = 64
ROPE_BASE = 10000.0
ROPE_PAIRS_PER_AXIS = HEAD_DIM // 4
N_MOD = 6
S5_GROUP_CH = 16
S5_STATE = 64

V7X_LANES = 128
V7X_SUBLANES = 8
V7X_VMEM_LIMIT_BYTES = 56 * 1024 * 1024

SSM_CHUNK = V7X_SUBLANES
SSM_TILE_GROUPS = V7X_LANES // S5_GROUP_CH
SSM_KDIM = SSM_CHUNK * S5_GROUP_CH
SSM_LANES = 4 * S5_STATE
SSM_GROUPS_PER_STEP = 2
SSM_SEGMENTS = 8
SSM_REGROUP_ROWS = 256
SSM_PITCH_PAD = 8

MLP_ACC_CHUNK = 512
QKV_CHUNK = 256
PROJ_CHUNK = 512


def _cparams(n_axes):
    return pltpu.CompilerParams(dimension_semantics=("arbitrary",) * n_axes,
                                vmem_limit_bytes=V7X_VMEM_LIMIT_BYTES)


def _sigmoid(x):
    return 1.0 / (1.0 + jnp.exp(-x))


def _rms(x, g):
    return x * lax.rsqrt(jnp.mean(x * x, axis=-1, keepdims=True) + EPS) * g


def _mod_spec(d, slot, row_fn):
    return pl.BlockSpec((1, 1, d), lambda i, *_: (row_fn(i) * N_MOD + slot, 0, 0))


def _mod_kernel(c_ref, w_ref, b_ref, o_ref):
    c = c_ref[...]
    s = (c * _sigmoid(c)).astype(BF16)
    o_ref[0] = jnp.dot(s, w_ref[0].astype(BF16), preferred_element_type=F32) + b_ref[0]


def _modulation(cond8, ada_w, ada_b):
    depth, d, n = ada_w.shape
    tn = 1024
    return pl.pallas_call(
        _mod_kernel,
        grid=(depth, n // tn),
        in_specs=[pl.BlockSpec((8, d), lambda l, j: (0, 0)),
                  pl.BlockSpec((1, d, tn), lambda l, j: (l, 0, j)),
                  pl.BlockSpec((1, 1, tn), lambda l, j: (l, 0, j))],
        out_specs=pl.BlockSpec((1, 8, tn), lambda l, j: (l, 0, j)),
        out_shape=jax.ShapeDtypeStruct((depth, 8, n), F32),
        compiler_params=_cparams(2),
    )(cond8, ada_w, ada_b.reshape(depth, 1, n))


def _split_heads(y, fn):
    return jnp.concatenate([fn(y[:, h * HEAD_DIM:(h + 1) * HEAD_DIM]) for h in range(y.shape[1] // HEAD_DIM)], axis=1)


def _qkv_kernel(*refs, rope, caches):
    x_ref, g_ref, sh_ref, sc_ref, w_ref, qkg_ref = refs[:6]
    refs = refs[6:]
    if rope:
        cos_ref, sin_ref = refs[:2]
        refs = refs[2:]
    qkv_ref = refs[0]
    if caches:
        ka_ref, va_ref, kb_ref, vb_ref = refs[1:5]
    n = (_rms(x_ref[...], g_ref[...]) * (1.0 + sc_ref[0]) + sh_ref[0]).astype(BF16)

    def rot(y):
        if not rope:
            return y
        return y * cos_ref[...] + pltpu.roll(y, HEAD_DIM // 2, axis=1) * sin_ref[...]

    q_scale = HEAD_DIM ** -0.5 * math.log2(math.e)
    qa_end = A_HEADS * HEAD_DIM
    ka_end = qa_end + QKV_CHUNK
    va_end = ka_end + QKV_CHUNK
    qb_end = va_end + B_HEADS * 2 * HEAD_DIM
    kb_end = qb_end + B_HEADS * 2 * HEAD_DIM
    for c0 in range(0, w_ref.shape[1], QKV_CHUNK):
        cols = slice(c0, c0 + QKV_CHUNK)
        y = jnp.dot(n, w_ref[:, cols], preferred_element_type=F32)
        if c0 < qa_end:
            y = _split_heads(y, lambda t: rot(_rms(t, qkg_ref[0:1, :])) * q_scale)
        elif c0 < ka_end:
            y = _split_heads(y, lambda t: _rms(t, qkg_ref[1:2, :]))
            if caches:
                ka_ref[...] = y
            y = _split_heads(y, rot)
        elif c0 < va_end:
            if caches:
                va_ref[...] = y
        elif c0 < qb_end:
            y = _split_heads(y, lambda t: rot(t) * q_scale)
        elif c0 < kb_end:
            if caches:
                kb_ref[:, c0 - qb_end:c0 - qb_end + QKV_CHUNK] = y
            y = _split_heads(y, rot)
        elif caches:
            vb_ref[:, c0 - kb_end:c0 - kb_end + QKV_CHUNK] = y
        qkv_ref[:, cols] = y.astype(BF16)


def _qkv_proj(x, g, mod, row_fn, w, qk_g, rope_tabs, caches, tm):
    r, d = x.shape
    n = w.shape[1]
    assert A_KV_HEADS * HEAD_DIM == QKV_CHUNK and r % tm == 0 and n % QKV_CHUNK == 0
    assert not (caches and rope_tabs is not None)
    row = lambda i: (i, 0)
    in_specs = [pl.BlockSpec((tm, d), row),
                pl.BlockSpec((1, d), lambda i: (0, 0)),
                _mod_spec(d, 0, row_fn), _mod_spec(d, 1, row_fn),
                pl.BlockSpec((d, n), lambda i: (0, 0), pipeline_mode=pl.Buffered(1)),
                pl.BlockSpec((2, HEAD_DIM), lambda i: (0, 0))]
    args = [x, g, mod, mod, w, qk_g]
    if rope_tabs is not None:
        nblk = rope_tabs[0].shape[0] // tm
        in_specs += [pl.BlockSpec((tm, HEAD_DIM), lambda i: (i % nblk, 0))] * 2
        args += list(rope_tabs)
    widths = [n] + ([QKV_CHUNK, QKV_CHUNK, B_HEADS * 2 * HEAD_DIM, B_HEADS * B_V_DIM] if caches else [])
    out_specs = [pl.BlockSpec((tm, wd), row) for wd in widths]
    out_shape = [jax.ShapeDtypeStruct((r, wd), BF16 if k == 0 else F32) for k, wd in enumerate(widths)]
    return pl.pallas_call(
        functools.partial(_qkv_kernel, rope=rope_tabs is not None, caches=caches),
        grid=(r // tm,),
        in_specs=in_specs, out_specs=out_specs, out_shape=out_shape,
        compiler_params=_cparams(1),
    )(*args)


def _lane_tiles(x, op, acc):
    for j in range(0, x.shape[1], V7X_LANES):
        t = x[:, j:j + V7X_LANES]
        acc = t if acc is None else op(acc, t)
    return acc


def _softmax_pv(q, k_chunks, v_chunks, s_scr, rb):
    m_rows = q.shape[0]
    outs = []
    for r0 in range(0, m_rows, rb):
        qr = q[r0:r0 + rb]
        m_part = None
        off = 0
        for kc in k_chunks:
            k = kc()
            s = lax.dot_general(qr, k, (((1,), (1,)), ((), ())), preferred_element_type=F32)
            s_scr[r0:r0 + rb, off:off + k.shape[0]] = s
            m_part = _lane_tiles(s, jnp.maximum, m_part)
            off += k.shape[0]
        m = jnp.max(m_part, axis=-1, keepdims=True)
        l_part = acc = None
        off = 0
        for vc in v_chunks:
            v = vc()
            p = jnp.exp2(s_scr[r0:r0 + rb, off:off + v.shape[0]] - m)
            l_part = _lane_tiles(p, jnp.add, l_part)
            pv = jnp.dot(p.astype(BF16), v, preferred_element_type=F32)
            acc = pv if acc is None else acc + pv
            off += v.shape[0]
        outs.append(acc / jnp.sum(l_part, axis=-1, keepdims=True))
    return jnp.concatenate(outs, axis=0)


def _chunks(ref, c0, width, tk, cache_ref=None):
    length = ref.shape[0]
    out = [(lambda s=s: ref[s:s + tk, c0:c0 + width]) for s in range(0, length, tk)]
    if cache_ref is not None:
        out.append(lambda: cache_ref[:, c0:c0 + width].astype(BF16))
    return out


def _gqa_kernel(*refs, has_cache, tk, rb):
    if has_cache:
        q_ref, k_ref, v_ref, ck_ref, cv_ref, o_ref, s_scr = refs
    else:
        q_ref, k_ref, v_ref, o_ref, s_scr = refs
        ck_ref = cv_ref = None
    tq = q_ref.shape[0]
    wq = A_GROUP * HEAD_DIM
    for hh in range(k_ref.shape[1] // HEAD_DIM):
        q = jnp.concatenate([q_ref[:, hh * wq + g * HEAD_DIM:hh * wq + (g + 1) * HEAD_DIM] for g in range(A_GROUP)],
                            axis=0)
        o = _softmax_pv(q, _chunks(k_ref, hh * HEAD_DIM, HEAD_DIM, tk, ck_ref),
                        _chunks(v_ref, hh * HEAD_DIM, HEAD_DIM, tk, cv_ref), s_scr, rb)
        o_ref[:, hh * wq:(hh + 1) * wq] = jnp.concatenate([o[g * tq:(g + 1) * tq] for g in range(A_GROUP)],
                                                          axis=1).astype(BF16)


def _diff_kernel(*refs, has_cache, tk, rb, lam_init):
    if has_cache:
        q_ref, k_ref, v_ref, ck_ref, cv_ref, lam_ref, sg_ref, o_ref, s_scr = refs
    else:
        q_ref, k_ref, v_ref, lam_ref, sg_ref, o_ref, s_scr = refs
        ck_ref = cv_ref = None
    lp = lam_ref[...]
    lam = (jnp.exp(jnp.sum(lp[0:1] * lp[1:2], axis=-1, keepdims=True))
           - jnp.exp(jnp.sum(lp[2:3] * lp[3:4], axis=-1, keepdims=True)) + lam_init)
    for hh in range(v_ref.shape[1] // B_V_DIM):
        c0 = hh * B_V_DIM
        outs = []
        for m in range(2):
            cm = c0 + m * HEAD_DIM
            outs.append(_softmax_pv(q_ref[:, cm:cm + HEAD_DIM], _chunks(k_ref, cm, HEAD_DIM, tk, ck_ref),
                                    _chunks(v_ref, c0, B_V_DIM, tk, cv_ref), s_scr, rb))
        o = outs[0] - lam * outs[1]
        o_ref[:, c0:c0 + B_V_DIM] = (_rms(o, sg_ref[...]) * (1.0 - lam_init)).astype(BF16)


def _attention(qkv, nb, caches, lam_p, subln_g, lam_init, tq_a, tq_b, tk, hp):
    r = qkv.shape[0]
    length = r // nb
    has_cache = caches is not None
    lk = length + (caches[0].shape[0] // nb if has_cache else 0)
    tk = min(tk, length)
    wk = hp * HEAD_DIM
    ka0, va0 = A_HEADS * HEAD_DIM // wk, (A_HEADS + A_KV_HEADS) * HEAD_DIM // wk
    wb = hp * B_V_DIM
    qb0 = (A_HEADS + 2 * A_KV_HEADS) * HEAD_DIM // wb
    kb0, vb0 = qb0 + B_HEADS // hp, qb0 + 2 * B_HEADS // hp
    assert (A_HEADS * HEAD_DIM) % wk == 0 and ((A_HEADS + 2 * A_KV_HEADS) * HEAD_DIM) % wb == 0

    nq = length // tq_a
    wq = hp * A_GROUP * HEAD_DIM
    in_specs = [pl.BlockSpec((tq_a, wq), lambda b, h, i: (b * nq + i, h)),
                pl.BlockSpec((length, wk), lambda b, h, i: (b, ka0 + h)),
                pl.BlockSpec((length, wk), lambda b, h, i: (b, va0 + h))]
    args = [qkv, qkv, qkv]
    if has_cache:
        pc = caches[0].shape[0] // nb
        in_specs += [pl.BlockSpec((pc, wk), lambda b, h, i: (b, h))] * 2
        args += [caches[0], caches[1]]
    rb_a = min(256, A_GROUP * tq_a)
    oa = pl.pallas_call(
        functools.partial(_gqa_kernel, has_cache=has_cache, tk=tk, rb=rb_a),
        grid=(nb, A_KV_HEADS // hp, nq),
        in_specs=in_specs,
        out_specs=pl.BlockSpec((tq_a, wq), lambda b, h, i: (b * nq + i, h)),
        out_shape=jax.ShapeDtypeStruct((r, A_HEADS * HEAD_DIM), BF16),
        scratch_shapes=[pltpu.VMEM((A_GROUP * tq_a, lk), F32)],
        compiler_params=_cparams(3),
    )(*args)

    nq = length // tq_b
    in_specs = [pl.BlockSpec((tq_b, wb), lambda b, h, i: (b * nq + i, qb0 + h)),
                pl.BlockSpec((length, wb), lambda b, h, i: (b, kb0 + h)),
                pl.BlockSpec((length, wb), lambda b, h, i: (b, vb0 + h))]
    args = [qkv, qkv, qkv]
    if has_cache:
        in_specs += [pl.BlockSpec((pc, wb), lambda b, h, i: (b, h))] * 2
        args += [caches[2], caches[3]]
    in_specs += [pl.BlockSpec((4, HEAD_DIM), lambda b, h, i: (0, 0)),
                 pl.BlockSpec((1, B_V_DIM), lambda b, h, i: (0, 0))]
    args += [lam_p, subln_g]
    ob = pl.pallas_call(
        functools.partial(_diff_kernel, has_cache=has_cache, tk=tk, rb=min(256, tq_b), lam_init=lam_init),
        grid=(nb, B_HEADS // hp, nq),
        in_specs=in_specs,
        out_specs=pl.BlockSpec((tq_b, wb), lambda b, h, i: (b * nq + i, h)),
        out_shape=jax.ShapeDtypeStruct((r, B_HEADS * B_V_DIM), BF16),
        scratch_shapes=[pltpu.VMEM((tq_b, lk), F32)],
        compiler_params=_cparams(3),
    )(*args)
    return oa, ob


def _outproj_kernel(*refs, n_in):
    o_refs = refs[:n_in]
    w_refs = refs[n_in:2 * n_in]
    x_ref, g_ref, gate_ref, out_ref = refs[2 * n_in:]
    y = None
    for o_ref, w_ref in zip(o_refs, w_refs):
        t = jnp.dot(o_ref[...], w_ref[...], preferred_element_type=F32)
        y = t if y is None else y + t
    out_ref[...] = x_ref[...] + gate_ref[0] * _rms(y, g_ref[...])


def _outproj(os_, w, x, g, mod, row_fn, tm):
    r, d = x.shape
    n_in = len(os_)
    kw = os_[0].shape[1]
    assert all(o.shape[1] == kw for o in os_) and w.shape == (n_in * kw, d)
    in_specs = ([pl.BlockSpec((tm, kw), lambda i: (i, 0)) for _ in os_]
                + [pl.BlockSpec((kw, d), lambda i, k=k: (k, 0), pipeline_mode=pl.Buffered(1)) for k in range(n_in)]
                + [pl.BlockSpec((tm, d), lambda i: (i, 0)),
                   pl.BlockSpec((1, d), lambda i: (0, 0)),
                   _mod_spec(d, 2, row_fn)])
    return pl.pallas_call(
        functools.partial(_outproj_kernel, n_in=n_in),
        grid=(r // tm,),
        in_specs=in_specs,
        out_specs=pl.BlockSpec((tm, d), lambda i: (i, 0)),
        out_shape=jax.ShapeDtypeStruct((r, d), F32),
        compiler_params=_cparams(1),
    )(*os_, *([w] * n_in), x, g, mod)


def _mlp_kernel(x_ref, g2_ref, sh_ref, sc_ref, gate_ref, g3_ref, w1_ref, w2_ref, out_ref, n_scr):
    f = pl.program_id(1)

    @pl.when(f == 0)
    def _():
        n = _rms(x_ref[...], g2_ref[...]) * (1.0 + sc_ref[0]) + sh_ref[0]
        n_scr[...] = n.astype(BF16)
        out_ref[...] = jnp.zeros_like(out_ref)

    h = jnp.maximum(jnp.dot(n_scr[...], w1_ref[...], preferred_element_type=F32), 0.0)
    hb = (h * h).astype(BF16)
    d = out_ref.shape[1]
    for c0 in range(0, d, MLP_ACC_CHUNK):
        out_ref[:, c0:c0 + MLP_ACC_CHUNK] += jnp.dot(hb, w2_ref[:, c0:c0 + MLP_ACC_CHUNK],
                                                     preferred_element_type=F32)

    @pl.when(f == pl.num_programs(1) - 1)
    def _():
        out_ref[...] = x_ref[...] + gate_ref[0] * _rms(out_ref[...], g3_ref[...])


def _mlp(x, g2, g3, mod, row_fn, w1, w2, layer, tm, tf):
    r, d = x.shape
    dff = w1.shape[2]
    return pl.pallas_call(
        _mlp_kernel,
        grid=(r // tm, dff // tf),
        in_specs=[pl.BlockSpec((tm, d), lambda i, f: (i, 0)),
                  pl.BlockSpec((1, d), lambda i, f: (0, 0)),
                  _mod_spec(d, 3, row_fn), _mod_spec(d, 4, row_fn), _mod_spec(d, 5, row_fn),
                  pl.BlockSpec((1, d), lambda i, f: (0, 0)),
                  pl.BlockSpec((None, d, tf), lambda i, f: (layer, 0, f)),
                  pl.BlockSpec((None, tf, d), lambda i, f: (layer, f, 0))],
        out_specs=pl.BlockSpec((tm, d), lambda i, f: (i, 0)),
        out_shape=jax.ShapeDtypeStruct((r, d), F32),
        scratch_shapes=[pltpu.VMEM((tm, d), BF16)],
        compiler_params=_cparams(2),
    )(x, g2, mod, mod, mod, g3, w1, w2)


def _ssm_in_kernel(x_ref, g_ref, sh_ref, sc_ref, w_ref, u_ref):
    n = (_rms(x_ref[...], g_ref[...]) * (1.0 + sc_ref[0]) + sh_ref[0]).astype(BF16)
    for c0 in range(0, w_ref.shape[1], PROJ_CHUNK):
        u_ref[:, c0:c0 + PROJ_CHUNK] = jnp.dot(n, w_ref[:, c0:c0 + PROJ_CHUNK], preferred_element_type=F32)


def _ssm_in(x, g, mod, row_fn, w, tm):
    r, d = x.shape
    n = w.shape[1]
    return pl.pallas_call(
        _ssm_in_kernel,
        grid=(r // tm,),
        in_specs=[pl.BlockSpec((tm, d), lambda i: (i, 0)),
                  pl.BlockSpec((1, d), lambda i: (0, 0)),
                  _mod_spec(d, 0, row_fn), _mod_spec(d, 1, row_fn),
                  pl.BlockSpec((d, n), lambda i: (0, 0), pipeline_mode=pl.Buffered(1))],
        out_specs=pl.BlockSpec((tm, n), lambda i: (i, 0)),
        out_shape=jax.ShapeDtypeStruct((r, n), F32),
        compiler_params=_cparams(1),
    )(x, g, mod, mod, w)


def _ssm_params(a_re, a_im, log_dt, b_re, b_im, c_re, c_im, seg_chunks):
    g = a_re.shape[1]
    both = lambda x: jnp.transpose(x.astype(F32), (1, 0, 2)).reshape(g, 1, 2 * S5_STATE)
    dt = jnp.exp(log_dt.astype(F32))[..., None]
    b_t = lambda x: jnp.transpose(x.astype(F32), (1, 3, 0, 2)).reshape(g, S5_GROUP_CH, 2 * S5_STATE)
    c_t = lambda x: jnp.transpose(x.astype(F32), (1, 2, 0, 3)).reshape(g, S5_GROUP_CH, 2 * S5_STATE)
    args = (both(a_re * dt), both(a_im * dt), both(a_re), both(a_im), b_t(b_re), b_t(b_im), c_t(c_re), c_t(c_im))
    blk = lambda a: pl.BlockSpec((1,) + a.shape[1:], lambda i: (i, 0, 0))
    out_shape = [jax.ShapeDtypeStruct((g, SSM_KDIM, SSM_LANES), BF16),
                 jax.ShapeDtypeStruct((g, SSM_KDIM, SSM_KDIM), BF16),
                 jax.ShapeDtypeStruct((g, SSM_KDIM, SSM_LANES), BF16),
                 jax.ShapeDtypeStruct((g, 1, SSM_LANES), F32),
                 jax.ShapeDtypeStruct((g, 1, SSM_LANES), F32)]
    return pl.pallas_call(
        functools.partial(_ssm_params_kernel, seg_chunks=seg_chunks),
        grid=(g,),
        in_specs=[blk(a) for a in args],
        out_specs=[blk(o) for o in out_shape],
        out_shape=out_shape,
        compiler_params=_cparams(1),
    )(*args)


def _ssm_params_kernel(lr_ref, li_ref, ar_ref, ai_ref, btr_ref, bti_ref, ccr_ref, cci_ref,
                       win_ref, toep_ref, cout_ref, achunk_ref, aseg_ref, *, seg_chunks):
    t, nch, half = SSM_CHUNK, S5_GROUP_CH, 2 * S5_STATE
    hi = lax.Precision.HIGHEST
    lr, li = lr_ref[0], li_ref[0]

    def power(k):
        mag = jnp.exp(lr * k)
        return mag * jnp.cos(li * k), mag * jnp.sin(li * k)

    pw = [(jnp.ones_like(lr), jnp.zeros_like(lr)), power(1.0)]
    for _ in range(t - 1):
        pw.append((pw[-1][0] * pw[1][0] - pw[-1][1] * pw[1][1], pw[-1][0] * pw[1][1] + pw[-1][1] * pw[1][0]))
    fwd1 = lax.broadcasted_iota(jnp.int32, (1, half), 1) < S5_STATE
    pick = lambda kf, kb: (jnp.where(fwd1, pw[kf][0], pw[kb][0]), jnp.where(fwd1, pw[kf][1], pw[kb][1]))
    ar, ai = ar_ref[0], ai_ref[0]
    xr, xi = pw[1][0] - 1.0, pw[1][1]
    den = ar * ar + ai * ai
    qr, qi = (xr * ar + xi * ai) / den, (xi * ar - xr * ai) / den
    bbr = qr * btr_ref[0] - qi * bti_ref[0]
    bbi = qr * bti_ref[0] + qi * btr_ref[0]
    ccr, cci = ccr_ref[0], cci_ref[0]

    for s in range(t):
        rows = slice(s * nch, (s + 1) * nch)
        er, ei = pick(t - 1 - s, s)
        win_ref[0, rows, 0:half] = (er * bbr - ei * bbi).astype(BF16)
        win_ref[0, rows, half:] = (er * bbi + ei * bbr).astype(BF16)
        er, ei = pick(s + 1, t - s)
        cout_ref[0, rows, 0:half] = (ccr * er - cci * ei).astype(BF16)
        cout_ref[0, rows, half:] = (-(ccr * ei + cci * er)).astype(BF16)

    fwd = lax.broadcasted_iota(jnp.int32, (nch, half), 1) < S5_STATE
    cp = [(ccr * pw[k][0] - cci * pw[k][1], ccr * pw[k][1] + cci * pw[k][0]) for k in range(t)]
    stack = lambda ks, j: jnp.concatenate([cp[k][j] for k in ks], axis=0)
    nt = lambda x, y: lax.dot_general(x, y, (((1,), (1,)), ((), ())), precision=hi, preferred_element_type=F32)
    zero = jnp.zeros_like(bbr)
    order, rev = list(range(t)), list(range(t - 1, -1, -1))
    taps_f = (nt(jnp.where(fwd, bbr, zero), stack(order, 0)) - nt(jnp.where(fwd, bbi, zero), stack(order, 1)))
    taps_b = (nt(jnp.where(fwd, zero, bbr), stack(rev, 0)) - nt(jnp.where(fwd, zero, bbi), stack(rev, 1)))
    lane = lax.broadcasted_iota(jnp.int32, (nch, SSM_KDIM), 1)
    for s in range(t):
        tf = taps_f if s == 0 else pltpu.roll(taps_f, s * nch, axis=1)
        tb = taps_b if s == t - 1 else pltpu.roll(taps_b, (s + 1) * nch, axis=1)
        blk = jnp.where(lane >= s * nch, tf, 0.0) + jnp.where(lane < (s + 1) * nch, tb, 0.0)
        toep_ref[0, s * nch:(s + 1) * nch, :] = blk.astype(BF16)

    for ref, (dr, di) in ((achunk_ref, pw[t]), (aseg_ref, power(float(t * seg_chunks)))):
        ref[0, :, 0:half] = dr
        ref[0, :, half:] = di


def _cmul_add(ar, ai, h_re, h_im, s_re, s_im):
    return ar * h_re - ai * h_im + s_re, ar * h_im + ai * h_re + s_im


def _regroup_perms():
    n = SSM_CHUNK * V7X_LANES
    idx = jnp.arange(n)
    tok, grp, ch = idx // V7X_LANES, idx % V7X_LANES // S5_GROUP_CH, idx % S5_GROUP_CH
    dst = grp * SSM_KDIM + tok * S5_GROUP_CH + ch
    p_in = (dst[:, None] == idx[None, :]).astype(BF16)
    return p_in, p_in.T


def _regroup_in(u_ref, pin_ref, lhs_scr, nrows):
    t, rbk = SSM_CHUNK, min(SSM_REGROUP_ROWS, nrows)

    def body(rb, carry):
        pieces = [u_ref[pl.ds(rb * (rbk * t) + tt, rbk, stride=t), :].astype(BF16) for tt in range(t)]
        out = jnp.dot(jnp.concatenate(pieces, axis=1), pin_ref[...], preferred_element_type=F32)
        rows = pl.ds(pl.multiple_of(rb * rbk, rbk), rbk)
        for gl in range(SSM_TILE_GROUPS):
            lhs_scr[gl, rows, :] = out[:, gl * SSM_KDIM:(gl + 1) * SSM_KDIM].astype(BF16)
        return carry

    lax.fori_loop(0, nrows // rbk, body, 0)


def _regroup_out(y_scr, pout_ref, y_ref, nrows):
    t, rbk = SSM_CHUNK, min(SSM_REGROUP_ROWS, nrows)

    def body(rb, carry):
        rows = pl.ds(pl.multiple_of(rb * rbk, rbk), rbk)
        rest = jnp.concatenate([y_scr[gl, rows, :] for gl in range(SSM_TILE_GROUPS)], axis=1)
        out = None
        for _ in range(3):
            term = rest.astype(BF16)
            rest = rest - term.astype(F32)
            part = jnp.dot(term, pout_ref[...], preferred_element_type=F32)
            out = part if out is None else out + part
        for tt in range(t):
            y_ref[pl.ds(rb * (rbk * t) + tt, rbk, stride=t), :] = out[:, tt * V7X_LANES:(tt + 1) * V7X_LANES]
        return carry

    lax.fori_loop(0, nrows // rbk, body, 0)


def _ssm_core_kernel(u_ref, pin_ref, pout_ref, win_ref, toep_ref, cout_ref, a_ref, aseg_ref, h0_ref, y_ref, hfin_ref,
                     lhs_scr, y_scr, s_re, s_im, tf_re, tb_re, tf_im, tb_im, *, nseq, nc, nsub):
    step = pl.program_id(1)
    nrows = nseq * nc

    @pl.when(step == 0)
    def _():
        _regroup_in(u_ref, pin_ref, lhs_scr, nrows)

    for k in range(SSM_GROUPS_PER_STEP):
        _ssm_group(k, step * SSM_GROUPS_PER_STEP + k, win_ref, toep_ref, cout_ref, a_ref, aseg_ref, h0_ref, hfin_ref,
                   lhs_scr, y_scr, s_re, s_im, tf_re, tb_re, tf_im, tb_im, nseq, nc, nsub)

    @pl.when(step == pl.num_programs(1) - 1)
    def _():
        _regroup_out(y_scr, pout_ref, y_ref, nrows)


def _ssm_group(k, g, win_ref, toep_ref, cout_ref, a_ref, aseg_ref, h0_ref, hfin_ref,
               lhs_scr, y_scr, s_re, s_im, tf_re, tb_re, tf_im, tb_im, nseq, nc, nsub):
    half = SSM_LANES // 2
    pitch = nc + SSM_PITCH_PAD
    nrows = nseq * nc
    nslab = nseq // V7X_SUBLANES
    base = k * nseq * pitch

    lhs = lhs_scr[g]
    s = jnp.dot(lhs, win_ref[k], preferred_element_type=F32)
    for p in range(nseq):
        s_re[base + p * pitch:base + p * pitch + nc, :] = s[p * nc:(p + 1) * nc, 0:half]
        s_im[base + p * pitch:base + p * pitch + nc, :] = s[p * nc:(p + 1) * nc, half:]

    ar, ai = a_ref[k][:, 0:half], a_ref[k][:, half:]
    fwd = lax.broadcasted_iota(jnp.int32, (V7X_SUBLANES, half), 1) < S5_STATE

    def slab(q, c):
        return pl.ds(base + q * V7X_SUBLANES * pitch + c, V7X_SUBLANES, stride=pitch)

    def scan(init, store):
        hs = list(init)
        for c in range(nc):
            cb = nc - 1 - c
            for q in range(nslab):
                h_re, h_im = hs[q]
                x_re = jnp.where(fwd, s_re[slab(q, c), :], s_re[slab(q, cb), :])
                x_im = jnp.where(fwd, s_im[slab(q, c), :], s_im[slab(q, cb), :])
                if store:
                    tf_re[slab(q, c), :] = h_re
                    tb_re[slab(q, cb), :] = h_re
                    tf_im[slab(q, c), :] = h_im
                    tb_im[slab(q, cb), :] = h_im
                hs[q] = _cmul_add(ar, ai, h_re, h_im, x_re, x_im)
        return hs

    rows8 = lambda q: slice(q * V7X_SUBLANES, (q + 1) * V7X_SUBLANES)
    init = [(h0_ref[k, rows8(q), 0:half], h0_ref[k, rows8(q), half:]) for q in range(nslab)]
    if nsub > 1:
        zero = jnp.zeros((V7X_SUBLANES, half), F32)
        ends = scan([(zero, zero)] * nslab, False)
        gr, gi = aseg_ref[k][:, 0:half], aseg_ref[k][:, half:]
        seg = lax.broadcasted_iota(jnp.int32, (V7X_SUBLANES, half), 0) % nsub
        carried = jnp.where(fwd, seg, nsub - 1 - seg) != 0
        shift = lambda x: jnp.where(fwd, pltpu.roll(x, 1, axis=0), pltpu.roll(x, V7X_SUBLANES - 1, axis=0))
        for q in range(nslab):
            (h0_re, h0_im), (f_re, f_im) = init[q], ends[q]
            e_re, e_im = h0_re, h0_im
            for _ in range(nsub - 1):
                x_re, x_im = _cmul_add(gr, gi, e_re, e_im, f_re, f_im)
                e_re = h0_re + jnp.where(carried, shift(x_re), 0.0)
                e_im = h0_im + jnp.where(carried, shift(x_im), 0.0)
            init[q] = (e_re, e_im)
    hs = scan(init, True)
    for q in range(nslab):
        hfin_ref[k, rows8(q), 0:half] = hs[q][0]
        hfin_ref[k, rows8(q), half:] = hs[q][1]

    gather = lambda ref: jnp.concatenate([ref[base + p * pitch:base + p * pitch + nc, :] for p in range(nseq)],
                                         axis=0)
    fwd_rows = lax.broadcasted_iota(jnp.int32, (nrows, half), 1) < S5_STATE
    hp = jnp.concatenate([jnp.where(fwd_rows, gather(tf_re), gather(tb_re)),
                          jnp.where(fwd_rows, gather(tf_im), gather(tb_im))], axis=1).astype(BF16)
    y_scr[g] = (jnp.dot(lhs, toep_ref[k], preferred_element_type=F32)
                + lax.dot_general(hp, cout_ref[k], (((1,), (1,)), ((), ())), preferred_element_type=F32))


def _ssm_core(u, ops, h0, nseq, nc, nsub):
    r, d = u.shape
    nrows = nseq * nc
    assert r == nrows * SSM_CHUNK and nseq % V7X_SUBLANES == 0 and V7X_SUBLANES % nsub == 0
    assert nrows % min(SSM_REGROUP_ROWS, nrows) == 0 and nc % V7X_SUBLANES == 0
    ntile = d // V7X_LANES
    gps = SSM_GROUPS_PER_STEP
    steps = SSM_TILE_GROUPS // gps
    grp = lambda j, g: (j * steps + g, 0, 0)
    scan_rows = gps * nseq * (nc + SSM_PITCH_PAD)
    p_in, p_out = _regroup_perms()
    perm_spec = pl.BlockSpec(p_in.shape, lambda j, g: (0, 0), pipeline_mode=pl.Buffered(1))
    return pl.pallas_call(
        functools.partial(_ssm_core_kernel, nseq=nseq, nc=nc, nsub=nsub),
        grid=(ntile, steps),
        in_specs=[pl.BlockSpec((r, V7X_LANES), lambda j, g: (0, j)),
                  perm_spec, perm_spec]
                 + [pl.BlockSpec((gps,) + o.shape[1:], grp) for o in ops]
                 + [pl.BlockSpec((gps, nseq, SSM_LANES), grp)],
        out_specs=[pl.BlockSpec((r, V7X_LANES), lambda j, g: (0, j)),
                   pl.BlockSpec((gps, nseq, SSM_LANES), grp)],
        out_shape=[jax.ShapeDtypeStruct((r, d), F32),
                   jax.ShapeDtypeStruct((d // S5_GROUP_CH, nseq, SSM_LANES), F32)],
        scratch_shapes=[pltpu.VMEM((SSM_TILE_GROUPS, nrows, SSM_KDIM), BF16),
                        pltpu.VMEM((SSM_TILE_GROUPS, nrows, SSM_KDIM), F32)]
                       + [pltpu.VMEM((scan_rows, V7X_LANES), F32)] * 6,
        compiler_params=_cparams(2),
    )(u, p_in, p_out, *ops, h0)


def _glu_kernel(y_ref, u_ref, d_ref, w_ref, z_ref):
    v = y_ref[...] + d_ref[...] * u_ref[...]
    z = 0.5 * v * (1.0 + jnp.tanh(math.sqrt(2.0 / math.pi) * (v + 0.044715 * (v * v * v))))
    zb = z.astype(BF16)
    for c0 in range(0, w_ref.shape[1], PROJ_CHUNK):
        cols = slice(c0, c0 + PROJ_CHUNK)
        gl = jnp.dot(zb, w_ref[:, cols], preferred_element_type=F32)
        z_ref[:, cols] = (z[:, cols] * _sigmoid(gl)).astype(BF16)


def _glu(y, u, d, w, tm):
    r, dm = y.shape
    return pl.pallas_call(
        _glu_kernel,
        grid=(r // tm,),
        in_specs=[pl.BlockSpec((tm, dm), lambda i: (i, 0)),
                  pl.BlockSpec((tm, dm), lambda i: (i, 0)),
                  pl.BlockSpec((1, dm), lambda i: (0, 0)),
                  pl.BlockSpec(w.shape, lambda i: (0, 0), pipeline_mode=pl.Buffered(1))],
        out_specs=pl.BlockSpec((tm, dm), lambda i: (i, 0)),
        out_shape=jax.ShapeDtypeStruct((r, dm), BF16),
        compiler_params=_cparams(1),
    )(y, u, d, w)


def _rope_tables(length):
    rows = length // GRID_W
    row = jnp.repeat(jnp.arange(rows, dtype=F32), GRID_W)
    col = jnp.tile(jnp.arange(GRID_W, dtype=F32), rows)
    inv = ROPE_BASE ** (-jnp.arange(ROPE_PAIRS_PER_AXIS, dtype=F32) / ROPE_PAIRS_PER_AXIS)
    ang = jnp.concatenate([row[:, None] * inv, col[:, None] * inv], axis=-1)
    cos, sin = jnp.cos(ang), jnp.sin(ang)
    return jnp.concatenate([cos, cos], axis=-1), jnp.concatenate([-sin, sin], axis=-1)


def kernel(x_prompt, x_sample, c, cache_a_k, cache_a_v, cache_b_k, cache_b_v, state_ssm, c_ctx, ada_w, ada_b, norm_g, mlp_w1, mlp_w2, attn_w_in, attn_w_out, attn_qk_norm, diff_lambda, diff_subln, ssm_w_in, ssm_a_re, ssm_a_im, ssm_log_dt, ssm_b_re, ssm_b_im, ssm_c_re, ssm_c_im, ssm_d, ssm_glu_w, ssm_w_out):
    bp, lp, d = x_prompt.shape
    bs, ls, _ = x_sample.shape
    depth = ada_w.shape[0]
    past = cache_a_k.shape[2]
    nsub = SSM_SEGMENTS
    assert bs + 1 <= 8 and bp % V7X_SUBLANES == 0
    assert lp % SSM_CHUNK == 0 and ls % (SSM_CHUNK * nsub) == 0 and ls % GRID_W == 0

    cond8 = jnp.zeros((8, d), F32).at[0].set(c_ctx).at[1:1 + bs].set(c)
    mods = _modulation(cond8, ada_w, ada_b).reshape(depth, 8 * N_MOD, 1, d)

    xp = x_prompt.reshape(bp * lp, d)
    xs = x_sample.reshape(bs * ls, d)
    tm = min(512, bp * lp, ls)
    tm_big = min(1024, bp * lp, ls)
    row_p = lambda i: 0
    row_s = lambda i: 1 + i // (ls // tm)
    row_s_big = lambda i: 1 + i // (ls // tm_big)
    w1_all = mlp_w1.astype(BF16)
    w2_all = mlp_w2.astype(BF16)
    new_ak = new_av = new_bk = new_bv = new_ssm = None
    for l in range(depth):
        mod = mods[l]
        g = norm_g[l][:, None, :]
        i = l // 2
        if l % 2 == 0:
            lam_init = 0.8 - 0.6 * math.exp(-0.3 * l)
            w_in = attn_w_in[i].astype(BF16)
            w_out = attn_w_out[i].astype(BF16)
            qkv, ka, va, kb, vb = _qkv_proj(xp, g[0], mod, row_p, w_in, attn_qk_norm[i], None, True, tm)
            new_ak = ka.reshape(bp, 1, lp, A_KV_HEADS, HEAD_DIM)
            new_av = va.reshape(bp, 1, lp, A_KV_HEADS, HEAD_DIM)
            new_bk = kb.reshape(bp, 1, lp, B_HEADS, 2, HEAD_DIM)
            new_bv = vb.reshape(bp, 1, lp, B_HEADS, B_V_DIM)
            oa, ob = _attention(qkv, bp, None, diff_lambda[i], diff_subln[i][None, :], lam_init,
                                tq_a=lp, tq_b=lp, tk=512, hp=2)
            xp = _outproj([oa, ob], w_out, xp, g[1], mod, row_p, tm)
            (qkv,) = _qkv_proj(xs, g[0], mod, row_s, w_in, attn_qk_norm[i], _rope_tables(ls), False, tm)
            caches = (cache_a_k[:, i].reshape(bs * past, A_KV_HEADS * HEAD_DIM),
                      cache_a_v[:, i].reshape(bs * past, A_KV_HEADS * HEAD_DIM),
                      cache_b_k[:, i].reshape(bs * past, B_HEADS * 2 * HEAD_DIM),
                      cache_b_v[:, i].reshape(bs * past, B_HEADS * B_V_DIM))
            oa, ob = _attention(qkv, bs, caches, diff_lambda[i], diff_subln[i][None, :], lam_init,
                                tq_a=min(256, ls), tq_b=min(512, ls), tk=512, hp=1)
            xs = _outproj([oa, ob], w_out, xs, g[1], mod, row_s, tm)
        else:
            w_in = ssm_w_in[i].astype(BF16)
            u_p = _ssm_in(xp, g[0], mod, row_p, w_in, tm_big)
            u_s = _ssm_in(xs, g[0], mod, row_s_big, w_in, tm_big)
            ng = d // S5_GROUP_CH
            nc_p, nc_s = lp // SSM_CHUNK, ls // SSM_CHUNK // nsub
            ops = _ssm_params(ssm_a_re[i], ssm_a_im[i], ssm_log_dt[i], ssm_b_re[i], ssm_b_im[i],
                                                 ssm_c_re[i], ssm_c_im[i], nc_s)
            st = state_ssm[:, i].astype(F32)
            h0 = jnp.zeros((ng, bs, nsub, 4, S5_STATE), F32)
            h0 = h0.at[:, :, 0, 0].set(jnp.transpose(st[:, 0, :, :, 0], (1, 0, 2)))
            h0 = h0.at[:, :, 0, 2].set(jnp.transpose(st[:, 0, :, :, 1], (1, 0, 2)))
            h0 = h0.at[:, :, nsub - 1, 1].set(jnp.transpose(st[:, 1, :, :, 0], (1, 0, 2)))
            h0 = h0.at[:, :, nsub - 1, 3].set(jnp.transpose(st[:, 1, :, :, 1], (1, 0, 2)))
            h0 = h0.reshape(ng, bs * nsub, SSM_LANES)
            y_p, hfin = _ssm_core(u_p, ops, jnp.zeros((ng, bp, SSM_LANES), F32), bp, nc_p, 1)
            y_s, _ = _ssm_core(u_s, ops, h0, bs * nsub, nc_s, nsub)
            new_ssm = jnp.transpose(hfin.reshape(ng, bp, 2, 2, S5_STATE), (1, 3, 0, 4, 2))
            glu_w = ssm_glu_w[i].astype(BF16)
            w_out = ssm_w_out[i].astype(BF16)
            dvec = ssm_d[i][None, :]
            z_p = _glu(y_p, u_p, dvec, glu_w, tm)
            z_s = _glu(y_s, u_s, dvec, glu_w, tm)
            xp = _outproj([z_p], w_out, xp, g[1], mod, row_p, tm)
            xs = _outproj([z_s], w_out, xs, g[1], mod, row_s, tm)
        xp = _mlp(xp, g[2], g[3], mod, row_p, w1_all, w2_all, l, tm_big, 512)
        xs = _mlp(xs, g[2], g[3], mod, row_s_big, w1_all, w2_all, l, tm_big, 512)
    return (xp.reshape(bp, lp, d), xs.reshape(bs, ls, d), new_ak, new_av, new_bk, new_bv,
            new_ssm[:, None])
```

```python
import functools
import math

import jax
import jax.numpy as jnp
from jax import lax
from jax.experimental import pallas as pl
from jax.experimental.pallas import tpu as pltpu

F32 = jnp.float32
BF16 = jnp.bfloat16
EPS = 1e-6

HEAD_DIM = 128
A_HEADS = 8
A_KV_HEADS = 2
A_GROUP = A_HEADS // A_KV_HEADS
B_HEADS = 4
B_V_DIM = 2 * HEAD_DIM
GRID_W = 64
ROPE_BASE = 10000.0
ROPE_PAIRS_PER_AXIS = HEAD_DIM // 4
N_MOD = 6
S5_GROUP_CH = 16
S5_STATE = 64

V7X_LANES = 128
V7X_SUBLANES = 8
V7X_VMEM_LIMIT_BYTES = 56 * 1024 * 1024

SSM_CHUNK = V7X_SUBLANES
SSM_TILE_GROUPS = V7X_LANES // S5_GROUP_CH
SSM_KDIM = SSM_CHUNK * S5_GROUP_CH
SSM_LANES = 4 * S5_STATE
SSM_GROUPS_PER_STEP = 2
SSM_SEGMENTS = 8
SSM_REGROUP_ROWS = 256
SSM_Y_TERMS = 2
SSM_PITCH_PAD = 8

MLP_ACC_CHUNK = 512
QKV_CHUNK = 256
PROJ_CHUNK = 512


def _cparams(n_axes):
    return pltpu.CompilerParams(dimension_semantics=("arbitrary",) * n_axes,
                                vmem_limit_bytes=V7X_VMEM_LIMIT_BYTES)


def _sigmoid(x):
    return 1.0 / (1.0 + jnp.exp(-x))


def _rms(x, g):
    return x * lax.rsqrt(jnp.mean(x * x, axis=-1, keepdims=True) + EPS) * g


def _mod_spec(d, slot, row_fn):
    return pl.BlockSpec((1, 1, d), lambda i, *_: (row_fn(i) * N_MOD + slot, 0, 0))


def _mod_kernel(c_ref, w_ref, b_ref, o_ref):
    c = c_ref[...]
    s = (c * _sigmoid(c)).astype(BF16)
    o_ref[0] = jnp.dot(s, w_ref[0].astype(BF16), preferred_element_type=F32) + b_ref[0]


def _modulation(cond8, ada_w, ada_b):
    depth, d, n = ada_w.shape
    tn = 1024
    return pl.pallas_call(
        _mod_kernel,
        grid=(depth, n // tn),
        in_specs=[pl.BlockSpec((8, d), lambda l, j: (0, 0)),
                  pl.BlockSpec((1, d, tn), lambda l, j: (l, 0, j)),
                  pl.BlockSpec((1, 1, tn), lambda l, j: (l, 0, j))],
        out_specs=pl.BlockSpec((1, 8, tn), lambda l, j: (l, 0, j)),
        out_shape=jax.ShapeDtypeStruct((depth, 8, n), F32),
        compiler_params=_cparams(2),
    )(cond8, ada_w, ada_b.reshape(depth, 1, n))


def _split_heads(y, fn):
    return jnp.concatenate([fn(y[:, h * HEAD_DIM:(h + 1) * HEAD_DIM]) for h in range(y.shape[1] // HEAD_DIM)], axis=1)


def _qkv_kernel(*refs, rope, caches):
    x_ref, g_ref, sh_ref, sc_ref, w_ref, qkg_ref = refs[:6]
    refs = refs[6:]
    if rope:
        cos_ref, sin_ref = refs[:2]
        refs = refs[2:]
    qkv_ref = refs[0]
    if caches:
        ka_ref, va_ref, kb_ref, vb_ref = refs[1:5]
    n = (_rms(x_ref[...], g_ref[...]) * (1.0 + sc_ref[0]) + sh_ref[0]).astype(BF16)

    def rot(y):
        if not rope:
            return y
        return y * cos_ref[...] + pltpu.roll(y, HEAD_DIM // 2, axis=1) * sin_ref[...]

    q_scale = HEAD_DIM ** -0.5 * math.log2(math.e)
    qa_end = A_HEADS * HEAD_DIM
    ka_end = qa_end + QKV_CHUNK
    va_end = ka_end + QKV_CHUNK
    qb_end = va_end + B_HEADS * 2 * HEAD_DIM
    kb_end = qb_end + B_HEADS * 2 * HEAD_DIM
    for c0 in range(0, w_ref.shape[1], QKV_CHUNK):
        cols = slice(c0, c0 + QKV_CHUNK)
        y = jnp.dot(n, w_ref[:, cols], preferred_element_type=F32)
        if c0 < qa_end:
            y = _split_heads(y, lambda t: rot(_rms(t, qkg_ref[0:1, :])) * q_scale)
        elif c0 < ka_end:
            y = _split_heads(y, lambda t: _rms(t, qkg_ref[1:2, :]))
            if caches:
                ka_ref[...] = y
            y = _split_heads(y, rot)
        elif c0 < va_end:
            if caches:
                va_ref[...] = y
        elif c0 < qb_end:
            y = _split_heads(y, lambda t: rot(t) * q_scale)
        elif c0 < kb_end:
            if caches:
                kb_ref[:, c0 - qb_end:c0 - qb_end + QKV_CHUNK] = y
            y = _split_heads(y, rot)
        elif caches:
            vb_ref[:, c0 - kb_end:c0 - kb_end + QKV_CHUNK] = y
        qkv_ref[:, cols] = y.astype(BF16)


def _qkv_proj(x, g, mod, row_fn, w, qk_g, rope_tabs, caches, tm):
    r, d = x.shape
    n = w.shape[1]
    assert A_KV_HEADS * HEAD_DIM == QKV_CHUNK and r % tm == 0 and n % QKV_CHUNK == 0
    assert not (caches and rope_tabs is not None)
    row = lambda i: (i, 0)
    in_specs = [pl.BlockSpec((tm, d), row),
                pl.BlockSpec((1, d), lambda i: (0, 0)),
                _mod_spec(d, 0, row_fn), _mod_spec(d, 1, row_fn),
                pl.BlockSpec((d, n), lambda i: (0, 0), pipeline_mode=pl.Buffered(1)),
                pl.BlockSpec((2, HEAD_DIM), lambda i: (0, 0))]
    args = [x, g, mod, mod, w, qk_g]
    if rope_tabs is not None:
        nblk = rope_tabs[0].shape[0] // tm
        in_specs += [pl.BlockSpec((tm, HEAD_DIM), lambda i: (i % nblk, 0))] * 2
        args += list(rope_tabs)
    widths = [n] + ([QKV_CHUNK, QKV_CHUNK, B_HEADS * 2 * HEAD_DIM, B_HEADS * B_V_DIM] if caches else [])
    out_specs = [pl.BlockSpec((tm, wd), row) for wd in widths]
    out_shape = [jax.ShapeDtypeStruct((r, wd), BF16 if k == 0 else F32) for k, wd in enumerate(widths)]
    return pl.pallas_call(
        functools.partial(_qkv_kernel, rope=rope_tabs is not None, caches=caches),
        grid=(r // tm,),
        in_specs=in_specs, out_specs=out_specs, out_shape=out_shape,
        compiler_params=_cparams(1),
    )(*args)


def _lane_tiles(x, op, acc):
    for j in range(0, x.shape[1], V7X_LANES):
        t = x[:, j:j + V7X_LANES]
        acc = t if acc is None else op(acc, t)
    return acc


def _softmax_pv(q, k_chunks, v_chunks, s_scr, rb):
    m_rows = q.shape[0]
    outs = []
    for r0 in range(0, m_rows, rb):
        qr = q[r0:r0 + rb]
        m_part = None
        off = 0
        for kc in k_chunks:
            k = kc()
            s = lax.dot_general(qr, k, (((1,), (1,)), ((), ())), preferred_element_type=F32)
            s_scr[r0:r0 + rb, off:off + k.shape[0]] = s
            m_part = _lane_tiles(s, jnp.maximum, m_part)
            off += k.shape[0]
        m = jnp.max(m_part, axis=-1, keepdims=True)
        l_part = acc = None
        off = 0
        for vc in v_chunks:
            v = vc()
            p = jnp.exp2(s_scr[r0:r0 + rb, off:off + v.shape[0]] - m)
            l_part = _lane_tiles(p, jnp.add, l_part)
            pv = jnp.dot(p.astype(BF16), v, preferred_element_type=F32)
            acc = pv if acc is None else acc + pv
            off += v.shape[0]
        outs.append(acc / jnp.sum(l_part, axis=-1, keepdims=True))
    return jnp.concatenate(outs, axis=0)


def _chunks(ref, c0, width, tk, cache_ref=None):
    length = ref.shape[0]
    out = [(lambda s=s: ref[s:s + tk, c0:c0 + width]) for s in range(0, length, tk)]
    if cache_ref is not None:
        out.append(lambda: cache_ref[:, c0:c0 + width].astype(BF16))
    return out


def _gqa_kernel(*refs, has_cache, tk, rb):
    if has_cache:
        q_ref, k_ref, v_ref, ck_ref, cv_ref, o_ref, s_scr = refs
    else:
        q_ref, k_ref, v_ref, o_ref, s_scr = refs
        ck_ref = cv_ref = None
    tq = q_ref.shape[0]
    wq = A_GROUP * HEAD_DIM
    for hh in range(k_ref.shape[1] // HEAD_DIM):
        q = jnp.concatenate([q_ref[:, hh * wq + g * HEAD_DIM:hh * wq + (g + 1) * HEAD_DIM] for g in range(A_GROUP)],
                            axis=0)
        o = _softmax_pv(q, _chunks(k_ref, hh * HEAD_DIM, HEAD_DIM, tk, ck_ref),
                        _chunks(v_ref, hh * HEAD_DIM, HEAD_DIM, tk, cv_ref), s_scr, rb)
        o_ref[:, hh * wq:(hh + 1) * wq] = jnp.concatenate([o[g * tq:(g + 1) * tq] for g in range(A_GROUP)],
                                                          axis=1).astype(BF16)


def _diff_kernel(*refs, has_cache, tk, rb, lam_init):
    if has_cache:
        q_ref, k_ref, v_ref, ck_ref, cv_ref, lam_ref, sg_ref, o_ref, s_scr = refs
    else:
        q_ref, k_ref, v_ref, lam_ref, sg_ref, o_ref, s_scr = refs
        ck_ref = cv_ref = None
    lp = lam_ref[...]
    lam = (jnp.exp(jnp.sum(lp[0:1] * lp[1:2], axis=-1, keepdims=True))
           - jnp.exp(jnp.sum(lp[2:3] * lp[3:4], axis=-1, keepdims=True)) + lam_init)
    for hh in range(v_ref.shape[1] // B_V_DIM):
        c0 = hh * B_V_DIM
        outs = []
        for m in range(2):
            cm = c0 + m * HEAD_DIM
            outs.append(_softmax_pv(q_ref[:, cm:cm + HEAD_DIM], _chunks(k_ref, cm, HEAD_DIM, tk, ck_ref),
                                    _chunks(v_ref, c0, B_V_DIM, tk, cv_ref), s_scr, rb))
        o = outs[0] - lam * outs[1]
        o_ref[:, c0:c0 + B_V_DIM] = (_rms(o, sg_ref[...]) * (1.0 - lam_init)).astype(BF16)


def _attention(qkv, nb, caches, lam_p, subln_g, lam_init, tq_a, tq_b, tk, hp):
    r = qkv.shape[0]
    length = r // nb
    has_cache = caches is not None
    lk = length + (caches[0].shape[0] // nb if has_cache else 0)
    tk = min(tk, length)
    wk = hp * HEAD_DIM
    ka0, va0 = A_HEADS * HEAD_DIM // wk, (A_HEADS + A_KV_HEADS) * HEAD_DIM // wk
    wb = hp * B_V_DIM
    qb0 = (A_HEADS + 2 * A_KV_HEADS) * HEAD_DIM // wb
    kb0, vb0 = qb0 + B_HEADS // hp, qb0 + 2 * B_HEADS // hp
    assert (A_HEADS * HEAD_DIM) % wk == 0 and ((A_HEADS + 2 * A_KV_HEADS) * HEAD_DIM) % wb == 0

    nq = length // tq_a
    wq = hp * A_GROUP * HEAD_DIM
    in_specs = [pl.BlockSpec((tq_a, wq), lambda b, h, i: (b * nq + i, h)),
                pl.BlockSpec((length, wk), lambda b, h, i: (b, ka0 + h)),
                pl.BlockSpec((length, wk), lambda b, h, i: (b, va0 + h))]
    args = [qkv, qkv, qkv]
    if has_cache:
        pc = caches[0].shape[0] // nb
        in_specs += [pl.BlockSpec((pc, wk), lambda b, h, i: (b, h))] * 2
        args += [caches[0], caches[1]]
    rb_a = min(256, A_GROUP * tq_a)
    oa = pl.pallas_call(
        functools.partial(_gqa_kernel, has_cache=has_cache, tk=tk, rb=rb_a),
        grid=(nb, A_KV_HEADS // hp, nq),
        in_specs=in_specs,
        out_specs=pl.BlockSpec((tq_a, wq), lambda b, h, i: (b * nq + i, h)),
        out_shape=jax.ShapeDtypeStruct((r, A_HEADS * HEAD_DIM), BF16),
        scratch_shapes=[pltpu.VMEM((A_GROUP * tq_a, lk), F32)],
        compiler_params=_cparams(3),
    )(*args)

    nq = length // tq_b
    in_specs = [pl.BlockSpec((tq_b, wb), lambda b, h, i: (b * nq + i, qb0 + h)),
                pl.BlockSpec((length, wb), lambda b, h, i: (b, kb0 + h)),
                pl.BlockSpec((length, wb), lambda b, h, i: (b, vb0 + h))]
    args = [qkv, qkv, qkv]
    if has_cache:
        in_specs += [pl.BlockSpec((pc, wb), lambda b, h, i: (b, h))] * 2
        args += [caches[2], caches[3]]
    in_specs += [pl.BlockSpec((4, HEAD_DIM), lambda b, h, i: (0, 0)),
                 pl.BlockSpec((1, B_V_DIM), lambda b, h, i: (0, 0))]
    args += [lam_p, subln_g]
    ob = pl.pallas_call(
        functools.partial(_diff_kernel, has_cache=has_cache, tk=tk, rb=min(128, tq_b), lam_init=lam_init),
        grid=(nb, B_HEADS // hp, nq),
        in_specs=in_specs,
        out_specs=pl.BlockSpec((tq_b, wb), lambda b, h, i: (b * nq + i, h)),
        out_shape=jax.ShapeDtypeStruct((r, B_HEADS * B_V_DIM), BF16),
        scratch_shapes=[pltpu.VMEM((tq_b, lk), F32)],
        compiler_params=_cparams(3),
    )(*args)
    return oa, ob


def _outproj_kernel(*refs, n_in):
    o_refs = refs[:n_in]
    w_refs = refs[n_in:2 * n_in]
    x_ref, g_ref, gate_ref, out_ref = refs[2 * n_in:]
    y = None
    for o_ref, w_ref in zip(o_refs, w_refs):
        t = jnp.dot(o_ref[...], w_ref[...], preferred_element_type=F32)
        y = t if y is None else y + t
    out_ref[...] = x_ref[...] + gate_ref[0] * _rms(y, g_ref[...])


def _outproj(os_, w, x, g, mod, row_fn, tm):
    r, d = x.shape
    n_in = len(os_)
    kw = os_[0].shape[1]
    assert all(o.shape[1] == kw for o in os_) and w.shape == (n_in * kw, d)
    in_specs = ([pl.BlockSpec((tm, kw), lambda i: (i, 0)) for _ in os_]
                + [pl.BlockSpec((kw, d), lambda i, k=k: (k, 0), pipeline_mode=pl.Buffered(1)) for k in range(n_in)]
                + [pl.BlockSpec((tm, d), lambda i: (i, 0)),
                   pl.BlockSpec((1, d), lambda i: (0, 0)),
                   _mod_spec(d, 2, row_fn)])
    return pl.pallas_call(
        functools.partial(_outproj_kernel, n_in=n_in),
        grid=(r // tm,),
        in_specs=in_specs,
        out_specs=pl.BlockSpec((tm, d), lambda i: (i, 0)),
        out_shape=jax.ShapeDtypeStruct((r, d), F32),
        compiler_params=_cparams(1),
    )(*os_, *([w] * n_in), x, g, mod)


def _mlp_kernel(x_ref, g2_ref, sh_ref, sc_ref, gate_ref, g3_ref, w1_ref, w2_ref, out_ref, n_scr):
    f = pl.program_id(1)

    @pl.when(f == 0)
    def _():
        n = _rms(x_ref[...], g2_ref[...]) * (1.0 + sc_ref[0]) + sh_ref[0]
        n_scr[...] = n.astype(BF16)
        out_ref[...] = jnp.zeros_like(out_ref)

    h = jnp.maximum(jnp.dot(n_scr[...], w1_ref[...], preferred_element_type=F32), 0.0)
    hb = (h * h).astype(BF16)
    d = out_ref.shape[1]
    for c0 in range(0, d, MLP_ACC_CHUNK):
        out_ref[:, c0:c0 + MLP_ACC_CHUNK] += jnp.dot(hb, w2_ref[:, c0:c0 + MLP_ACC_CHUNK],
                                                     preferred_element_type=F32)

    @pl.when(f == pl.num_programs(1) - 1)
    def _():
        out_ref[...] = x_ref[...] + gate_ref[0] * _rms(out_ref[...], g3_ref[...])


def _mlp(x, g2, g3, mod, row_fn, w1, w2, layer, tm, tf):
    r, d = x.shape
    dff = w1.shape[2]
    return pl.pallas_call(
        _mlp_kernel,
        grid=(r // tm, dff // tf),
        in_specs=[pl.BlockSpec((tm, d), lambda i, f: (i, 0)),
                  pl.BlockSpec((1, d), lambda i, f: (0, 0)),
                  _mod_spec(d, 3, row_fn), _mod_spec(d, 4, row_fn), _mod_spec(d, 5, row_fn),
                  pl.BlockSpec((1, d), lambda i, f: (0, 0)),
                  pl.BlockSpec((None, d, tf), lambda i, f: (layer, 0, f)),
                  pl.BlockSpec((None, tf, d), lambda i, f: (layer, f, 0))],
        out_specs=pl.BlockSpec((tm, d), lambda i, f: (i, 0)),
        out_shape=jax.ShapeDtypeStruct((r, d), F32),
        scratch_shapes=[pltpu.VMEM((tm, d), BF16)],
        compiler_params=_cparams(2),
    )(x, g2, mod, mod, mod, g3, w1, w2)


def _ssm_in_kernel(x_ref, g_ref, sh_ref, sc_ref, w_ref, u_ref):
    n = (_rms(x_ref[...], g_ref[...]) * (1.0 + sc_ref[0]) + sh_ref[0]).astype(BF16)
    for c0 in range(0, w_ref.shape[1], PROJ_CHUNK):
        u_ref[:, c0:c0 + PROJ_CHUNK] = jnp.dot(n, w_ref[:, c0:c0 + PROJ_CHUNK], preferred_element_type=F32)


def _ssm_in(x, g, mod, row_fn, w, tm):
    r, d = x.shape
    n = w.shape[1]
    return pl.pallas_call(
        _ssm_in_kernel,
        grid=(r // tm,),
        in_specs=[pl.BlockSpec((tm, d), lambda i: (i, 0)),
                  pl.BlockSpec((1, d), lambda i: (0, 0)),
                  _mod_spec(d, 0, row_fn), _mod_spec(d, 1, row_fn),
                  pl.BlockSpec((d, n), lambda i: (0, 0), pipeline_mode=pl.Buffered(1))],
        out_specs=pl.BlockSpec((tm, n), lambda i: (i, 0)),
        out_shape=jax.ShapeDtypeStruct((r, n), F32),
        compiler_params=_cparams(1),
    )(x, g, mod, mod, w)


def _ssm_params(a_re, a_im, log_dt, b_re, b_im, c_re, c_im, seg_chunks):
    g = a_re.shape[1]
    vec = lambda x: x.astype(F32).reshape(2, g, 1, -1)
    args = (vec(log_dt), vec(a_re), vec(a_im)) + tuple(x.astype(F32) for x in (b_re, b_im, c_re, c_im))
    per_group = lambda a: pl.BlockSpec((2, None) + a.shape[2:], lambda i: (0, i, 0, 0))
    blk = lambda a: pl.BlockSpec((1,) + a.shape[1:], lambda i: (i, 0, 0))
    out_shape = [jax.ShapeDtypeStruct((g, SSM_KDIM, SSM_LANES), BF16),
                 jax.ShapeDtypeStruct((g, SSM_KDIM, SSM_KDIM), BF16),
                 jax.ShapeDtypeStruct((g, SSM_KDIM, SSM_LANES), BF16),
                 jax.ShapeDtypeStruct((g, 1, SSM_LANES), F32),
                 jax.ShapeDtypeStruct((g, 1, SSM_LANES), F32)]
    return pl.pallas_call(
        functools.partial(_ssm_params_kernel, seg_chunks=seg_chunks),
        grid=(g,),
        in_specs=[per_group(a) for a in args],
        out_specs=[blk(o) for o in out_shape],
        out_shape=out_shape,
        compiler_params=_cparams(1),
    )(*args)


def _ssm_params_kernel(ldt_ref, are_ref, aim_ref, br_ref, bi_ref, cr_ref, ci_ref,
                       win_ref, toep_ref, cout_ref, achunk_ref, aseg_ref, *, seg_chunks):
    t, nch, half = SSM_CHUNK, S5_GROUP_CH, 2 * S5_STATE
    hi = lax.Precision.HIGHEST
    both = lambda ref: jnp.concatenate([ref[0], ref[1]], axis=1)
    eye = (lax.broadcasted_iota(jnp.int32, (S5_STATE, S5_STATE), 0)
           == lax.broadcasted_iota(jnp.int32, (S5_STATE, S5_STATE), 1)).astype(F32)
    tr = lambda m: lax.dot_general(m, eye, (((0,), (0,)), ((), ())), precision=hi, preferred_element_type=F32)
    both_t = lambda ref: jnp.concatenate([tr(ref[0]), tr(ref[1])], axis=1)
    ar, ai = both(are_ref), both(aim_ref)
    dt = jnp.exp(jnp.concatenate([jnp.broadcast_to(ldt_ref[d], (1, S5_STATE)) for d in range(2)], axis=1))
    lr, li = ar * dt, ai * dt

    def power(k):
        mag = jnp.exp(lr * k)
        return mag * jnp.cos(li * k), mag * jnp.sin(li * k)

    pw = [(jnp.ones_like(lr), jnp.zeros_like(lr)), power(1.0)]
    for _ in range(t - 1):
        pw.append((pw[-1][0] * pw[1][0] - pw[-1][1] * pw[1][1], pw[-1][0] * pw[1][1] + pw[-1][1] * pw[1][0]))
    fwd1 = lax.broadcasted_iota(jnp.int32, (1, half), 1) < S5_STATE
    pick = lambda kf, kb: (jnp.where(fwd1, pw[kf][0], pw[kb][0]), jnp.where(fwd1, pw[kf][1], pw[kb][1]))
    xr, xi = pw[1][0] - 1.0, pw[1][1]
    den = ar * ar + ai * ai
    qr, qi = (xr * ar + xi * ai) / den, (xi * ar - xr * ai) / den
    btr, bti = both_t(br_ref), both_t(bi_ref)
    bbr = qr * btr - qi * bti
    bbi = qr * bti + qi * btr
    ccr, cci = both(cr_ref), both(ci_ref)

    for s in range(t):
        rows = slice(s * nch, (s + 1) * nch)
        er, ei = pick(t - 1 - s, s)
        win_ref[0, rows, 0:half] = (er * bbr - ei * bbi).astype(BF16)
        win_ref[0, rows, half:] = (er * bbi + ei * bbr).astype(BF16)
        er, ei = pick(s + 1, t - s)
        cout_ref[0, rows, 0:half] = (ccr * er - cci * ei).astype(BF16)
        cout_ref[0, rows, half:] = (-(ccr * ei + cci * er)).astype(BF16)

    fwd = lax.broadcasted_iota(jnp.int32, (nch, half), 1) < S5_STATE
    cp = [(ccr * pw[k][0] - cci * pw[k][1], ccr * pw[k][1] + cci * pw[k][0]) for k in range(t)]
    stack = lambda ks, j: jnp.concatenate([cp[k][j] for k in ks], axis=0)
    nt = lambda x, y: lax.dot_general(x, y, (((1,), (1,)), ((), ())), precision=hi, preferred_element_type=F32)
    zero = jnp.zeros_like(bbr)
    order, rev = list(range(t)), list(range(t - 1, -1, -1))
    taps_f = (nt(jnp.where(fwd, bbr, zero), stack(order, 0)) - nt(jnp.where(fwd, bbi, zero), stack(order, 1)))
    taps_b = (nt(jnp.where(fwd, zero, bbr), stack(rev, 0)) - nt(jnp.where(fwd, zero, bbi), stack(rev, 1)))
    lane = lax.broadcasted_iota(jnp.int32, (nch, SSM_KDIM), 1)
    for s in range(t):
        tf = taps_f if s == 0 else pltpu.roll(taps_f, s * nch, axis=1)
        tb = taps_b if s == t - 1 else pltpu.roll(taps_b, (s + 1) * nch, axis=1)
        blk = jnp.where(lane >= s * nch, tf, 0.0) + jnp.where(lane < (s + 1) * nch, tb, 0.0)
        toep_ref[0, s * nch:(s + 1) * nch, :] = blk.astype(BF16)

    for ref, (dr, di) in ((achunk_ref, pw[t]), (aseg_ref, power(float(t * seg_chunks)))):
        ref[0, :, 0:half] = dr
        ref[0, :, half:] = di


def _cmul_add(ar, ai, h_re, h_im, s_re, s_im):
    return ar * h_re - ai * h_im + s_re, ar * h_im + ai * h_re + s_im


def _regroup_perms():
    n = SSM_CHUNK * V7X_LANES
    idx = jnp.arange(n)
    tok, grp, ch = idx // V7X_LANES, idx % V7X_LANES // S5_GROUP_CH, idx % S5_GROUP_CH
    dst = grp * SSM_KDIM + tok * S5_GROUP_CH + ch
    p_in = (dst[:, None] == idx[None, :]).astype(BF16)
    return p_in, p_in.T


def _regroup_in(u_ref, pin_ref, lhs_scr, nrows):
    t, rbk = SSM_CHUNK, min(SSM_REGROUP_ROWS, nrows)

    def body(rb, carry):
        pieces = [u_ref[pl.ds(rb * (rbk * t) + tt, rbk, stride=t), :].astype(BF16) for tt in range(t)]
        out = jnp.dot(jnp.concatenate(pieces, axis=1), pin_ref[...], preferred_element_type=F32)
        rows = pl.ds(pl.multiple_of(rb * rbk, rbk), rbk)
        for gl in range(SSM_TILE_GROUPS):
            lhs_scr[gl, rows, :] = out[:, gl * SSM_KDIM:(gl + 1) * SSM_KDIM].astype(BF16)
        return carry

    lax.fori_loop(0, nrows // rbk, body, 0)


def _regroup_out(y_scr, pout_ref, y_ref, nrows):
    t, rbk = SSM_CHUNK, min(SSM_REGROUP_ROWS, nrows)

    def body(rb, carry):
        rows = pl.ds(pl.multiple_of(rb * rbk, rbk), rbk)
        rest = jnp.concatenate([y_scr[gl, rows, :] for gl in range(SSM_TILE_GROUPS)], axis=1)
        out = None
        for _ in range(SSM_Y_TERMS):
            term = rest.astype(BF16)
            rest = rest - term.astype(F32)
            part = jnp.dot(term, pout_ref[...], preferred_element_type=F32)
            out = part if out is None else out + part
        for tt in range(t):
            y_ref[pl.ds(rb * (rbk * t) + tt, rbk, stride=t), :] = out[:, tt * V7X_LANES:(tt + 1) * V7X_LANES]
        return carry

    lax.fori_loop(0, nrows // rbk, body, 0)


def _ssm_core_kernel(u_ref, pin_ref, pout_ref, win_ref, toep_ref, cout_ref, a_ref, aseg_ref, h0_ref, y_ref, hfin_ref,
                     lhs_scr, y_scr, s_re, s_im, tf_re, tb_re, tf_im, tb_im, *, nseq, nc, nsub):
    step = pl.program_id(1)
    nrows = nseq * nc

    @pl.when(step == 0)
    def _():
        _regroup_in(u_ref, pin_ref, lhs_scr, nrows)

    for k in range(SSM_GROUPS_PER_STEP):
        _ssm_group(k, step * SSM_GROUPS_PER_STEP + k, win_ref, toep_ref, cout_ref, a_ref, aseg_ref, h0_ref, hfin_ref,
                   lhs_scr, y_scr, s_re, s_im, tf_re, tb_re, tf_im, tb_im, nseq, nc, nsub)

    @pl.when(step == pl.num_programs(1) - 1)
    def _():
        _regroup_out(y_scr, pout_ref, y_ref, nrows)


def _ssm_group(k, g, win_ref, toep_ref, cout_ref, a_ref, aseg_ref, h0_ref, hfin_ref,
               lhs_scr, y_scr, s_re, s_im, tf_re, tb_re, tf_im, tb_im, nseq, nc, nsub):
    half = SSM_LANES // 2
    pitch = nc + SSM_PITCH_PAD
    nrows = nseq * nc
    nslab = nseq // V7X_SUBLANES
    base = k * nseq * pitch

    lhs = lhs_scr[g]
    s = jnp.dot(lhs, win_ref[k], preferred_element_type=F32)
    for p in range(nseq):
        s_re[base + p * pitch:base + p * pitch + nc, :] = s[p * nc:(p + 1) * nc, 0:half]
        s_im[base + p * pitch:base + p * pitch + nc, :] = s[p * nc:(p + 1) * nc, half:]

    ar, ai = a_ref[k][:, 0:half], a_ref[k][:, half:]
    fwd = lax.broadcasted_iota(jnp.int32, (V7X_SUBLANES, half), 1) < S5_STATE

    def slab(q, c):
        return pl.ds(base + q * V7X_SUBLANES * pitch + c, V7X_SUBLANES, stride=pitch)

    def scan(init, store):
        hs = list(init)
        for c in range(nc):
            cb = nc - 1 - c
            for q in range(nslab):
                h_re, h_im = hs[q]
                x_re = jnp.where(fwd, s_re[slab(q, c), :], s_re[slab(q, cb), :])
                x_im = jnp.where(fwd, s_im[slab(q, c), :], s_im[slab(q, cb), :])
                if store:
                    tf_re[slab(q, c), :] = h_re
                    tb_re[slab(q, cb), :] = h_re
                    tf_im[slab(q, c), :] = h_im
                    tb_im[slab(q, cb), :] = h_im
                hs[q] = _cmul_add(ar, ai, h_re, h_im, x_re, x_im)
        return hs

    rows8 = lambda q: slice(q * V7X_SUBLANES, (q + 1) * V7X_SUBLANES)
    init = [(h0_ref[k, rows8(q), 0:half], h0_ref[k, rows8(q), half:]) for q in range(nslab)]
    if nsub > 1:
        zero = jnp.zeros((V7X_SUBLANES, half), F32)
        ends = scan([(zero, zero)] * nslab, False)
        gr, gi = aseg_ref[k][:, 0:half], aseg_ref[k][:, half:]
        seg = lax.broadcasted_iota(jnp.int32, (V7X_SUBLANES, half), 0) % nsub
        carried = jnp.where(fwd, seg, nsub - 1 - seg) != 0
        shift = lambda x: jnp.where(fwd, pltpu.roll(x, 1, axis=0), pltpu.roll(x, V7X_SUBLANES - 1, axis=0))
        for q in range(nslab):
            (h0_re, h0_im), (f_re, f_im) = init[q], ends[q]
            e_re, e_im = h0_re, h0_im
            for _ in range(nsub - 1):
                x_re, x_im = _cmul_add(gr, gi, e_re, e_im, f_re, f_im)
                e_re = h0_re + jnp.where(carried, shift(x_re), 0.0)
                e_im = h0_im + jnp.where(carried, shift(x_im), 0.0)
            init[q] = (e_re, e_im)
    hs = scan(init, True)
    for q in range(nslab):
        hfin_ref[k, rows8(q), 0:half] = hs[q][0]
        hfin_ref[k, rows8(q), half:] = hs[q][1]

    gather = lambda ref: jnp.concatenate([ref[base + p * pitch:base + p * pitch + nc, :] for p in range(nseq)],
                                         axis=0)
    fwd_rows = lax.broadcasted_iota(jnp.int32, (nrows, half), 1) < S5_STATE
    hp = jnp.concatenate([jnp.where(fwd_rows, gather(tf_re), gather(tb_re)),
                          jnp.where(fwd_rows, gather(tf_im), gather(tb_im))], axis=1).astype(BF16)
    y_scr[g] = (jnp.dot(lhs, toep_ref[k], preferred_element_type=F32)
                + lax.dot_general(hp, cout_ref[k], (((1,), (1,)), ((), ())), preferred_element_type=F32))


def _ssm_core(u, ops, h0, nseq, nc, nsub):
    r, d = u.shape
    nrows = nseq * nc
    assert r == nrows * SSM_CHUNK and nseq % V7X_SUBLANES == 0 and V7X_SUBLANES % nsub == 0
    assert nrows % min(SSM_REGROUP_ROWS, nrows) == 0 and nc % V7X_SUBLANES == 0
    ntile = d // V7X_LANES
    gps = SSM_GROUPS_PER_STEP
    steps = SSM_TILE_GROUPS // gps
    grp = lambda j, g: (j * steps + g, 0, 0)
    scan_rows = gps * nseq * (nc + SSM_PITCH_PAD)
    p_in, p_out = _regroup_perms()
    perm_spec = pl.BlockSpec(p_in.shape, lambda j, g: (0, 0), pipeline_mode=pl.Buffered(1))
    return pl.pallas_call(
        functools.partial(_ssm_core_kernel, nseq=nseq, nc=nc, nsub=nsub),
        grid=(ntile, steps),
        in_specs=[pl.BlockSpec((r, V7X_LANES), lambda j, g: (0, j)),
                  perm_spec, perm_spec]
                 + [pl.BlockSpec((gps,) + o.shape[1:], grp) for o in ops]
                 + [pl.BlockSpec((gps, nseq, SSM_LANES), grp)],
        out_specs=[pl.BlockSpec((r, V7X_LANES), lambda j, g: (0, j)),
                   pl.BlockSpec((gps, nseq, SSM_LANES), grp)],
        out_shape=[jax.ShapeDtypeStruct((r, d), F32),
                   jax.ShapeDtypeStruct((d // S5_GROUP_CH, nseq, SSM_LANES), F32)],
        scratch_shapes=[pltpu.VMEM((SSM_TILE_GROUPS, nrows, SSM_KDIM), BF16),
                        pltpu.VMEM((SSM_TILE_GROUPS, nrows, SSM_KDIM), F32)]
                       + [pltpu.VMEM((scan_rows, V7X_LANES), F32)] * 6,
        compiler_params=_cparams(2),
    )(u, p_in, p_out, *ops, h0)


def _glu_kernel(y_ref, u_ref, d_ref, w_ref, z_ref):
    v = y_ref[...] + d_ref[...] * u_ref[...]
    z = 0.5 * v * (1.0 + jnp.tanh(math.sqrt(2.0 / math.pi) * (v + 0.044715 * (v * v * v))))
    zb = z.astype(BF16)
    for c0 in range(0, w_ref.shape[1], PROJ_CHUNK):
        cols = slice(c0, c0 + PROJ_CHUNK)
        gl = jnp.dot(zb, w_ref[:, cols], preferred_element_type=F32)
        z_ref[:, cols] = (z[:, cols] * _sigmoid(gl)).astype(BF16)


def _glu(y, u, d, w, tm):
    r, dm = y.shape
    return pl.pallas_call(
        _glu_kernel,
        grid=(r // tm,),
        in_specs=[pl.BlockSpec((tm, dm), lambda i: (i, 0)),
                  pl.BlockSpec((tm, dm), lambda i: (i, 0)),
                  pl.BlockSpec((1, dm), lambda i: (0, 0)),
                  pl.BlockSpec(w.shape, lambda i: (0, 0), pipeline_mode=pl.Buffered(1))],
        out_specs=pl.BlockSpec((tm, dm), lambda i: (i, 0)),
        out_shape=jax.ShapeDtypeStruct((r, dm), BF16),
        compiler_params=_cparams(1),
    )(y, u, d, w)


def _rope_tables(length):
    rows = length // GRID_W
    row = jnp.repeat(jnp.arange(rows, dtype=F32), GRID_W)
    col = jnp.tile(jnp.arange(GRID_W, dtype=F32), rows)
    inv = ROPE_BASE ** (-jnp.arange(ROPE_PAIRS_PER_AXIS, dtype=F32) / ROPE_PAIRS_PER_AXIS)
    ang = jnp.concatenate([row[:, None] * inv, col[:, None] * inv], axis=-1)
    cos, sin = jnp.cos(ang), jnp.sin(ang)
    return jnp.concatenate([cos, cos], axis=-1), jnp.concatenate([-sin, sin], axis=-1)


def kernel(x_prompt, x_sample, c, cache_a_k, cache_a_v, cache_b_k, cache_b_v, state_ssm, c_ctx, ada_w, ada_b, norm_g, mlp_w1, mlp_w2, attn_w_in, attn_w_out, attn_qk_norm, diff_lambda, diff_subln, ssm_w_in, ssm_a_re, ssm_a_im, ssm_log_dt, ssm_b_re, ssm_b_im, ssm_c_re, ssm_c_im, ssm_d, ssm_glu_w, ssm_w_out):
    bp, lp, d = x_prompt.shape
    bs, ls, _ = x_sample.shape
    depth = ada_w.shape[0]
    past = cache_a_k.shape[2]
    nsub = SSM_SEGMENTS
    assert bs + 1 <= 8 and bp % V7X_SUBLANES == 0
    assert lp % SSM_CHUNK == 0 and ls % (SSM_CHUNK * nsub) == 0 and ls % GRID_W == 0

    cond8 = jnp.zeros((8, d), F32).at[0].set(c_ctx).at[1:1 + bs].set(c)
    mods = _modulation(cond8, ada_w, ada_b).reshape(depth, 8 * N_MOD, 1, d)

    xp = x_prompt.reshape(bp * lp, d)
    xs = x_sample.reshape(bs * ls, d)
    tm = min(512, bp * lp, ls)
    tm_big = min(1024, bp * lp, ls)
    row_p = lambda i: 0
    row_s = lambda i: 1 + i // (ls // tm)
    row_s_big = lambda i: 1 + i // (ls // tm_big)
    w1_all = mlp_w1.astype(BF16)
    w2_all = mlp_w2.astype(BF16)
    new_ak = new_av = new_bk = new_bv = new_ssm = None
    for l in range(depth):
        mod = mods[l]
        g = norm_g[l][:, None, :]
        i = l // 2
        if l % 2 == 0:
            lam_init = 0.8 - 0.6 * math.exp(-0.3 * l)
            w_in = attn_w_in[i].astype(BF16)
            w_out = attn_w_out[i].astype(BF16)
            qkv, ka, va, kb, vb = _qkv_proj(xp, g[0], mod, row_p, w_in, attn_qk_norm[i], None, True, tm)
            new_ak = ka.reshape(bp, 1, lp, A_KV_HEADS, HEAD_DIM)
            new_av = va.reshape(bp, 1, lp, A_KV_HEADS, HEAD_DIM)
            new_bk = kb.reshape(bp, 1, lp, B_HEADS, 2, HEAD_DIM)
            new_bv = vb.reshape(bp, 1, lp, B_HEADS, B_V_DIM)
            oa, ob = _attention(qkv, bp, None, diff_lambda[i], diff_subln[i][None, :], lam_init,
                                tq_a=lp, tq_b=lp, tk=512, hp=2)
            xp = _outproj([oa, ob], w_out, xp, g[1], mod, row_p, tm)
            (qkv,) = _qkv_proj(xs, g[0], mod, row_s, w_in, attn_qk_norm[i], _rope_tables(ls), False, tm)
            caches = (cache_a_k[:, i].reshape(bs * past, A_KV_HEADS * HEAD_DIM),
                      cache_a_v[:, i].reshape(bs * past, A_KV_HEADS * HEAD_DIM),
                      cache_b_k[:, i].reshape(bs * past, B_HEADS * 2 * HEAD_DIM),
                      cache_b_v[:, i].reshape(bs * past, B_HEADS * B_V_DIM))
            oa, ob = _attention(qkv, bs, caches, diff_lambda[i], diff_subln[i][None, :], lam_init,
                                tq_a=min(256, ls), tq_b=min(512, ls), tk=512, hp=1)
            xs = _outproj([oa, ob], w_out, xs, g[1], mod, row_s, tm)
        else:
            w_in = ssm_w_in[i].astype(BF16)
            u_p = _ssm_in(xp, g[0], mod, row_p, w_in, tm_big)
            u_s = _ssm_in(xs, g[0], mod, row_s_big, w_in, tm_big)
            ng = d // S5_GROUP_CH
            nc_p, nc_s = lp // SSM_CHUNK, ls // SSM_CHUNK // nsub
            ops = _ssm_params(ssm_a_re[i], ssm_a_im[i], ssm_log_dt[i], ssm_b_re[i], ssm_b_im[i],
                                                 ssm_c_re[i], ssm_c_im[i], nc_s)
            st = state_ssm[:, i].astype(F32)
            h0 = jnp.zeros((ng, bs, nsub, 4, S5_STATE), F32)
            h0 = h0.at[:, :, 0, 0].set(jnp.transpose(st[:, 0, :, :, 0], (1, 0, 2)))
            h0 = h0.at[:, :, 0, 2].set(jnp.transpose(st[:, 0, :, :, 1], (1, 0, 2)))
            h0 = h0.at[:, :, nsub - 1, 1].set(jnp.transpose(st[:, 1, :, :, 0], (1, 0, 2)))
            h0 = h0.at[:, :, nsub - 1, 3].set(jnp.transpose(st[:, 1, :, :, 1], (1, 0, 2)))
            h0 = h0.reshape(ng, bs * nsub, SSM_LANES)
            y_p, hfin = _ssm_core(u_p, ops, jnp.zeros((ng, bp, SSM_LANES), F32), bp, nc_p, 1)
            y_s, _ = _ssm_core(u_s, ops, h0, bs * nsub, nc_s, nsub)
            new_ssm = jnp.transpose(hfin.reshape(ng, bp, 2, 2, S5_STATE), (1, 3, 0, 4, 2))
            glu_w = ssm_glu_w[i].astype(BF16)
            w_out = ssm_w_out[i].astype(BF16)
            dvec = ssm_d[i][None, :]
            z_p = _glu(y_p, u_p, dvec, glu_w, tm)
            z_s = _glu(y_s, u_s, dvec, glu_w, tm)
            xp = _outproj([z_p], w_out, xp, g[1], mod, row_p, tm)
            xs = _outproj([z_s], w_out, xs, g[1], mod, row_s, tm)
        xp = _mlp(xp, g[2], g[3], mod, row_p, w1_all, w2_all, l, tm_big, 512)
        xs = _mlp(xs, g[2], g[3], mod, row_s_big, w1_all, w2_all, l, tm_big, 512)
    return (xp.reshape(bp, lp, d), xs.reshape(bs, ls, d), new_ak, new_av, new_bk, new_bv,
            new_ssm[:, None])
```

```python
import functools
import math

import jax
import jax.numpy as jnp
from jax import lax
from jax.experimental import pallas as pl
from jax.experimental.pallas import tpu as pltpu

F32 = jnp.float32
BF16 = jnp.bfloat16
EPS = 1e-6

HEAD_DIM = 128
A_HEADS = 8
A_KV_HEADS = 2
A_GROUP = A_HEADS // A_KV_HEADS
B_HEADS = 4
B_V_DIM = 2 * HEAD_DIM
GRID_W = 64
ROPE_BASE = 10000.0
ROPE_PAIRS_PER_AXIS = HEAD_DIM // 4
N_MOD = 6
S5_GROUP_CH = 16
S5_STATE = 64

V7X_LANES = 128
V7X_SUBLANES = 8
V7X_VMEM_LIMIT_BYTES = 56 * 1024 * 1024

SSM_CHUNK = V7X_SUBLANES
SSM_TILE_GROUPS = V7X_LANES // S5_GROUP_CH
SSM_KDIM = SSM_CHUNK * S5_GROUP_CH
SSM_LANES = 4 * S5_STATE
SSM_GROUPS_PER_STEP = 2
SSM_PARAM_GROUPS = 4
SSM_SEGMENTS = 8
SSM_REGROUP_ROWS = 256
SSM_Y_TERMS = 2
SSM_PITCH_PAD = 8

MLP_ACC_CHUNK = 512
QKV_CHUNK = 256
PROJ_CHUNK = 512


def _cparams(n_axes):
    return pltpu.CompilerParams(dimension_semantics=("arbitrary",) * n_axes,
                                vmem_limit_bytes=V7X_VMEM_LIMIT_BYTES)


def _sigmoid(x):
    return 1.0 / (1.0 + jnp.exp(-x))


def _rms(x, g):
    return x * lax.rsqrt(jnp.mean(x * x, axis=-1, keepdims=True) + EPS) * g


def _mod_spec(d, slot, row_fn):
    return pl.BlockSpec((1, 1, d), lambda i, *_: (row_fn(i) * N_MOD + slot, 0, 0))


def _mod_kernel(c_ref, w_ref, b_ref, o_ref):
    c = c_ref[...]
    s = (c * _sigmoid(c)).astype(BF16)
    o_ref[0] = jnp.dot(s, w_ref[0].astype(BF16), preferred_element_type=F32) + b_ref[0]


def _modulation(cond8, ada_w, ada_b):
    depth, d, n = ada_w.shape
    tn = 1024
    return pl.pallas_call(
        _mod_kernel,
        grid=(depth, n // tn),
        in_specs=[pl.BlockSpec((8, d), lambda l, j: (0, 0)),
                  pl.BlockSpec((1, d, tn), lambda l, j: (l, 0, j)),
                  pl.BlockSpec((1, 1, tn), lambda l, j: (l, 0, j))],
        out_specs=pl.BlockSpec((1, 8, tn), lambda l, j: (l, 0, j)),
        out_shape=jax.ShapeDtypeStruct((depth, 8, n), F32),
        compiler_params=_cparams(2),
    )(cond8, ada_w, ada_b.reshape(depth, 1, n))


def _split_heads(y, fn):
    return jnp.concatenate([fn(y[:, h * HEAD_DIM:(h + 1) * HEAD_DIM]) for h in range(y.shape[1] // HEAD_DIM)], axis=1)


def _qkv_kernel(*refs, rope, caches):
    x_ref, g_ref, sh_ref, sc_ref, w_ref, qkg_ref = refs[:6]
    refs = refs[6:]
    if rope:
        cos_ref, sin_ref = refs[:2]
        refs = refs[2:]
    qkv_ref = refs[0]
    if caches:
        ka_ref, va_ref, kb_ref, vb_ref = refs[1:5]
    n = (_rms(x_ref[...], g_ref[...]) * (1.0 + sc_ref[0]) + sh_ref[0]).astype(BF16)

    def rot(y):
        if not rope:
            return y
        return y * cos_ref[...] + pltpu.roll(y, HEAD_DIM // 2, axis=1) * sin_ref[...]

    q_scale = HEAD_DIM ** -0.5 * math.log2(math.e)
    qa_end = A_HEADS * HEAD_DIM
    ka_end = qa_end + QKV_CHUNK
    va_end = ka_end + QKV_CHUNK
    qb_end = va_end + B_HEADS * 2 * HEAD_DIM
    kb_end = qb_end + B_HEADS * 2 * HEAD_DIM
    for c0 in range(0, w_ref.shape[1], QKV_CHUNK):
        cols = slice(c0, c0 + QKV_CHUNK)
        y = jnp.dot(n, w_ref[:, cols], preferred_element_type=F32)
        if c0 < qa_end:
            y = _split_heads(y, lambda t: rot(_rms(t, qkg_ref[0:1, :])) * q_scale)
        elif c0 < ka_end:
            y = _split_heads(y, lambda t: _rms(t, qkg_ref[1:2, :]))
            if caches:
                _store_cache(ka_ref, (), y, HEAD_DIM)
            y = _split_heads(y, rot)
        elif c0 < va_end:
            if caches:
                _store_cache(va_ref, (), y, HEAD_DIM)
        elif c0 < qb_end:
            y = _split_heads(y, lambda t: rot(t) * q_scale)
        elif c0 < kb_end:
            if caches:
                _store_cache(kb_ref, ((c0 - qb_end) // QKV_CHUNK,), y, HEAD_DIM)
            y = _split_heads(y, rot)
        elif caches:
            _store_cache(vb_ref, (), y, B_V_DIM, (c0 - kb_end) // QKV_CHUNK)
        qkv_ref[:, cols] = y.astype(BF16)


def _store_cache(ref, lead, y, width, only=None):
    length = ref.shape[2]
    for b in range(ref.shape[0]):
        rows = slice(b * length, (b + 1) * length)
        for j in range(y.shape[1] // width):
            idx = (b, 0, slice(None)) + tuple(lead) + (j if only is None else only, slice(None))
            ref[idx] = y[rows, j * width:(j + 1) * width]


def _qkv_proj(x, g, mod, row_fn, w, qk_g, rope_tabs, caches, tm):
    r, d = x.shape
    n = w.shape[1]
    assert A_KV_HEADS * HEAD_DIM == QKV_CHUNK and r % tm == 0 and n % QKV_CHUNK == 0
    assert not (caches and rope_tabs is not None)
    row = lambda i: (i, 0)
    in_specs = [pl.BlockSpec((tm, d), row),
                pl.BlockSpec((1, d), lambda i: (0, 0)),
                _mod_spec(d, 0, row_fn), _mod_spec(d, 1, row_fn),
                pl.BlockSpec((d, n), lambda i: (0, 0), pipeline_mode=pl.Buffered(1)),
                pl.BlockSpec((2, HEAD_DIM), lambda i: (0, 0))]
    args = [x, g, mod, mod, w, qk_g]
    if rope_tabs is not None:
        nblk = rope_tabs[0].shape[0] // tm
        in_specs += [pl.BlockSpec((tm, HEAD_DIM), lambda i: (i % nblk, 0))] * 2
        args += list(rope_tabs)
    out_specs = [pl.BlockSpec((tm, n), row)]
    out_shape = [jax.ShapeDtypeStruct((r, n), BF16)]
    if caches:
        assert tm % caches == 0
        for tail in ((A_KV_HEADS, HEAD_DIM), (A_KV_HEADS, HEAD_DIM), (B_HEADS, 2, HEAD_DIM), (B_HEADS, B_V_DIM)):
            out_specs.append(pl.BlockSpec((tm // caches, 1, caches) + tail, lambda i, z=(0,) * (2 + len(tail)): (i,) + z))
            out_shape.append(jax.ShapeDtypeStruct((r // caches, 1, caches) + tail, F32))
    return pl.pallas_call(
        functools.partial(_qkv_kernel, rope=rope_tabs is not None, caches=caches),
        grid=(r // tm,),
        in_specs=in_specs, out_specs=out_specs, out_shape=out_shape,
        compiler_params=_cparams(1),
    )(*args)


def _lane_tiles(x, op, acc):
    for j in range(0, x.shape[1], V7X_LANES):
        t = x[:, j:j + V7X_LANES]
        acc = t if acc is None else op(acc, t)
    return acc


def _softmax_pv(q, k_chunks, v_chunks, s_scr, rb):
    m_rows = q.shape[0]
    outs = []
    for r0 in range(0, m_rows, rb):
        qr = q[r0:r0 + rb]
        m_part = None
        off = 0
        for kc in k_chunks:
            k = kc()
            s = lax.dot_general(qr, k, (((1,), (1,)), ((), ())), preferred_element_type=F32)
            s_scr[r0:r0 + rb, off:off + k.shape[0]] = s
            m_part = _lane_tiles(s, jnp.maximum, m_part)
            off += k.shape[0]
        m = jnp.max(m_part, axis=-1, keepdims=True)
        l_part = acc = None
        off = 0
        for vc in v_chunks:
            v = vc()
            p = jnp.exp2(s_scr[r0:r0 + rb, off:off + v.shape[0]] - m)
            l_part = _lane_tiles(p, jnp.add, l_part)
            pv = jnp.dot(p.astype(BF16), v, preferred_element_type=F32)
            acc = pv if acc is None else acc + pv
            off += v.shape[0]
        outs.append(acc / jnp.sum(l_part, axis=-1, keepdims=True))
    return jnp.concatenate(outs, axis=0)


def _chunks(ref, c0, width, tk, cache_ref=None):
    length = ref.shape[0]
    out = [(lambda s=s: ref[s:s + tk, c0:c0 + width]) for s in range(0, length, tk)]
    if cache_ref is not None:
        out.append(lambda: cache_ref[:, c0:c0 + width].astype(BF16))
    return out


def _gqa_kernel(*refs, has_cache, tk, rb):
    if has_cache:
        q_ref, k_ref, v_ref, ck_ref, cv_ref, o_ref, s_scr = refs
    else:
        q_ref, k_ref, v_ref, o_ref, s_scr = refs
        ck_ref = cv_ref = None
    tq = q_ref.shape[0]
    wq = A_GROUP * HEAD_DIM
    for hh in range(k_ref.shape[1] // HEAD_DIM):
        q = jnp.concatenate([q_ref[:, hh * wq + g * HEAD_DIM:hh * wq + (g + 1) * HEAD_DIM] for g in range(A_GROUP)],
                            axis=0)
        o = _softmax_pv(q, _chunks(k_ref, hh * HEAD_DIM, HEAD_DIM, tk, ck_ref),
                        _chunks(v_ref, hh * HEAD_DIM, HEAD_DIM, tk, cv_ref), s_scr, rb)
        o_ref[:, hh * wq:(hh + 1) * wq] = jnp.concatenate([o[g * tq:(g + 1) * tq] for g in range(A_GROUP)],
                                                          axis=1).astype(BF16)


def _diff_kernel(*refs, has_cache, tk, rb, lam_init):
    if has_cache:
        q_ref, k_ref, v_ref, ck_ref, cv_ref, lam_ref, sg_ref, o_ref, s_scr = refs
    else:
        q_ref, k_ref, v_ref, lam_ref, sg_ref, o_ref, s_scr = refs
        ck_ref = cv_ref = None
    lp = lam_ref[...]
    lam = (jnp.exp(jnp.sum(lp[0:1] * lp[1:2], axis=-1, keepdims=True))
           - jnp.exp(jnp.sum(lp[2:3] * lp[3:4], axis=-1, keepdims=True)) + lam_init)
    for hh in range(v_ref.shape[1] // B_V_DIM):
        c0 = hh * B_V_DIM
        outs = []
        for m in range(2):
            cm = c0 + m * HEAD_DIM
            outs.append(_softmax_pv(q_ref[:, cm:cm + HEAD_DIM], _chunks(k_ref, cm, HEAD_DIM, tk, ck_ref),
                                    _chunks(v_ref, c0, B_V_DIM, tk, cv_ref), s_scr, rb))
        o = outs[0] - lam * outs[1]
        o_ref[:, c0:c0 + B_V_DIM] = (_rms(o, sg_ref[...]) * (1.0 - lam_init)).astype(BF16)


def _attention(qkv, nb, caches, lam_p, subln_g, lam_init, tq_a, tq_b, tk, hp):
    r = qkv.shape[0]
    length = r // nb
    has_cache = caches is not None
    lk = length + (caches[0].shape[0] // nb if has_cache else 0)
    tk = min(tk, length)
    wk = hp * HEAD_DIM
    ka0, va0 = A_HEADS * HEAD_DIM // wk, (A_HEADS + A_KV_HEADS) * HEAD_DIM // wk
    wb = hp * B_V_DIM
    qb0 = (A_HEADS + 2 * A_KV_HEADS) * HEAD_DIM // wb
    kb0, vb0 = qb0 + B_HEADS // hp, qb0 + 2 * B_HEADS // hp
    assert (A_HEADS * HEAD_DIM) % wk == 0 and ((A_HEADS + 2 * A_KV_HEADS) * HEAD_DIM) % wb == 0

    nq = length // tq_a
    wq = hp * A_GROUP * HEAD_DIM
    in_specs = [pl.BlockSpec((tq_a, wq), lambda b, h, i: (b * nq + i, h)),
                pl.BlockSpec((length, wk), lambda b, h, i: (b, ka0 + h)),
                pl.BlockSpec((length, wk), lambda b, h, i: (b, va0 + h))]
    args = [qkv, qkv, qkv]
    if has_cache:
        pc = caches[0].shape[0] // nb
        in_specs += [pl.BlockSpec((pc, wk), lambda b, h, i: (b, h))] * 2
        args += [caches[0], caches[1]]
    rb_a = min(256, A_GROUP * tq_a)
    oa = pl.pallas_call(
        functools.partial(_gqa_kernel, has_cache=has_cache, tk=tk, rb=rb_a),
        grid=(nb, A_KV_HEADS // hp, nq),
        in_specs=in_specs,
        out_specs=pl.BlockSpec((tq_a, wq), lambda b, h, i: (b * nq + i, h)),
        out_shape=jax.ShapeDtypeStruct((r, A_HEADS * HEAD_DIM), BF16),
        scratch_shapes=[pltpu.VMEM((A_GROUP * tq_a, lk), F32)],
        compiler_params=_cparams(3),
    )(*args)

    nq = length // tq_b
    in_specs = [pl.BlockSpec((tq_b, wb), lambda b, h, i: (b * nq + i, qb0 + h)),
                pl.BlockSpec((length, wb), lambda b, h, i: (b, kb0 + h)),
                pl.BlockSpec((length, wb), lambda b, h, i: (b, vb0 + h))]
    args = [qkv, qkv, qkv]
    if has_cache:
        in_specs += [pl.BlockSpec((pc, wb), lambda b, h, i: (b, h))] * 2
        args += [caches[2], caches[3]]
    in_specs += [pl.BlockSpec((4, HEAD_DIM), lambda b, h, i: (0, 0)),
                 pl.BlockSpec((1, B_V_DIM), lambda b, h, i: (0, 0))]
    args += [lam_p, subln_g]
    ob = pl.pallas_call(
        functools.partial(_diff_kernel, has_cache=has_cache, tk=tk, rb=min(128, tq_b), lam_init=lam_init),
        grid=(nb, B_HEADS // hp, nq),
        in_specs=in_specs,
        out_specs=pl.BlockSpec((tq_b, wb), lambda b, h, i: (b * nq + i, h)),
        out_shape=jax.ShapeDtypeStruct((r, B_HEADS * B_V_DIM), BF16),
        scratch_shapes=[pltpu.VMEM((tq_b, lk), F32)],
        compiler_params=_cparams(3),
    )(*args)
    return oa, ob


def _outproj_kernel(*refs, n_in):
    o_refs = refs[:n_in]
    w_refs = refs[n_in:2 * n_in]
    x_ref, g_ref, gate_ref, out_ref = refs[2 * n_in:]
    y = None
    for o_ref, w_ref in zip(o_refs, w_refs):
        t = jnp.dot(o_ref[...], w_ref[...], preferred_element_type=F32)
        y = t if y is None else y + t
    out_ref[...] = x_ref[...] + gate_ref[0] * _rms(y, g_ref[...])


def _outproj(os_, w, x, g, mod, row_fn, tm):
    r, d = x.shape
    n_in = len(os_)
    kw = os_[0].shape[1]
    assert all(o.shape[1] == kw for o in os_) and w.shape == (n_in * kw, d)
    in_specs = ([pl.BlockSpec((tm, kw), lambda i: (i, 0)) for _ in os_]
                + [pl.BlockSpec((kw, d), lambda i, k=k: (k, 0), pipeline_mode=pl.Buffered(1)) for k in range(n_in)]
                + [pl.BlockSpec((tm, d), lambda i: (i, 0)),
                   pl.BlockSpec((1, d), lambda i: (0, 0)),
                   _mod_spec(d, 2, row_fn)])
    return pl.pallas_call(
        functools.partial(_outproj_kernel, n_in=n_in),
        grid=(r // tm,),
        in_specs=in_specs,
        out_specs=pl.BlockSpec((tm, d), lambda i: (i, 0)),
        out_shape=jax.ShapeDtypeStruct((r, d), F32),
        compiler_params=_cparams(1),
    )(*os_, *([w] * n_in), x, g, mod)


def _mlp_kernel(x_ref, g2_ref, sh_ref, sc_ref, gate_ref, g3_ref, w1_ref, w2_ref, out_ref, n_scr):
    f = pl.program_id(1)

    @pl.when(f == 0)
    def _():
        n = _rms(x_ref[...], g2_ref[...]) * (1.0 + sc_ref[0]) + sh_ref[0]
        n_scr[...] = n.astype(BF16)
        out_ref[...] = jnp.zeros_like(out_ref)

    h = jnp.maximum(jnp.dot(n_scr[...], w1_ref[...], preferred_element_type=F32), 0.0)
    hb = (h * h).astype(BF16)
    d = out_ref.shape[1]
    for c0 in range(0, d, MLP_ACC_CHUNK):
        out_ref[:, c0:c0 + MLP_ACC_CHUNK] += jnp.dot(hb, w2_ref[:, c0:c0 + MLP_ACC_CHUNK],
                                                     preferred_element_type=F32)

    @pl.when(f == pl.num_programs(1) - 1)
    def _():
        out_ref[...] = x_ref[...] + gate_ref[0] * _rms(out_ref[...], g3_ref[...])


def _mlp(x, g2, g3, mod, row_fn, w1, w2, layer, tm, tf):
    r, d = x.shape
    dff = w1.shape[2]
    return pl.pallas_call(
        _mlp_kernel,
        grid=(r // tm, dff // tf),
        in_specs=[pl.BlockSpec((tm, d), lambda i, f: (i, 0)),
                  pl.BlockSpec((1, d), lambda i, f: (0, 0)),
                  _mod_spec(d, 3, row_fn), _mod_spec(d, 4, row_fn), _mod_spec(d, 5, row_fn),
                  pl.BlockSpec((1, d), lambda i, f: (0, 0)),
                  pl.BlockSpec((None, d, tf), lambda i, f: (layer, 0, f)),
                  pl.BlockSpec((None, tf, d), lambda i, f: (layer, f, 0))],
        out_specs=pl.BlockSpec((tm, d), lambda i, f: (i, 0)),
        out_shape=jax.ShapeDtypeStruct((r, d), F32),
        scratch_shapes=[pltpu.VMEM((tm, d), BF16)],
        compiler_params=_cparams(2),
    )(x, g2, mod, mod, mod, g3, w1, w2)


def _ssm_in_kernel(x_ref, g_ref, sh_ref, sc_ref, w_ref, u_ref):
    n = (_rms(x_ref[...], g_ref[...]) * (1.0 + sc_ref[0]) + sh_ref[0]).astype(BF16)
    for c0 in range(0, w_ref.shape[1], PROJ_CHUNK):
        u_ref[:, c0:c0 + PROJ_CHUNK] = jnp.dot(n, w_ref[:, c0:c0 + PROJ_CHUNK], preferred_element_type=F32)


def _ssm_in(x, g, mod, row_fn, w, tm):
    r, d = x.shape
    n = w.shape[1]
    return pl.pallas_call(
        _ssm_in_kernel,
        grid=(r // tm,),
        in_specs=[pl.BlockSpec((tm, d), lambda i: (i, 0)),
                  pl.BlockSpec((1, d), lambda i: (0, 0)),
                  _mod_spec(d, 0, row_fn), _mod_spec(d, 1, row_fn),
                  pl.BlockSpec((d, n), lambda i: (0, 0), pipeline_mode=pl.Buffered(1))],
        out_specs=pl.BlockSpec((tm, n), lambda i: (i, 0)),
        out_shape=jax.ShapeDtypeStruct((r, n), F32),
        compiler_params=_cparams(1),
    )(x, g, mod, mod, w)


def _ssm_params(a_re, a_im, log_dt, b_re, b_im, c_re, c_im, seg_chunks):
    g = a_re.shape[1]
    vec = lambda x: x.astype(F32).reshape(2, g, 1, -1)
    args = (vec(log_dt), vec(a_re), vec(a_im)) + tuple(x.astype(F32) for x in (b_re, b_im, c_re, c_im))
    pg = SSM_PARAM_GROUPS
    per_group = lambda a: pl.BlockSpec((2, pg) + a.shape[2:], lambda i: (0, i, 0, 0))
    blk = lambda a: pl.BlockSpec((pg,) + a.shape[1:], lambda i: (i, 0, 0))
    out_shape = [jax.ShapeDtypeStruct((g, SSM_KDIM, SSM_LANES), BF16),
                 jax.ShapeDtypeStruct((g, SSM_KDIM, SSM_KDIM), BF16),
                 jax.ShapeDtypeStruct((g, SSM_KDIM, SSM_LANES), BF16),
                 jax.ShapeDtypeStruct((g, 1, SSM_LANES), F32),
                 jax.ShapeDtypeStruct((g, 1, SSM_LANES), F32)]
    return pl.pallas_call(
        functools.partial(_ssm_params_kernel, seg_chunks=seg_chunks),
        grid=(g // pg,),
        in_specs=[per_group(a) for a in args],
        out_specs=[blk(o) for o in out_shape],
        out_shape=out_shape,
        compiler_params=_cparams(1),
    )(*args)


def _ssm_params_kernel(*refs, seg_chunks):
    for k in range(SSM_PARAM_GROUPS):
        _ssm_params_group(k, *refs, seg_chunks=seg_chunks)


def _ssm_params_group(k, ldt_ref, are_ref, aim_ref, br_ref, bi_ref, cr_ref, ci_ref,
                      win_ref, toep_ref, cout_ref, achunk_ref, aseg_ref, *, seg_chunks):
    t, nch, half = SSM_CHUNK, S5_GROUP_CH, 2 * S5_STATE
    hi = lax.Precision.HIGHEST
    both = lambda ref: jnp.concatenate([ref[0, k], ref[1, k]], axis=1)
    eye = (lax.broadcasted_iota(jnp.int32, (S5_STATE, S5_STATE), 0)
           == lax.broadcasted_iota(jnp.int32, (S5_STATE, S5_STATE), 1)).astype(F32)
    tr = lambda m: lax.dot_general(m, eye, (((0,), (0,)), ((), ())), precision=hi, preferred_element_type=F32)
    both_t = lambda ref: jnp.concatenate([tr(ref[0, k]), tr(ref[1, k])], axis=1)
    ar, ai = both(are_ref), both(aim_ref)
    dt = jnp.exp(jnp.concatenate([jnp.broadcast_to(ldt_ref[d, k], (1, S5_STATE)) for d in range(2)], axis=1))
    lr, li = ar * dt, ai * dt

    def power(k):
        mag = jnp.exp(lr * k)
        return mag * jnp.cos(li * k), mag * jnp.sin(li * k)

    pw = [(jnp.ones_like(lr), jnp.zeros_like(lr)), power(1.0)]
    for _ in range(t - 1):
        pw.append((pw[-1][0] * pw[1][0] - pw[-1][1] * pw[1][1], pw[-1][0] * pw[1][1] + pw[-1][1] * pw[1][0]))
    fwd1 = lax.broadcasted_iota(jnp.int32, (1, half), 1) < S5_STATE
    pick = lambda kf, kb: (jnp.where(fwd1, pw[kf][0], pw[kb][0]), jnp.where(fwd1, pw[kf][1], pw[kb][1]))
    xr, xi = pw[1][0] - 1.0, pw[1][1]
    den = ar * ar + ai * ai
    qr, qi = (xr * ar + xi * ai) / den, (xi * ar - xr * ai) / den
    btr, bti = both_t(br_ref), both_t(bi_ref)
    bbr = qr * btr - qi * bti
    bbi = qr * bti + qi * btr
    ccr, cci = both(cr_ref), both(ci_ref)

    for s in range(t):
        rows = slice(s * nch, (s + 1) * nch)
        er, ei = pick(t - 1 - s, s)
        win_ref[k, rows, 0:half] = (er * bbr - ei * bbi).astype(BF16)
        win_ref[k, rows, half:] = (er * bbi + ei * bbr).astype(BF16)
        er, ei = pick(s + 1, t - s)
        cout_ref[k, rows, 0:half] = (ccr * er - cci * ei).astype(BF16)
        cout_ref[k, rows, half:] = (-(ccr * ei + cci * er)).astype(BF16)

    fwd = lax.broadcasted_iota(jnp.int32, (nch, half), 1) < S5_STATE
    cp = [(ccr * pw[k][0] - cci * pw[k][1], ccr * pw[k][1] + cci * pw[k][0]) for k in range(t)]
    stack = lambda ks, j: jnp.concatenate([cp[k][j] for k in ks], axis=0)
    nt = lambda x, y: lax.dot_general(x, y, (((1,), (1,)), ((), ())), precision=hi, preferred_element_type=F32)
    zero = jnp.zeros_like(bbr)
    order, rev = list(range(t)), list(range(t - 1, -1, -1))
    taps_f = (nt(jnp.where(fwd, bbr, zero), stack(order, 0)) - nt(jnp.where(fwd, bbi, zero), stack(order, 1)))
    taps_b = (nt(jnp.where(fwd, zero, bbr), stack(rev, 0)) - nt(jnp.where(fwd, zero, bbi), stack(rev, 1)))
    lane = lax.broadcasted_iota(jnp.int32, (nch, SSM_KDIM), 1)
    for s in range(t):
        tf = taps_f if s == 0 else pltpu.roll(taps_f, s * nch, axis=1)
        tb = taps_b if s == t - 1 else pltpu.roll(taps_b, (s + 1) * nch, axis=1)
        blk = jnp.where(lane >= s * nch, tf, 0.0) + jnp.where(lane < (s + 1) * nch, tb, 0.0)
        toep_ref[k, s * nch:(s + 1) * nch, :] = blk.astype(BF16)

    for ref, (dr, di) in ((achunk_ref, pw[t]), (aseg_ref, power(float(t * seg_chunks)))):
        ref[k, :, 0:half] = dr
        ref[k, :, half:] = di


def _cmul_add(ar, ai, h_re, h_im, s_re, s_im):
    return ar * h_re - ai * h_im + s_re, ar * h_im + ai * h_re + s_im


def _regroup_perms():
    n = SSM_CHUNK * V7X_LANES
    idx = jnp.arange(n)
    tok, grp, ch = idx // V7X_LANES, idx % V7X_LANES // S5_GROUP_CH, idx % S5_GROUP_CH
    dst = grp * SSM_KDIM + tok * S5_GROUP_CH + ch
    p_in = (dst[:, None] == idx[None, :]).astype(BF16)
    return p_in, p_in.T


def _regroup_in(u_ref, pin_ref, lhs_scr, nrows):
    t, rbk = SSM_CHUNK, min(SSM_REGROUP_ROWS, nrows)

    def body(rb, carry):
        pieces = [u_ref[pl.ds(rb * (rbk * t) + tt, rbk, stride=t), :].astype(BF16) for tt in range(t)]
        out = jnp.dot(jnp.concatenate(pieces, axis=1), pin_ref[...], preferred_element_type=F32)
        rows = pl.ds(pl.multiple_of(rb * rbk, rbk), rbk)
        for gl in range(SSM_TILE_GROUPS):
            lhs_scr[gl, rows, :] = out[:, gl * SSM_KDIM:(gl + 1) * SSM_KDIM].astype(BF16)
        return carry

    lax.fori_loop(0, nrows // rbk, body, 0)


def _regroup_out(y_scr, pout_ref, y_ref, nrows):
    t, rbk = SSM_CHUNK, min(SSM_REGROUP_ROWS, nrows)

    def body(rb, carry):
        rows = pl.ds(pl.multiple_of(rb * rbk, rbk), rbk)
        rest = jnp.concatenate([y_scr[gl, rows, :] for gl in range(SSM_TILE_GROUPS)], axis=1)
        out = None
        for _ in range(SSM_Y_TERMS):
            term = rest.astype(BF16)
            rest = rest - term.astype(F32)
            part = jnp.dot(term, pout_ref[...], preferred_element_type=F32)
            out = part if out is None else out + part
        for tt in range(t):
            y_ref[pl.ds(rb * (rbk * t) + tt, rbk, stride=t), :] = out[:, tt * V7X_LANES:(tt + 1) * V7X_LANES]
        return carry

    lax.fori_loop(0, nrows // rbk, body, 0)


def _ssm_core_kernel(u_ref, pin_ref, pout_ref, win_ref, toep_ref, cout_ref, a_ref, aseg_ref, h0_ref, y_ref, hfin_ref,
                     lhs_scr, y_scr, s_re, s_im, tf_re, tb_re, tf_im, tb_im, *, nseq, nc, nsub):
    step = pl.program_id(1)
    nrows = nseq * nc

    @pl.when(step == 0)
    def _():
        _regroup_in(u_ref, pin_ref, lhs_scr, nrows)

    for k in range(SSM_GROUPS_PER_STEP):
        _ssm_group(k, step * SSM_GROUPS_PER_STEP + k, win_ref, toep_ref, cout_ref, a_ref, aseg_ref, h0_ref, hfin_ref,
                   lhs_scr, y_scr, s_re, s_im, tf_re, tb_re, tf_im, tb_im, nseq, nc, nsub)

    @pl.when(step == pl.num_programs(1) - 1)
    def _():
        _regroup_out(y_scr, pout_ref, y_ref, nrows)


def _ssm_group(k, g, win_ref, toep_ref, cout_ref, a_ref, aseg_ref, h0_ref, hfin_ref,
               lhs_scr, y_scr, s_re, s_im, tf_re, tb_re, tf_im, tb_im, nseq, nc, nsub):
    half = SSM_LANES // 2
    pitch = nc + SSM_PITCH_PAD
    nrows = nseq * nc
    nslab = nseq // V7X_SUBLANES
    base = k * nseq * pitch

    lhs = lhs_scr[g]
    s = jnp.dot(lhs, win_ref[k], preferred_element_type=F32)
    for p in range(nseq):
        s_re[base + p * pitch:base + p * pitch + nc, :] = s[p * nc:(p + 1) * nc, 0:half]
        s_im[base + p * pitch:base + p * pitch + nc, :] = s[p * nc:(p + 1) * nc, half:]

    ar, ai = a_ref[k][:, 0:half], a_ref[k][:, half:]
    fwd = lax.broadcasted_iota(jnp.int32, (V7X_SUBLANES, half), 1) < S5_STATE

    def slab(q, c):
        return pl.ds(base + q * V7X_SUBLANES * pitch + c, V7X_SUBLANES, stride=pitch)

    def scan(init, store):
        hs = list(init)
        for c in range(nc):
            cb = nc - 1 - c
            for q in range(nslab):
                h_re, h_im = hs[q]
                x_re = jnp.where(fwd, s_re[slab(q, c), :], s_re[slab(q, cb), :])
                x_im = jnp.where(fwd, s_im[slab(q, c), :], s_im[slab(q, cb), :])
                if store:
                    tf_re[slab(q, c), :] = h_re
                    tb_re[slab(q, cb), :] = h_re
                    tf_im[slab(q, c), :] = h_im
                    tb_im[slab(q, cb), :] = h_im
                hs[q] = _cmul_add(ar, ai, h_re, h_im, x_re, x_im)
        return hs

    rows8 = lambda q: slice(q * V7X_SUBLANES, (q + 1) * V7X_SUBLANES)
    init = [(h0_ref[k, rows8(q), 0:half], h0_ref[k, rows8(q), half:]) for q in range(nslab)]
    if nsub > 1:
        zero = jnp.zeros((V7X_SUBLANES, half), F32)
        ends = scan([(zero, zero)] * nslab, False)
        gr, gi = aseg_ref[k][:, 0:half], aseg_ref[k][:, half:]
        seg = lax.broadcasted_iota(jnp.int32, (V7X_SUBLANES, half), 0) % nsub
        carried = jnp.where(fwd, seg, nsub - 1 - seg) != 0
        shift = lambda x: jnp.where(fwd, pltpu.roll(x, 1, axis=0), pltpu.roll(x, V7X_SUBLANES - 1, axis=0))
        for q in range(nslab):
            (h0_re, h0_im), (f_re, f_im) = init[q], ends[q]
            e_re, e_im = h0_re, h0_im
            for _ in range(nsub - 1):
                x_re, x_im = _cmul_add(gr, gi, e_re, e_im, f_re, f_im)
                e_re = h0_re + jnp.where(carried, shift(x_re), 0.0)
                e_im = h0_im + jnp.where(carried, shift(x_im), 0.0)
            init[q] = (e_re, e_im)
    hs = scan(init, True)
    for q in range(nslab):
        hfin_ref[k, rows8(q), 0:half] = hs[q][0]
        hfin_ref[k, rows8(q), half:] = hs[q][1]

    gather = lambda ref: jnp.concatenate([ref[base + p * pitch:base + p * pitch + nc, :] for p in range(nseq)],
                                         axis=0)
    fwd_rows = lax.broadcasted_iota(jnp.int32, (nrows, half), 1) < S5_STATE
    hp = jnp.concatenate([jnp.where(fwd_rows, gather(tf_re), gather(tb_re)),
                          jnp.where(fwd_rows, gather(tf_im), gather(tb_im))], axis=1).astype(BF16)
    y_scr[g] = (jnp.dot(lhs, toep_ref[k], preferred_element_type=F32)
                + lax.dot_general(hp, cout_ref[k], (((1,), (1,)), ((), ())), preferred_element_type=F32))


def _ssm_core(u, ops, h0, nseq, nc, nsub):
    r, d = u.shape
    nrows = nseq * nc
    assert r == nrows * SSM_CHUNK and nseq % V7X_SUBLANES == 0 and V7X_SUBLANES % nsub == 0
    assert nrows % min(SSM_REGROUP_ROWS, nrows) == 0 and nc % V7X_SUBLANES == 0
    ntile = d // V7X_LANES
    gps = SSM_GROUPS_PER_STEP
    steps = SSM_TILE_GROUPS // gps
    grp = lambda j, g: (j * steps + g, 0, 0)
    scan_rows = gps * nseq * (nc + SSM_PITCH_PAD)
    p_in, p_out = _regroup_perms()
    perm_spec = pl.BlockSpec(p_in.shape, lambda j, g: (0, 0), pipeline_mode=pl.Buffered(1))
    return pl.pallas_call(
        functools.partial(_ssm_core_kernel, nseq=nseq, nc=nc, nsub=nsub),
        grid=(ntile, steps),
        in_specs=[pl.BlockSpec((r, V7X_LANES), lambda j, g: (0, j)),
                  perm_spec, perm_spec]
                 + [pl.BlockSpec((gps,) + o.shape[1:], grp) for o in ops]
                 + [pl.BlockSpec((gps, nseq, SSM_LANES), grp)],
        out_specs=[pl.BlockSpec((r, V7X_LANES), lambda j, g: (0, j)),
                   pl.BlockSpec((gps, nseq, SSM_LANES), grp)],
        out_shape=[jax.ShapeDtypeStruct((r, d), F32),
                   jax.ShapeDtypeStruct((d // S5_GROUP_CH, nseq, SSM_LANES), F32)],
        scratch_shapes=[pltpu.VMEM((SSM_TILE_GROUPS, nrows, SSM_KDIM), BF16),
                        pltpu.VMEM((SSM_TILE_GROUPS, nrows, SSM_KDIM), F32)]
                       + [pltpu.VMEM((scan_rows, V7X_LANES), F32)] * 6,
        compiler_params=_cparams(2),
    )(u, p_in, p_out, *ops, h0)


def _glu_kernel(y_ref, u_ref, d_ref, w_ref, z_ref):
    v = y_ref[...] + d_ref[...] * u_ref[...]
    z = 0.5 * v * (1.0 + jnp.tanh(math.sqrt(2.0 / math.pi) * (v + 0.044715 * (v * v * v))))
    zb = z.astype(BF16)
    for c0 in range(0, w_ref.shape[1], PROJ_CHUNK):
        cols = slice(c0, c0 + PROJ_CHUNK)
        gl = jnp.dot(zb, w_ref[:, cols], preferred_element_type=F32)
        z_ref[:, cols] = (z[:, cols] * _sigmoid(gl)).astype(BF16)


def _glu(y, u, d, w, tm):
    r, dm = y.shape
    return pl.pallas_call(
        _glu_kernel,
        grid=(r // tm,),
        in_specs=[pl.BlockSpec((tm, dm), lambda i: (i, 0)),
                  pl.BlockSpec((tm, dm), lambda i: (i, 0)),
                  pl.BlockSpec((1, dm), lambda i: (0, 0)),
                  pl.BlockSpec(w.shape, lambda i: (0, 0), pipeline_mode=pl.Buffered(1))],
        out_specs=pl.BlockSpec((tm, dm), lambda i: (i, 0)),
        out_shape=jax.ShapeDtypeStruct((r, dm), BF16),
        compiler_params=_cparams(1),
    )(y, u, d, w)


def _rope_tables(length):
    rows = length // GRID_W
    row = jnp.repeat(jnp.arange(rows, dtype=F32), GRID_W)
    col = jnp.tile(jnp.arange(GRID_W, dtype=F32), rows)
    inv = ROPE_BASE ** (-jnp.arange(ROPE_PAIRS_PER_AXIS, dtype=F32) / ROPE_PAIRS_PER_AXIS)
    ang = jnp.concatenate([row[:, None] * inv, col[:, None] * inv], axis=-1)
    cos, sin = jnp.cos(ang), jnp.sin(ang)
    return jnp.concatenate([cos, cos], axis=-1), jnp.concatenate([-sin, sin], axis=-1)


def kernel(x_prompt, x_sample, c, cache_a_k, cache_a_v, cache_b_k, cache_b_v, state_ssm, c_ctx, ada_w, ada_b, norm_g, mlp_w1, mlp_w2, attn_w_in, attn_w_out, attn_qk_norm, diff_lambda, diff_subln, ssm_w_in, ssm_a_re, ssm_a_im, ssm_log_dt, ssm_b_re, ssm_b_im, ssm_c_re, ssm_c_im, ssm_d, ssm_glu_w, ssm_w_out):
    bp, lp, d = x_prompt.shape
    bs, ls, _ = x_sample.shape
    depth = ada_w.shape[0]
    past = cache_a_k.shape[2]
    nsub = SSM_SEGMENTS
    assert bs + 1 <= 8 and bp % V7X_SUBLANES == 0
    assert lp % SSM_CHUNK == 0 and ls % (SSM_CHUNK * nsub) == 0 and ls % GRID_W == 0

    cond8 = jnp.zeros((8, d), F32).at[0].set(c_ctx).at[1:1 + bs].set(c)
    mods = _modulation(cond8, ada_w, ada_b).reshape(depth, 8 * N_MOD, 1, d)

    xp = x_prompt.reshape(bp * lp, d)
    xs = x_sample.reshape(bs * ls, d)
    tm = min(512, bp * lp, ls)
    tm_big = min(1024, bp * lp, ls)
    row_p = lambda i: 0
    row_s = lambda i: 1 + i // (ls // tm)
    row_s_big = lambda i: 1 + i // (ls // tm_big)
    w1_all = mlp_w1.astype(BF16)
    w2_all = mlp_w2.astype(BF16)
    new_ak = new_av = new_bk = new_bv = new_ssm = None
    for l in range(depth):
        mod = mods[l]
        g = norm_g[l][:, None, :]
        i = l // 2
        if l % 2 == 0:
            lam_init = 0.8 - 0.6 * math.exp(-0.3 * l)
            w_in = attn_w_in[i].astype(BF16)
            w_out = attn_w_out[i].astype(BF16)
            qkv, new_ak, new_av, new_bk, new_bv = _qkv_proj(xp, g[0], mod, row_p, w_in, attn_qk_norm[i], None, lp, tm)
            oa, ob = _attention(qkv, bp, None, diff_lambda[i], diff_subln[i][None, :], lam_init,
                                tq_a=lp, tq_b=lp, tk=512, hp=2)
            xp = _outproj([oa, ob], w_out, xp, g[1], mod, row_p, tm)
            (qkv,) = _qkv_proj(xs, g[0], mod, row_s, w_in, attn_qk_norm[i], _rope_tables(ls), 0, tm)
            caches = (cache_a_k[:, i].reshape(bs * past, A_KV_HEADS * HEAD_DIM),
                      cache_a_v[:, i].reshape(bs * past, A_KV_HEADS * HEAD_DIM),
                      cache_b_k[:, i].reshape(bs * past, B_HEADS * 2 * HEAD_DIM),
                      cache_b_v[:, i].reshape(bs * past, B_HEADS * B_V_DIM))
            oa, ob = _attention(qkv, bs, caches, diff_lambda[i], diff_subln[i][None, :], lam_init,
                                tq_a=min(512, ls), tq_b=min(1024, ls), tk=512, hp=1)
            xs = _outproj([oa, ob], w_out, xs, g[1], mod, row_s, tm)
        else:
            w_in = ssm_w_in[i].astype(BF16)
            u_p = _ssm_in(xp, g[0], mod, row_p, w_in, tm_big)
            u_s = _ssm_in(xs, g[0], mod, row_s_big, w_in, tm_big)
            ng = d // S5_GROUP_CH
            nc_p, nc_s = lp // SSM_CHUNK, ls // SSM_CHUNK // nsub
            ops = _ssm_params(ssm_a_re[i], ssm_a_im[i], ssm_log_dt[i], ssm_b_re[i], ssm_b_im[i],
                                                 ssm_c_re[i], ssm_c_im[i], nc_s)
            st = state_ssm[:, i].astype(F32)
            h0 = jnp.zeros((ng, bs, nsub, 4, S5_STATE), F32)
            h0 = h0.at[:, :, 0, 0].set(jnp.transpose(st[:, 0, :, :, 0], (1, 0, 2)))
            h0 = h0.at[:, :, 0, 2].set(jnp.transpose(st[:, 0, :, :, 1], (1, 0, 2)))
            h0 = h0.at[:, :, nsub - 1, 1].set(jnp.transpose(st[:, 1, :, :, 0], (1, 0, 2)))
            h0 = h0.at[:, :, nsub - 1, 3].set(jnp.transpose(st[:, 1, :, :, 1], (1, 0, 2)))
            h0 = h0.reshape(ng, bs * nsub, SSM_LANES)
            y_p, hfin = _ssm_core(u_p, ops, jnp.zeros((ng, bp, SSM_LANES), F32), bp, nc_p, 1)
            y_s, _ = _ssm_core(u_s, ops, h0, bs * nsub, nc_s, nsub)
            new_ssm = jnp.transpose(hfin.reshape(ng, bp, 2, 2, S5_STATE), (1, 3, 0, 4, 2))
            glu_w = ssm_glu_w[i].astype(BF16)
            w_out = ssm_w_out[i].astype(BF16)
            dvec = ssm_d[i][None, :]
            z_p = _glu(y_p, u_p, dvec, glu_w, tm)
            z_s = _glu(y_s, u_s, dvec, glu_w, tm)
            xp = _outproj([z_p], w_out, xp, g[1], mod, row_p, tm)
            xs = _outproj([z_s], w_out, xs, g[1], mod, row_s, tm)
        xp = _mlp(xp, g[2], g[3], mod, row_p, w1_all, w2_all, l, tm_big, 512)
        xs = _mlp(xs, g[2], g[3], mod, row_s_big, w1_all, w2_all, l, tm_big, 512)
    return (xp.reshape(bp, lp, d), xs.reshape(bs, ls, d), new_ak, new_av, new_bk, new_bv,
            new_ssm[:, None])
```

```python
import functools
import math

import jax
import jax.numpy as jnp
from jax import lax
from jax.experimental import pallas as pl
from jax.experimental.pallas import tpu as pltpu

F32 = jnp.float32
BF16 = jnp.bfloat16
EPS = 1e-6

HEAD_DIM = 128
A_HEADS = 8
A_KV_HEADS = 2
A_GROUP = A_HEADS // A_KV_HEADS
B_HEADS = 4
B_V_DIM = 2 * HEAD_DIM
GRID_W = 64
ROPE_BASE = 10000.0
ROPE_PAIRS_PER_AXIS = HEAD_DIM // 4
N_MOD = 6
S5_GROUP_CH = 16
S5_STATE = 64

V7X_LANES = 128
V7X_SUBLANES = 8
V7X_VMEM_LIMIT_BYTES = 56 * 1024 * 1024

SSM_CHUNK = V7X_SUBLANES
SSM_TILE_GROUPS = V7X_LANES // S5_GROUP_CH
SSM_KDIM = SSM_CHUNK * S5_GROUP_CH
SSM_LANES = 4 * S5_STATE
SSM_GROUPS_PER_STEP = 2
SSM_PARAM_GROUPS = 4
SSM_SEGMENTS = 8
SSM_REGROUP_ROWS = 256
SSM_Y_TERMS = 2
SSM_PITCH_PAD = 8

MLP_ACC_CHUNK = 512
QKV_CHUNK = 256
PROJ_CHUNK = 512


def _cparams(n_axes):
    return pltpu.CompilerParams(dimension_semantics=("arbitrary",) * n_axes,
                                vmem_limit_bytes=V7X_VMEM_LIMIT_BYTES)


def _sigmoid(x):
    return 1.0 / (1.0 + jnp.exp(-x))


def _rms(x, g):
    return x * lax.rsqrt(jnp.mean(x * x, axis=-1, keepdims=True) + EPS) * g


def _modulate(x, g, scale, shift):
    return _rms(x, g * (1.0 + scale)) + shift


def _gated_rms(x, g, gate):
    return _rms(x, g * gate)


def _mod_spec(d, slot, row_fn):
    return pl.BlockSpec((1, 1, d), lambda i, *_: (row_fn(i) * N_MOD + slot, 0, 0))


def _mod_kernel(c_ref, w_ref, b_ref, o_ref):
    c = c_ref[...]
    s = (c * _sigmoid(c)).astype(BF16)
    o_ref[0] = jnp.dot(s, w_ref[0].astype(BF16), preferred_element_type=F32) + b_ref[0]


def _modulation(cond8, ada_w, ada_b):
    depth, d, n = ada_w.shape
    tn = 1024
    return pl.pallas_call(
        _mod_kernel,
        grid=(depth, n // tn),
        in_specs=[pl.BlockSpec((8, d), lambda l, j: (0, 0)),
                  pl.BlockSpec((1, d, tn), lambda l, j: (l, 0, j)),
                  pl.BlockSpec((1, 1, tn), lambda l, j: (l, 0, j))],
        out_specs=pl.BlockSpec((1, 8, tn), lambda l, j: (l, 0, j)),
        out_shape=jax.ShapeDtypeStruct((depth, 8, n), F32),
        compiler_params=_cparams(2),
    )(cond8, ada_w, ada_b.reshape(depth, 1, n))


def _split_heads(y, fn):
    return jnp.concatenate([fn(y[:, h * HEAD_DIM:(h + 1) * HEAD_DIM]) for h in range(y.shape[1] // HEAD_DIM)], axis=1)


def _qkv_kernel(*refs, rope, caches):
    x_ref, g_ref, sh_ref, sc_ref, w_ref, qkg_ref = refs[:6]
    refs = refs[6:]
    if rope:
        cos_ref, sin_ref = refs[:2]
        refs = refs[2:]
    qkv_ref = refs[0]
    if caches:
        ka_ref, va_ref, kb_ref, vb_ref = refs[1:5]
    n = _modulate(x_ref[...], g_ref[...], sc_ref[0], sh_ref[0]).astype(BF16)

    def rot(y):
        if not rope:
            return y
        return y * cos_ref[...] + pltpu.roll(y, HEAD_DIM // 2, axis=1) * sin_ref[...]

    q_scale = HEAD_DIM ** -0.5 * math.log2(math.e)
    qa_end = A_HEADS * HEAD_DIM
    ka_end = qa_end + QKV_CHUNK
    va_end = ka_end + QKV_CHUNK
    qb_end = va_end + B_HEADS * 2 * HEAD_DIM
    kb_end = qb_end + B_HEADS * 2 * HEAD_DIM
    for c0 in range(0, w_ref.shape[1], QKV_CHUNK):
        cols = slice(c0, c0 + QKV_CHUNK)
        y = jnp.dot(n, w_ref[:, cols], preferred_element_type=F32)
        if c0 < qa_end:
            y = _split_heads(y, lambda t: rot(_rms(t, qkg_ref[0:1, :])) * q_scale)
        elif c0 < ka_end:
            y = _split_heads(y, lambda t: _rms(t, qkg_ref[1:2, :]))
            if caches:
                _store_cache(ka_ref, (), y, HEAD_DIM)
            y = _split_heads(y, rot)
        elif c0 < va_end:
            if caches:
                _store_cache(va_ref, (), y, HEAD_DIM)
        elif c0 < qb_end:
            y = _split_heads(y, lambda t: rot(t) * q_scale)
        elif c0 < kb_end:
            if caches:
                _store_cache(kb_ref, ((c0 - qb_end) // QKV_CHUNK,), y, HEAD_DIM)
            y = _split_heads(y, rot)
        elif caches:
            _store_cache(vb_ref, (), y, B_V_DIM, (c0 - kb_end) // QKV_CHUNK)
        qkv_ref[:, cols] = y.astype(BF16)


def _store_cache(ref, lead, y, width, only=None):
    length = ref.shape[2]
    for b in range(ref.shape[0]):
        rows = slice(b * length, (b + 1) * length)
        for j in range(y.shape[1] // width):
            idx = (b, 0, slice(None)) + tuple(lead) + (j if only is None else only, slice(None))
            ref[idx] = y[rows, j * width:(j + 1) * width]


def _qkv_proj(x, g, mod, row_fn, w, qk_g, rope_tabs, caches, tm):
    r, d = x.shape
    n = w.shape[1]
    assert A_KV_HEADS * HEAD_DIM == QKV_CHUNK and r % tm == 0 and n % QKV_CHUNK == 0
    assert not (caches and rope_tabs is not None)
    row = lambda i: (i, 0)
    in_specs = [pl.BlockSpec((tm, d), row),
                pl.BlockSpec((1, d), lambda i: (0, 0)),
                _mod_spec(d, 0, row_fn), _mod_spec(d, 1, row_fn),
                pl.BlockSpec((d, n), lambda i: (0, 0), pipeline_mode=pl.Buffered(1)),
                pl.BlockSpec((2, HEAD_DIM), lambda i: (0, 0))]
    args = [x, g, mod, mod, w, qk_g]
    if rope_tabs is not None:
        nblk = rope_tabs[0].shape[0] // tm
        in_specs += [pl.BlockSpec((tm, HEAD_DIM), lambda i: (i % nblk, 0))] * 2
        args += list(rope_tabs)
    out_specs = [pl.BlockSpec((tm, n), row)]
    out_shape = [jax.ShapeDtypeStruct((r, n), BF16)]
    if caches:
        assert tm % caches == 0
        for tail in ((A_KV_HEADS, HEAD_DIM), (A_KV_HEADS, HEAD_DIM), (B_HEADS, 2, HEAD_DIM), (B_HEADS, B_V_DIM)):
            out_specs.append(pl.BlockSpec((tm // caches, 1, caches) + tail, lambda i, z=(0,) * (2 + len(tail)): (i,) + z))
            out_shape.append(jax.ShapeDtypeStruct((r // caches, 1, caches) + tail, F32))
    return pl.pallas_call(
        functools.partial(_qkv_kernel, rope=rope_tabs is not None, caches=caches),
        grid=(r // tm,),
        in_specs=in_specs, out_specs=out_specs, out_shape=out_shape,
        compiler_params=_cparams(1),
    )(*args)


def _lane_tiles(x, op, acc):
    for j in range(0, x.shape[1], V7X_LANES):
        t = x[:, j:j + V7X_LANES]
        acc = t if acc is None else op(acc, t)
    return acc


def _softmax_pv(q, k_chunks, v_chunks, s_scr, rb):
    m_rows = q.shape[0]
    outs = []
    for r0 in range(0, m_rows, rb):
        qr = q[r0:r0 + rb]
        m_part = None
        off = 0
        for kc in k_chunks:
            k = kc()
            s = lax.dot_general(qr, k, (((1,), (1,)), ((), ())), preferred_element_type=F32)
            s_scr[r0:r0 + rb, off:off + k.shape[0]] = s
            m_part = _lane_tiles(s, jnp.maximum, m_part)
            off += k.shape[0]
        m = jnp.max(m_part, axis=-1, keepdims=True)
        l_part = acc = None
        off = 0
        for vc in v_chunks:
            v = vc()
            p = jnp.exp2(s_scr[r0:r0 + rb, off:off + v.shape[0]] - m)
            l_part = _lane_tiles(p, jnp.add, l_part)
            pv = jnp.dot(p.astype(BF16), v, preferred_element_type=F32)
            acc = pv if acc is None else acc + pv
            off += v.shape[0]
        outs.append(acc / jnp.sum(l_part, axis=-1, keepdims=True))
    return jnp.concatenate(outs, axis=0)


def _chunks(ref, c0, width, tk, cache_ref=None):
    length = ref.shape[0]
    out = [(lambda s=s: ref[s:s + tk, c0:c0 + width]) for s in range(0, length, tk)]
    if cache_ref is not None:
        out.append(lambda: cache_ref[:, c0:c0 + width].astype(BF16))
    return out


def _gqa_kernel(*refs, has_cache, tk, rb):
    if has_cache:
        q_ref, k_ref, v_ref, ck_ref, cv_ref, o_ref, s_scr = refs
    else:
        q_ref, k_ref, v_ref, o_ref, s_scr = refs
        ck_ref = cv_ref = None
    tq = q_ref.shape[0]
    wq = A_GROUP * HEAD_DIM
    for hh in range(k_ref.shape[1] // HEAD_DIM):
        q = jnp.concatenate([q_ref[:, hh * wq + g * HEAD_DIM:hh * wq + (g + 1) * HEAD_DIM] for g in range(A_GROUP)],
                            axis=0)
        o = _softmax_pv(q, _chunks(k_ref, hh * HEAD_DIM, HEAD_DIM, tk, ck_ref),
                        _chunks(v_ref, hh * HEAD_DIM, HEAD_DIM, tk, cv_ref), s_scr, rb)
        o_ref[:, hh * wq:(hh + 1) * wq] = jnp.concatenate([o[g * tq:(g + 1) * tq] for g in range(A_GROUP)],
                                                          axis=1).astype(BF16)


def _diff_kernel(*refs, has_cache, tk, rb, lam_init):
    if has_cache:
        q_ref, k_ref, v_ref, ck_ref, cv_ref, lam_ref, sg_ref, o_ref, s_scr = refs
    else:
        q_ref, k_ref, v_ref, lam_ref, sg_ref, o_ref, s_scr = refs
        ck_ref = cv_ref = None
    lp = lam_ref[...]
    lam = (jnp.exp(jnp.sum(lp[0:1] * lp[1:2], axis=-1, keepdims=True))
           - jnp.exp(jnp.sum(lp[2:3] * lp[3:4], axis=-1, keepdims=True)) + lam_init)
    for hh in range(v_ref.shape[1] // B_V_DIM):
        c0 = hh * B_V_DIM
        outs = []
        for m in range(2):
            cm = c0 + m * HEAD_DIM
            outs.append(_softmax_pv(q_ref[:, cm:cm + HEAD_DIM], _chunks(k_ref, cm, HEAD_DIM, tk, ck_ref),
                                    _chunks(v_ref, c0, B_V_DIM, tk, cv_ref), s_scr, rb))
        o = outs[0] - lam * outs[1]
        o_ref[:, c0:c0 + B_V_DIM] = (_rms(o, sg_ref[...]) * (1.0 - lam_init)).astype(BF16)


def _attention(qkv, nb, caches, lam_p, subln_g, lam_init, tq_a, tq_b, tk, hp):
    r = qkv.shape[0]
    length = r // nb
    has_cache = caches is not None
    lk = length + (caches[0].shape[0] // nb if has_cache else 0)
    tk = min(tk, length)
    wk = hp * HEAD_DIM
    ka0, va0 = A_HEADS * HEAD_DIM // wk, (A_HEADS + A_KV_HEADS) * HEAD_DIM // wk
    wb = hp * B_V_DIM
    qb0 = (A_HEADS + 2 * A_KV_HEADS) * HEAD_DIM // wb
    kb0, vb0 = qb0 + B_HEADS // hp, qb0 + 2 * B_HEADS // hp
    assert (A_HEADS * HEAD_DIM) % wk == 0 and ((A_HEADS + 2 * A_KV_HEADS) * HEAD_DIM) % wb == 0

    nq = length // tq_a
    wq = hp * A_GROUP * HEAD_DIM
    in_specs = [pl.BlockSpec((tq_a, wq), lambda b, h, i: (b * nq + i, h)),
                pl.BlockSpec((length, wk), lambda b, h, i: (b, ka0 + h)),
                pl.BlockSpec((length, wk), lambda b, h, i: (b, va0 + h))]
    args = [qkv, qkv, qkv]
    if has_cache:
        pc = caches[0].shape[0] // nb
        in_specs += [pl.BlockSpec((pc, wk), lambda b, h, i: (b, h))] * 2
        args += [caches[0], caches[1]]
    rb_a = min(256, A_GROUP * tq_a)
    oa = pl.pallas_call(
        functools.partial(_gqa_kernel, has_cache=has_cache, tk=tk, rb=rb_a),
        grid=(nb, A_KV_HEADS // hp, nq),
        in_specs=in_specs,
        out_specs=pl.BlockSpec((tq_a, wq), lambda b, h, i: (b * nq + i, h)),
        out_shape=jax.ShapeDtypeStruct((r, A_HEADS * HEAD_DIM), BF16),
        scratch_shapes=[pltpu.VMEM((A_GROUP * tq_a, lk), F32)],
        compiler_params=_cparams(3),
    )(*args)

    nq = length // tq_b
    in_specs = [pl.BlockSpec((tq_b, wb), lambda b, h, i: (b * nq + i, qb0 + h)),
                pl.BlockSpec((length, wb), lambda b, h, i: (b, kb0 + h)),
                pl.BlockSpec((length, wb), lambda b, h, i: (b, vb0 + h))]
    args = [qkv, qkv, qkv]
    if has_cache:
        in_specs += [pl.BlockSpec((pc, wb), lambda b, h, i: (b, h))] * 2
        args += [caches[2], caches[3]]
    in_specs += [pl.BlockSpec((4, HEAD_DIM), lambda b, h, i: (0, 0)),
                 pl.BlockSpec((1, B_V_DIM), lambda b, h, i: (0, 0))]
    args += [lam_p, subln_g]
    ob = pl.pallas_call(
        functools.partial(_diff_kernel, has_cache=has_cache, tk=tk, rb=min(128, tq_b), lam_init=lam_init),
        grid=(nb, B_HEADS // hp, nq),
        in_specs=in_specs,
        out_specs=pl.BlockSpec((tq_b, wb), lambda b, h, i: (b * nq + i, h)),
        out_shape=jax.ShapeDtypeStruct((r, B_HEADS * B_V_DIM), BF16),
        scratch_shapes=[pltpu.VMEM((tq_b, lk), F32)],
        compiler_params=_cparams(3),
    )(*args)
    return oa, ob


def _outproj_kernel(*refs, n_in):
    o_refs = refs[:n_in]
    w_refs = refs[n_in:2 * n_in]
    x_ref, g_ref, gate_ref, out_ref = refs[2 * n_in:]
    y = None
    for o_ref, w_ref in zip(o_refs, w_refs):
        t = jnp.dot(o_ref[...], w_ref[...], preferred_element_type=F32)
        y = t if y is None else y + t
    out_ref[...] = x_ref[...] + _gated_rms(y, g_ref[...], gate_ref[0])


def _outproj(os_, w, x, g, mod, row_fn, tm):
    r, d = x.shape
    n_in = len(os_)
    kw = os_[0].shape[1]
    assert all(o.shape[1] == kw for o in os_) and w.shape == (n_in * kw, d)
    in_specs = ([pl.BlockSpec((tm, kw), lambda i: (i, 0)) for _ in os_]
                + [pl.BlockSpec((kw, d), lambda i, k=k: (k, 0), pipeline_mode=pl.Buffered(1)) for k in range(n_in)]
                + [pl.BlockSpec((tm, d), lambda i: (i, 0)),
                   pl.BlockSpec((1, d), lambda i: (0, 0)),
                   _mod_spec(d, 2, row_fn)])
    return pl.pallas_call(
        functools.partial(_outproj_kernel, n_in=n_in),
        grid=(r // tm,),
        in_specs=in_specs,
        out_specs=pl.BlockSpec((tm, d), lambda i: (i, 0)),
        out_shape=jax.ShapeDtypeStruct((r, d), F32),
        compiler_params=_cparams(1),
    )(*os_, *([w] * n_in), x, g, mod)


def _mlp_kernel(x_ref, g2_ref, sh_ref, sc_ref, gate_ref, g3_ref, w1_ref, w2_ref, out_ref, n_scr):
    f = pl.program_id(1)

    def ffn_block(first):
        h = jnp.maximum(jnp.dot(n_scr[...], w1_ref[...], preferred_element_type=F32), 0.0)
        hb = (h * h).astype(BF16)
        for c0 in range(0, out_ref.shape[1], MLP_ACC_CHUNK):
            cols = slice(c0, c0 + MLP_ACC_CHUNK)
            t = jnp.dot(hb, w2_ref[:, cols], preferred_element_type=F32)
            if first:
                out_ref[:, cols] = t
            else:
                out_ref[:, cols] += t

    @pl.when(f == 0)
    def _():
        n_scr[...] = _modulate(x_ref[...], g2_ref[...], sc_ref[0], sh_ref[0]).astype(BF16)
        ffn_block(True)

    @pl.when(f > 0)
    def _():
        ffn_block(False)

    @pl.when(f == pl.num_programs(1) - 1)
    def _():
        out_ref[...] = x_ref[...] + _gated_rms(out_ref[...], g3_ref[...], gate_ref[0])


def _mlp(x, g2, g3, mod, row_fn, w1, w2, layer, tm, tf):
    r, d = x.shape
    dff = w1.shape[2]
    return pl.pallas_call(
        _mlp_kernel,
        grid=(r // tm, dff // tf),
        in_specs=[pl.BlockSpec((tm, d), lambda i, f: (i, 0)),
                  pl.BlockSpec((1, d), lambda i, f: (0, 0)),
                  _mod_spec(d, 3, row_fn), _mod_spec(d, 4, row_fn), _mod_spec(d, 5, row_fn),
                  pl.BlockSpec((1, d), lambda i, f: (0, 0)),
                  pl.BlockSpec((None, d, tf), lambda i, f: (layer, 0, f)),
                  pl.BlockSpec((None, tf, d), lambda i, f: (layer, f, 0))],
        out_specs=pl.BlockSpec((tm, d), lambda i, f: (i, 0)),
        out_shape=jax.ShapeDtypeStruct((r, d), F32),
        scratch_shapes=[pltpu.VMEM((tm, d), BF16)],
        compiler_params=_cparams(2),
    )(x, g2, mod, mod, mod, g3, w1, w2)


def _ssm_in_kernel(x_ref, g_ref, sh_ref, sc_ref, w_ref, u_ref):
    n = _modulate(x_ref[...], g_ref[...], sc_ref[0], sh_ref[0]).astype(BF16)
    for c0 in range(0, w_ref.shape[1], PROJ_CHUNK):
        u_ref[:, c0:c0 + PROJ_CHUNK] = jnp.dot(n, w_ref[:, c0:c0 + PROJ_CHUNK], preferred_element_type=F32)


def _ssm_in(x, g, mod, row_fn, w, tm):
    r, d = x.shape
    n = w.shape[1]
    return pl.pallas_call(
        _ssm_in_kernel,
        grid=(r // tm,),
        in_specs=[pl.BlockSpec((tm, d), lambda i: (i, 0)),
                  pl.BlockSpec((1, d), lambda i: (0, 0)),
                  _mod_spec(d, 0, row_fn), _mod_spec(d, 1, row_fn),
                  pl.BlockSpec((d, n), lambda i: (0, 0), pipeline_mode=pl.Buffered(1))],
        out_specs=pl.BlockSpec((tm, n), lambda i: (i, 0)),
        out_shape=jax.ShapeDtypeStruct((r, n), F32),
        compiler_params=_cparams(1),
    )(x, g, mod, mod, w)


def _ssm_params(a_re, a_im, log_dt, b_re, b_im, c_re, c_im, seg_chunks):
    g = a_re.shape[1]
    vec = lambda x: x.astype(F32).reshape(2, g, 1, -1)
    args = (vec(log_dt), vec(a_re), vec(a_im)) + tuple(x.astype(F32) for x in (b_re, b_im, c_re, c_im))
    pg = SSM_PARAM_GROUPS
    per_group = lambda a: pl.BlockSpec((2, pg) + a.shape[2:], lambda i: (0, i, 0, 0))
    blk = lambda a: pl.BlockSpec((pg,) + a.shape[1:], lambda i: (i, 0, 0))
    out_shape = [jax.ShapeDtypeStruct((g, SSM_KDIM, SSM_LANES), BF16),
                 jax.ShapeDtypeStruct((g, SSM_KDIM, SSM_KDIM), BF16),
                 jax.ShapeDtypeStruct((g, SSM_KDIM, SSM_LANES), BF16),
                 jax.ShapeDtypeStruct((g, 1, SSM_LANES), F32),
                 jax.ShapeDtypeStruct((g, 1, SSM_LANES), F32)]
    return pl.pallas_call(
        functools.partial(_ssm_params_kernel, seg_chunks=seg_chunks),
        grid=(g // pg,),
        in_specs=[per_group(a) for a in args],
        out_specs=[blk(o) for o in out_shape],
        out_shape=out_shape,
        compiler_params=_cparams(1),
    )(*args)


def _ssm_params_kernel(*refs, seg_chunks):
    for k in range(SSM_PARAM_GROUPS):
        _ssm_params_group(k, *refs, seg_chunks=seg_chunks)


def _ssm_params_group(k, ldt_ref, are_ref, aim_ref, br_ref, bi_ref, cr_ref, ci_ref,
                      win_ref, toep_ref, cout_ref, achunk_ref, aseg_ref, *, seg_chunks):
    t, nch, half = SSM_CHUNK, S5_GROUP_CH, 2 * S5_STATE
    hi = lax.Precision.HIGHEST
    both = lambda ref: jnp.concatenate([ref[0, k], ref[1, k]], axis=1)
    eye = (lax.broadcasted_iota(jnp.int32, (S5_STATE, S5_STATE), 0)
           == lax.broadcasted_iota(jnp.int32, (S5_STATE, S5_STATE), 1)).astype(F32)
    tr = lambda m: lax.dot_general(m, eye, (((0,), (0,)), ((), ())), precision=hi, preferred_element_type=F32)
    both_t = lambda ref: jnp.concatenate([tr(ref[0, k]), tr(ref[1, k])], axis=1)
    ar, ai = both(are_ref), both(aim_ref)
    dt = jnp.exp(jnp.concatenate([jnp.broadcast_to(ldt_ref[d, k], (1, S5_STATE)) for d in range(2)], axis=1))
    lr, li = ar * dt, ai * dt

    def power(k):
        mag = jnp.exp(lr * k)
        return mag * jnp.cos(li * k), mag * jnp.sin(li * k)

    pw = [(jnp.ones_like(lr), jnp.zeros_like(lr)), power(1.0)]
    for _ in range(t - 1):
        pw.append((pw[-1][0] * pw[1][0] - pw[-1][1] * pw[1][1], pw[-1][0] * pw[1][1] + pw[-1][1] * pw[1][0]))
    fwd1 = lax.broadcasted_iota(jnp.int32, (1, half), 1) < S5_STATE
    pick = lambda kf, kb: (jnp.where(fwd1, pw[kf][0], pw[kb][0]), jnp.where(fwd1, pw[kf][1], pw[kb][1]))
    xr, xi = pw[1][0] - 1.0, pw[1][1]
    den = ar * ar + ai * ai
    qr, qi = (xr * ar + xi * ai) / den, (xi * ar - xr * ai) / den
    btr, bti = both_t(br_ref), both_t(bi_ref)
    bbr = qr * btr - qi * bti
    bbi = qr * bti + qi * btr
    ccr, cci = both(cr_ref), both(ci_ref)

    for s in range(t):
        rows = slice(s * nch, (s + 1) * nch)
        er, ei = pick(t - 1 - s, s)
        win_ref[k, rows, 0:half] = (er * bbr - ei * bbi).astype(BF16)
        win_ref[k, rows, half:] = (er * bbi + ei * bbr).astype(BF16)
        er, ei = pick(s + 1, t - s)
        cout_ref[k, rows, 0:half] = (ccr * er - cci * ei).astype(BF16)
        cout_ref[k, rows, half:] = (-(ccr * ei + cci * er)).astype(BF16)

    fwd = lax.broadcasted_iota(jnp.int32, (nch, half), 1) < S5_STATE
    cp = [(ccr * pw[k][0] - cci * pw[k][1], ccr * pw[k][1] + cci * pw[k][0]) for k in range(t)]
    stack = lambda ks, j: jnp.concatenate([cp[k][j] for k in ks], axis=0)
    nt = lambda x, y: lax.dot_general(x, y, (((1,), (1,)), ((), ())), precision=hi, preferred_element_type=F32)
    zero = jnp.zeros_like(bbr)
    order, rev = list(range(t)), list(range(t - 1, -1, -1))
    taps_f = (nt(jnp.where(fwd, bbr, zero), stack(order, 0)) - nt(jnp.where(fwd, bbi, zero), stack(order, 1)))
    taps_b = (nt(jnp.where(fwd, zero, bbr), stack(rev, 0)) - nt(jnp.where(fwd, zero, bbi), stack(rev, 1)))
    lane = lax.broadcasted_iota(jnp.int32, (nch, SSM_KDIM), 1)
    for s in range(t):
        tf = taps_f if s == 0 else pltpu.roll(taps_f, s * nch, axis=1)
        tb = taps_b if s == t - 1 else pltpu.roll(taps_b, (s + 1) * nch, axis=1)
        blk = jnp.where(lane >= s * nch, tf, 0.0) + jnp.where(lane < (s + 1) * nch, tb, 0.0)
        toep_ref[k, s * nch:(s + 1) * nch, :] = blk.astype(BF16)

    for ref, (dr, di) in ((achunk_ref, pw[t]), (aseg_ref, power(float(t * seg_chunks)))):
        ref[k, :, 0:half] = dr
        ref[k, :, half:] = di


def _cmul_add(ar, ai, h_re, h_im, s_re, s_im):
    return ar * h_re - ai * h_im + s_re, ar * h_im + ai * h_re + s_im


def _regroup_perms():
    n = SSM_CHUNK * V7X_LANES
    idx = jnp.arange(n)
    tok, grp, ch = idx // V7X_LANES, idx % V7X_LANES // S5_GROUP_CH, idx % S5_GROUP_CH
    dst = grp * SSM_KDIM + tok * S5_GROUP_CH + ch
    p_in = (dst[:, None] == idx[None, :]).astype(BF16)
    return p_in, p_in.T


def _regroup_in(u_ref, pin_ref, lhs_scr, nrows):
    t, rbk = SSM_CHUNK, min(SSM_REGROUP_ROWS, nrows)

    def body(rb, carry):
        pieces = [u_ref[pl.ds(rb * (rbk * t) + tt, rbk, stride=t), :].astype(BF16) for tt in range(t)]
        out = jnp.dot(jnp.concatenate(pieces, axis=1), pin_ref[...], preferred_element_type=F32)
        rows = pl.ds(pl.multiple_of(rb * rbk, rbk), rbk)
        for gl in range(SSM_TILE_GROUPS):
            lhs_scr[gl, rows, :] = out[:, gl * SSM_KDIM:(gl + 1) * SSM_KDIM].astype(BF16)
        return carry

    lax.fori_loop(0, nrows // rbk, body, 0)


def _regroup_out(y_scr, pout_ref, y_ref, nrows):
    t, rbk = SSM_CHUNK, min(SSM_REGROUP_ROWS, nrows)

    def body(rb, carry):
        rows = pl.ds(pl.multiple_of(rb * rbk, rbk), rbk)
        rest = jnp.concatenate([y_scr[gl, rows, :] for gl in range(SSM_TILE_GROUPS)], axis=1)
        out = None
        for _ in range(SSM_Y_TERMS):
            term = rest.astype(BF16)
            rest = rest - term.astype(F32)
            part = jnp.dot(term, pout_ref[...], preferred_element_type=F32)
            out = part if out is None else out + part
        for tt in range(t):
            y_ref[pl.ds(rb * (rbk * t) + tt, rbk, stride=t), :] = out[:, tt * V7X_LANES:(tt + 1) * V7X_LANES]
        return carry

    lax.fori_loop(0, nrows // rbk, body, 0)


def _ssm_core_kernel(u_ref, pin_ref, pout_ref, win_ref, toep_ref, cout_ref, a_ref, aseg_ref, h0_ref, y_ref, hfin_ref,
                     lhs_scr, y_scr, s_re, s_im, tf_re, tb_re, tf_im, tb_im, *, nseq, nc, nsub):
    step = pl.program_id(1)
    nrows = nseq * nc

    @pl.when(step == 0)
    def _():
        _regroup_in(u_ref, pin_ref, lhs_scr, nrows)

    for k in range(SSM_GROUPS_PER_STEP):
        _ssm_group(k, step * SSM_GROUPS_PER_STEP + k, win_ref, toep_ref, cout_ref, a_ref, aseg_ref, h0_ref, hfin_ref,
                   lhs_scr, y_scr, s_re, s_im, tf_re, tb_re, tf_im, tb_im, nseq, nc, nsub)

    @pl.when(step == pl.num_programs(1) - 1)
    def _():
        _regroup_out(y_scr, pout_ref, y_ref, nrows)


def _ssm_group(k, g, win_ref, toep_ref, cout_ref, a_ref, aseg_ref, h0_ref, hfin_ref,
               lhs_scr, y_scr, s_re, s_im, tf_re, tb_re, tf_im, tb_im, nseq, nc, nsub):
    half = SSM_LANES // 2
    pitch = nc + SSM_PITCH_PAD
    nrows = nseq * nc
    nslab = nseq // V7X_SUBLANES
    base = k * nseq * pitch

    lhs = lhs_scr[g]
    s = jnp.dot(lhs, win_ref[k], preferred_element_type=F32)
    for p in range(nseq):
        s_re[base + p * pitch:base + p * pitch + nc, :] = s[p * nc:(p + 1) * nc, 0:half]
        s_im[base + p * pitch:base + p * pitch + nc, :] = s[p * nc:(p + 1) * nc, half:]

    ar, ai = a_ref[k][:, 0:half], a_ref[k][:, half:]
    fwd = lax.broadcasted_iota(jnp.int32, (V7X_SUBLANES, half), 1) < S5_STATE

    def slab(q, c):
        return pl.ds(base + q * V7X_SUBLANES * pitch + c, V7X_SUBLANES, stride=pitch)

    def scan(init, store):
        hs = list(init)
        for c in range(nc):
            cb = nc - 1 - c
            for q in range(nslab):
                h_re, h_im = hs[q]
                x_re = jnp.where(fwd, s_re[slab(q, c), :], s_re[slab(q, cb), :])
                x_im = jnp.where(fwd, s_im[slab(q, c), :], s_im[slab(q, cb), :])
                if store:
                    tf_re[slab(q, c), :] = h_re
                    tb_re[slab(q, cb), :] = h_re
                    tf_im[slab(q, c), :] = h_im
                    tb_im[slab(q, cb), :] = h_im
                hs[q] = _cmul_add(ar, ai, h_re, h_im, x_re, x_im)
        return hs

    rows8 = lambda q: slice(q * V7X_SUBLANES, (q + 1) * V7X_SUBLANES)
    init = [(h0_ref[k, rows8(q), 0:half], h0_ref[k, rows8(q), half:]) for q in range(nslab)]
    if nsub > 1:
        zero = jnp.zeros((V7X_SUBLANES, half), F32)
        ends = scan([(zero, zero)] * nslab, False)
        gr, gi = aseg_ref[k][:, 0:half], aseg_ref[k][:, half:]
        seg = lax.broadcasted_iota(jnp.int32, (V7X_SUBLANES, half), 0) % nsub
        carried = jnp.where(fwd, seg, nsub - 1 - seg) != 0
        shift = lambda x: jnp.where(fwd, pltpu.roll(x, 1, axis=0), pltpu.roll(x, V7X_SUBLANES - 1, axis=0))
        for q in range(nslab):
            (h0_re, h0_im), (f_re, f_im) = init[q], ends[q]
            e_re, e_im = h0_re, h0_im
            for _ in range(nsub - 1):
                x_re, x_im = _cmul_add(gr, gi, e_re, e_im, f_re, f_im)
                e_re = h0_re + jnp.where(carried, shift(x_re), 0.0)
                e_im = h0_im + jnp.where(carried, shift(x_im), 0.0)
            init[q] = (e_re, e_im)
    hs = scan(init, True)
    for q in range(nslab):
        hfin_ref[k, rows8(q), 0:half] = hs[q][0]
        hfin_ref[k, rows8(q), half:] = hs[q][1]

    gather = lambda ref: jnp.concatenate([ref[base + p * pitch:base + p * pitch + nc, :] for p in range(nseq)],
                                         axis=0)
    fwd_rows = lax.broadcasted_iota(jnp.int32, (nrows, half), 1) < S5_STATE
    hp = jnp.concatenate([jnp.where(fwd_rows, gather(tf_re), gather(tb_re)),
                          jnp.where(fwd_rows, gather(tf_im), gather(tb_im))], axis=1).astype(BF16)
    y_scr[g] = (jnp.dot(lhs, toep_ref[k], preferred_element_type=F32)
                + lax.dot_general(hp, cout_ref[k], (((1,), (1,)), ((), ())), preferred_element_type=F32))


def _ssm_core(u, ops, h0, nseq, nc, nsub):
    r, d = u.shape
    nrows = nseq * nc
    assert r == nrows * SSM_CHUNK and nseq % V7X_SUBLANES == 0 and V7X_SUBLANES % nsub == 0
    assert nrows % min(SSM_REGROUP_ROWS, nrows) == 0 and nc % V7X_SUBLANES == 0
    ntile = d // V7X_LANES
    gps = SSM_GROUPS_PER_STEP
    steps = SSM_TILE_GROUPS // gps
    grp = lambda j, g: (j * steps + g, 0, 0)
    scan_rows = gps * nseq * (nc + SSM_PITCH_PAD)
    p_in, p_out = _regroup_perms()
    perm_spec = pl.BlockSpec(p_in.shape, lambda j, g: (0, 0), pipeline_mode=pl.Buffered(1))
    return pl.pallas_call(
        functools.partial(_ssm_core_kernel, nseq=nseq, nc=nc, nsub=nsub),
        grid=(ntile, steps),
        in_specs=[pl.BlockSpec((r, V7X_LANES), lambda j, g: (0, j)),
                  perm_spec, perm_spec]
                 + [pl.BlockSpec((gps,) + o.shape[1:], grp) for o in ops]
                 + [pl.BlockSpec((gps, nseq, SSM_LANES), grp)],
        out_specs=[pl.BlockSpec((r, V7X_LANES), lambda j, g: (0, j)),
                   pl.BlockSpec((gps, nseq, SSM_LANES), grp)],
        out_shape=[jax.ShapeDtypeStruct((r, d), F32),
                   jax.ShapeDtypeStruct((d // S5_GROUP_CH, nseq, SSM_LANES), F32)],
        scratch_shapes=[pltpu.VMEM((SSM_TILE_GROUPS, nrows, SSM_KDIM), BF16),
                        pltpu.VMEM((SSM_TILE_GROUPS, nrows, SSM_KDIM), F32)]
                       + [pltpu.VMEM((scan_rows, V7X_LANES), F32)] * 6,
        compiler_params=_cparams(2),
    )(u, p_in, p_out, *ops, h0)


def _glu_kernel(y_ref, u_ref, d_ref, w_ref, z_ref):
    v = y_ref[...] + d_ref[...] * u_ref[...]
    z = 0.5 * v * (1.0 + jnp.tanh(math.sqrt(2.0 / math.pi) * (v + 0.044715 * (v * v * v))))
    zb = z.astype(BF16)
    for c0 in range(0, w_ref.shape[1], PROJ_CHUNK):
        cols = slice(c0, c0 + PROJ_CHUNK)
        gl = jnp.dot(zb, w_ref[:, cols], preferred_element_type=F32)
        z_ref[:, cols] = (z[:, cols] * _sigmoid(gl)).astype(BF16)


def _glu(y, u, d, w, tm):
    r, dm = y.shape
    return pl.pallas_call(
        _glu_kernel,
        grid=(r // tm,),
        in_specs=[pl.BlockSpec((tm, dm), lambda i: (i, 0)),
                  pl.BlockSpec((tm, dm), lambda i: (i, 0)),
                  pl.BlockSpec((1, dm), lambda i: (0, 0)),
                  pl.BlockSpec(w.shape, lambda i: (0, 0), pipeline_mode=pl.Buffered(1))],
        out_specs=pl.BlockSpec((tm, dm), lambda i: (i, 0)),
        out_shape=jax.ShapeDtypeStruct((r, dm), BF16),
        compiler_params=_cparams(1),
    )(y, u, d, w)


def _rope_tables(length):
    rows = length // GRID_W
    row = jnp.repeat(jnp.arange(rows, dtype=F32), GRID_W)
    col = jnp.tile(jnp.arange(GRID_W, dtype=F32), rows)
    inv = ROPE_BASE ** (-jnp.arange(ROPE_PAIRS_PER_AXIS, dtype=F32) / ROPE_PAIRS_PER_AXIS)
    ang = jnp.concatenate([row[:, None] * inv, col[:, None] * inv], axis=-1)
    cos, sin = jnp.cos(ang), jnp.sin(ang)
    return jnp.concatenate([cos, cos], axis=-1), jnp.concatenate([-sin, sin], axis=-1)


def kernel(x_prompt, x_sample, c, cache_a_k, cache_a_v, cache_b_k, cache_b_v, state_ssm, c_ctx, ada_w, ada_b, norm_g, mlp_w1, mlp_w2, attn_w_in, attn_w_out, attn_qk_norm, diff_lambda, diff_subln, ssm_w_in, ssm_a_re, ssm_a_im, ssm_log_dt, ssm_b_re, ssm_b_im, ssm_c_re, ssm_c_im, ssm_d, ssm_glu_w, ssm_w_out):
    bp, lp, d = x_prompt.shape
    bs, ls, _ = x_sample.shape
    depth = ada_w.shape[0]
    past = cache_a_k.shape[2]
    nsub = SSM_SEGMENTS
    assert bs + 1 <= 8 and bp % V7X_SUBLANES == 0
    assert lp % SSM_CHUNK == 0 and ls % (SSM_CHUNK * nsub) == 0 and ls % GRID_W == 0

    cond8 = jnp.zeros((8, d), F32).at[0].set(c_ctx).at[1:1 + bs].set(c)
    mods = _modulation(cond8, ada_w, ada_b).reshape(depth, 8 * N_MOD, 1, d)

    xp = x_prompt.reshape(bp * lp, d)
    xs = x_sample.reshape(bs * ls, d)
    tm = min(512, bp * lp, ls)
    tm_big = min(1024, bp * lp, ls)
    row_p = lambda i: 0
    row_s = lambda i: 1 + i // (ls // tm)
    row_s_big = lambda i: 1 + i // (ls // tm_big)
    w1_all = mlp_w1.astype(BF16)
    w2_all = mlp_w2.astype(BF16)
    new_ak = new_av = new_bk = new_bv = new_ssm = None
    for l in range(depth):
        mod = mods[l]
        g = norm_g[l][:, None, :]
        i = l // 2
        if l % 2 == 0:
            lam_init = 0.8 - 0.6 * math.exp(-0.3 * l)
            w_in = attn_w_in[i].astype(BF16)
            w_out = attn_w_out[i].astype(BF16)
            qkv, new_ak, new_av, new_bk, new_bv = _qkv_proj(xp, g[0], mod, row_p, w_in, attn_qk_norm[i], None, lp, tm)
            oa, ob = _attention(qkv, bp, None, diff_lambda[i], diff_subln[i][None, :], lam_init,
                                tq_a=lp, tq_b=lp, tk=512, hp=2)
            xp = _outproj([oa, ob], w_out, xp, g[1], mod, row_p, tm)
            (qkv,) = _qkv_proj(xs, g[0], mod, row_s, w_in, attn_qk_norm[i], _rope_tables(ls), 0, tm)
            caches = (cache_a_k[:, i].reshape(bs * past, A_KV_HEADS * HEAD_DIM),
                      cache_a_v[:, i].reshape(bs * past, A_KV_HEADS * HEAD_DIM),
                      cache_b_k[:, i].reshape(bs * past, B_HEADS * 2 * HEAD_DIM),
                      cache_b_v[:, i].reshape(bs * past, B_HEADS * B_V_DIM))
            oa, ob = _attention(qkv, bs, caches, diff_lambda[i], diff_subln[i][None, :], lam_init,
                                tq_a=min(512, ls), tq_b=min(1024, ls), tk=512, hp=1)
            xs = _outproj([oa, ob], w_out, xs, g[1], mod, row_s, tm)
        else:
            w_in = ssm_w_in[i].astype(BF16)
            u_p = _ssm_in(xp, g[0], mod, row_p, w_in, tm_big)
            u_s = _ssm_in(xs, g[0], mod, row_s_big, w_in, tm_big)
            ng = d // S5_GROUP_CH
            nc_p, nc_s = lp // SSM_CHUNK, ls // SSM_CHUNK // nsub
            ops = _ssm_params(ssm_a_re[i], ssm_a_im[i], ssm_log_dt[i], ssm_b_re[i], ssm_b_im[i],
                                                 ssm_c_re[i], ssm_c_im[i], nc_s)
            st = state_ssm[:, i].astype(F32)
            h0 = jnp.zeros((ng, bs, nsub, 4, S5_STATE), F32)
            h0 = h0.at[:, :, 0, 0].set(jnp.transpose(st[:, 0, :, :, 0], (1, 0, 2)))
            h0 = h0.at[:, :, 0, 2].set(jnp.transpose(st[:, 0, :, :, 1], (1, 0, 2)))
            h0 = h0.at[:, :, nsub - 1, 1].set(jnp.transpose(st[:, 1, :, :, 0], (1, 0, 2)))
            h0 = h0.at[:, :, nsub - 1, 3].set(jnp.transpose(st[:, 1, :, :, 1], (1, 0, 2)))
            h0 = h0.reshape(ng, bs * nsub, SSM_LANES)
            y_p, hfin = _ssm_core(u_p, ops, jnp.zeros((ng, bp, SSM_LANES), F32), bp, nc_p, 1)
            y_s, _ = _ssm_core(u_s, ops, h0, bs * nsub, nc_s, nsub)
            new_ssm = jnp.transpose(hfin.reshape(ng, bp, 2, 2, S5_STATE), (1, 3, 0, 4, 2))
            glu_w = ssm_glu_w[i].astype(BF16)
            w_out = ssm_w_out[i].astype(BF16)
            dvec = ssm_d[i][None, :]
            z_p = _glu(y_p, u_p, dvec, glu_w, tm)
            z_s = _glu(y_s, u_s, dvec, glu_w, tm)
            xp = _outproj([z_p], w_out, xp, g[1], mod, row_p, tm)
            xs = _outproj([z_s], w_out, xs, g[1], mod, row_s, tm)
        xp = _mlp(xp, g[2], g[3], mod, row_p, w1_all, w2_all, l, tm_big, 512)
        xs = _mlp(xs, g[2], g[3], mod, row_s_big, w1_all, w2_all, l, tm_big, 512)
    return (xp.reshape(bp, lp, d), xs.reshape(bs, ls, d), new_ak, new_av, new_bk, new_bv,
            new_ssm[:, None])
```

```python
import functools
import math

import jax
import jax.numpy as jnp
from jax import lax
from jax.experimental import pallas as pl
from jax.experimental.pallas import tpu as pltpu

F32 = jnp.float32
BF16 = jnp.bfloat16
EPS = 1e-6

HEAD_DIM = 128
A_HEADS = 8
A_KV_HEADS = 2
A_GROUP = A_HEADS // A_KV_HEADS
B_HEADS = 4
B_V_DIM = 2 * HEAD_DIM
GRID_W = 64
ROPE_BASE = 10000.0
ROPE_PAIRS_PER_AXIS = HEAD_DIM // 4
N_MOD = 6
S5_GROUP_CH = 16
S5_STATE = 64

V7X_LANES = 128
V7X_SUBLANES = 8
V7X_VMEM_LIMIT_BYTES = 56 * 1024 * 1024

SSM_CHUNK = V7X_SUBLANES
SSM_TILE_GROUPS = V7X_LANES // S5_GROUP_CH
SSM_KDIM = SSM_CHUNK * S5_GROUP_CH
SSM_LANES = 4 * S5_STATE
SSM_GROUPS_PER_STEP = 2
SSM_PARAM_GROUPS = 4
SSM_SEGMENTS = 8
SSM_REGROUP_ROWS = 256
SSM_Y_TERMS = 2
SSM_PITCH_PAD = 8

MLP_ACC_CHUNK = 512
QKV_CHUNK = 256
PROJ_CHUNK = 512


def _cparams(n_axes):
    return pltpu.CompilerParams(dimension_semantics=("arbitrary",) * n_axes,
                                vmem_limit_bytes=V7X_VMEM_LIMIT_BYTES)


def _sigmoid(x):
    return 1.0 / (1.0 + jnp.exp(-x))


def _rms(x, g):
    return x * lax.rsqrt(jnp.mean(x * x, axis=-1, keepdims=True) + EPS) * g


def _modulate(x, g, scale, shift):
    return _rms(x, g * (1.0 + scale)) + shift


def _gated_rms(x, g, gate):
    return _rms(x, g * gate)


def _mod_spec(d, slot, row_fn):
    return pl.BlockSpec((1, 1, d), lambda i, *_: (row_fn(i) * N_MOD + slot, 0, 0))


def _mod_kernel(c_ref, w_ref, b_ref, o_ref):
    c = c_ref[...]
    s = (c * _sigmoid(c)).astype(BF16)
    o_ref[0] = jnp.dot(s, w_ref[0].astype(BF16), preferred_element_type=F32) + b_ref[0]


def _modulation(cond8, ada_w, ada_b):
    depth, d, n = ada_w.shape
    tn = 1024
    return pl.pallas_call(
        _mod_kernel,
        grid=(depth, n // tn),
        in_specs=[pl.BlockSpec((8, d), lambda l, j: (0, 0)),
                  pl.BlockSpec((1, d, tn), lambda l, j: (l, 0, j)),
                  pl.BlockSpec((1, 1, tn), lambda l, j: (l, 0, j))],
        out_specs=pl.BlockSpec((1, 8, tn), lambda l, j: (l, 0, j)),
        out_shape=jax.ShapeDtypeStruct((depth, 8, n), F32),
        compiler_params=_cparams(2),
    )(cond8, ada_w, ada_b.reshape(depth, 1, n))


def _split_heads(y, fn):
    return jnp.concatenate([fn(y[:, h * HEAD_DIM:(h + 1) * HEAD_DIM]) for h in range(y.shape[1] // HEAD_DIM)], axis=1)


def _qkv_kernel(*refs, rope, caches):
    x_ref, g_ref, sh_ref, sc_ref, w_ref, qkg_ref = refs[:6]
    refs = refs[6:]
    if rope:
        cos_ref, sin_ref = refs[:2]
        refs = refs[2:]
    qkv_ref = refs[0]
    if caches:
        ka_ref, va_ref, kb_ref, vb_ref = refs[1:5]
    n = _modulate(x_ref[...], g_ref[...], sc_ref[0], sh_ref[0]).astype(BF16)

    def rot(y):
        if not rope:
            return y
        return y * cos_ref[...] + pltpu.roll(y, HEAD_DIM // 2, axis=1) * sin_ref[...]

    q_scale = HEAD_DIM ** -0.5 * math.log2(math.e)
    qa_end = A_HEADS * HEAD_DIM
    ka_end = qa_end + QKV_CHUNK
    va_end = ka_end + QKV_CHUNK
    qb_end = va_end + B_HEADS * 2 * HEAD_DIM
    kb_end = qb_end + B_HEADS * 2 * HEAD_DIM
    for c0 in range(0, w_ref.shape[1], QKV_CHUNK):
        cols = slice(c0, c0 + QKV_CHUNK)
        y = jnp.dot(n, w_ref[:, cols], preferred_element_type=F32)
        if c0 < qa_end:
            y = _split_heads(y, lambda t: rot(_rms(t, qkg_ref[0:1, :])) * q_scale)
        elif c0 < ka_end:
            y = _split_heads(y, lambda t: _rms(t, qkg_ref[1:2, :]))
            if caches:
                _store_cache(ka_ref, (), y, HEAD_DIM)
            y = _split_heads(y, rot)
        elif c0 < va_end:
            if caches:
                _store_cache(va_ref, (), y, HEAD_DIM)
        elif c0 < qb_end:
            y = _split_heads(y, lambda t: rot(t) * q_scale)
        elif c0 < kb_end:
            if caches:
                _store_cache(kb_ref, ((c0 - qb_end) // QKV_CHUNK,), y, HEAD_DIM)
            y = _split_heads(y, rot)
        elif caches:
            _store_cache(vb_ref, (), y, B_V_DIM, (c0 - kb_end) // QKV_CHUNK)
        qkv_ref[:, cols] = y.astype(BF16)


def _store_cache(ref, lead, y, width, only=None):
    length = ref.shape[2]
    for b in range(ref.shape[0]):
        rows = slice(b * length, (b + 1) * length)
        for j in range(y.shape[1] // width):
            idx = (b, 0, slice(None)) + tuple(lead) + (j if only is None else only, slice(None))
            ref[idx] = y[rows, j * width:(j + 1) * width]


def _qkv_proj(x, g, mod, row_fn, w, qk_g, rope_tabs, caches, tm):
    r, d = x.shape
    n = w.shape[1]
    assert A_KV_HEADS * HEAD_DIM == QKV_CHUNK and r % tm == 0 and n % QKV_CHUNK == 0
    assert not (caches and rope_tabs is not None)
    row = lambda i: (i, 0)
    in_specs = [pl.BlockSpec((tm, d), row),
                pl.BlockSpec((1, d), lambda i: (0, 0)),
                _mod_spec(d, 0, row_fn), _mod_spec(d, 1, row_fn),
                pl.BlockSpec((d, n), lambda i: (0, 0), pipeline_mode=pl.Buffered(1)),
                pl.BlockSpec((2, HEAD_DIM), lambda i: (0, 0))]
    args = [x, g, mod, mod, w, qk_g]
    if rope_tabs is not None:
        nblk = rope_tabs[0].shape[0] // tm
        in_specs += [pl.BlockSpec((tm, HEAD_DIM), lambda i: (i % nblk, 0))] * 2
        args += list(rope_tabs)
    out_specs = [pl.BlockSpec((tm, n), row)]
    out_shape = [jax.ShapeDtypeStruct((r, n), BF16)]
    if caches:
        assert tm % caches == 0
        for tail in ((A_KV_HEADS, HEAD_DIM), (A_KV_HEADS, HEAD_DIM), (B_HEADS, 2, HEAD_DIM), (B_HEADS, B_V_DIM)):
            out_specs.append(pl.BlockSpec((tm // caches, 1, caches) + tail, lambda i, z=(0,) * (2 + len(tail)): (i,) + z))
            out_shape.append(jax.ShapeDtypeStruct((r // caches, 1, caches) + tail, F32))
    return pl.pallas_call(
        functools.partial(_qkv_kernel, rope=rope_tabs is not None, caches=caches),
        grid=(r // tm,),
        in_specs=in_specs, out_specs=out_specs, out_shape=out_shape,
        compiler_params=_cparams(1),
    )(*args)


def _lane_tiles(x, op, acc):
    for j in range(0, x.shape[1], V7X_LANES):
        t = x[:, j:j + V7X_LANES]
        acc = t if acc is None else op(acc, t)
    return acc


def _softmax_pv(q, k_chunks, v_chunks, s_scr, rb):
    m_rows = q.shape[0]
    outs = []
    for r0 in range(0, m_rows, rb):
        qr = q[r0:r0 + rb]
        m_part = None
        off = 0
        for kc in k_chunks:
            k = kc()
            s = lax.dot_general(qr, k, (((1,), (1,)), ((), ())), preferred_element_type=F32)
            s_scr[r0:r0 + rb, off:off + k.shape[0]] = s
            m_part = _lane_tiles(s, jnp.maximum, m_part)
            off += k.shape[0]
        m = jnp.max(m_part, axis=-1, keepdims=True)
        l_part = acc = None
        off = 0
        for vc in v_chunks:
            v = vc()
            p = jnp.exp2(s_scr[r0:r0 + rb, off:off + v.shape[0]] - m)
            l_part = _lane_tiles(p, jnp.add, l_part)
            pv = jnp.dot(p.astype(BF16), v, preferred_element_type=F32)
            acc = pv if acc is None else acc + pv
            off += v.shape[0]
        outs.append(acc / jnp.sum(l_part, axis=-1, keepdims=True))
    return jnp.concatenate(outs, axis=0)


def _chunks(ref, c0, width, tk, cache_ref=None):
    length = ref.shape[0]
    out = [(lambda s=s: ref[s:s + tk, c0:c0 + width]) for s in range(0, length, tk)]
    if cache_ref is not None:
        out.append(lambda: cache_ref[:, c0:c0 + width].astype(BF16))
    return out


def _gqa_kernel(*refs, has_cache, tk, rb):
    if has_cache:
        q_ref, k_ref, v_ref, ck_ref, cv_ref, o_ref, s_scr = refs
    else:
        q_ref, k_ref, v_ref, o_ref, s_scr = refs
        ck_ref = cv_ref = None
    tq = q_ref.shape[0]
    wq = A_GROUP * HEAD_DIM
    for hh in range(k_ref.shape[1] // HEAD_DIM):
        q = jnp.concatenate([q_ref[:, hh * wq + g * HEAD_DIM:hh * wq + (g + 1) * HEAD_DIM] for g in range(A_GROUP)],
                            axis=0)
        o = _softmax_pv(q, _chunks(k_ref, hh * HEAD_DIM, HEAD_DIM, tk, ck_ref),
                        _chunks(v_ref, hh * HEAD_DIM, HEAD_DIM, tk, cv_ref), s_scr, rb)
        o_ref[:, hh * wq:(hh + 1) * wq] = jnp.concatenate([o[g * tq:(g + 1) * tq] for g in range(A_GROUP)],
                                                          axis=1).astype(BF16)


def _diff_kernel(*refs, has_cache, tk, rb, lam_init):
    if has_cache:
        q_ref, k_ref, v_ref, ck_ref, cv_ref, lam_ref, sg_ref, o_ref, s_scr = refs
    else:
        q_ref, k_ref, v_ref, lam_ref, sg_ref, o_ref, s_scr = refs
        ck_ref = cv_ref = None
    lp = lam_ref[...]
    lam = (jnp.exp(jnp.sum(lp[0:1] * lp[1:2], axis=-1, keepdims=True))
           - jnp.exp(jnp.sum(lp[2:3] * lp[3:4], axis=-1, keepdims=True)) + lam_init)
    for hh in range(v_ref.shape[1] // B_V_DIM):
        c0 = hh * B_V_DIM
        outs = []
        for m in range(2):
            cm = c0 + m * HEAD_DIM
            outs.append(_softmax_pv(q_ref[:, cm:cm + HEAD_DIM], _chunks(k_ref, cm, HEAD_DIM, tk, ck_ref),
                                    _chunks(v_ref, c0, B_V_DIM, tk, cv_ref), s_scr, rb))
        o = outs[0] - lam * outs[1]
        o_ref[:, c0:c0 + B_V_DIM] = (_rms(o, sg_ref[...]) * (1.0 - lam_init)).astype(BF16)


def _attention(qkv, nb, caches, lam_p, subln_g, lam_init, tq_a, tq_b, tk, hp):
    r = qkv.shape[0]
    length = r // nb
    has_cache = caches is not None
    lk = length + (caches[0].shape[0] // nb if has_cache else 0)
    tk = min(tk, length)
    wk = hp * HEAD_DIM
    ka0, va0 = A_HEADS * HEAD_DIM // wk, (A_HEADS + A_KV_HEADS) * HEAD_DIM // wk
    wb = hp * B_V_DIM
    qb0 = (A_HEADS + 2 * A_KV_HEADS) * HEAD_DIM // wb
    kb0, vb0 = qb0 + B_HEADS // hp, qb0 + 2 * B_HEADS // hp
    assert (A_HEADS * HEAD_DIM) % wk == 0 and ((A_HEADS + 2 * A_KV_HEADS) * HEAD_DIM) % wb == 0

    nq = length // tq_a
    wq = hp * A_GROUP * HEAD_DIM
    in_specs = [pl.BlockSpec((tq_a, wq), lambda b, h, i: (b * nq + i, h)),
                pl.BlockSpec((length, wk), lambda b, h, i: (b, ka0 + h)),
                pl.BlockSpec((length, wk), lambda b, h, i: (b, va0 + h))]
    args = [qkv, qkv, qkv]
    if has_cache:
        pc = caches[0].shape[0] // nb
        in_specs += [pl.BlockSpec((pc, wk), lambda b, h, i: (b, h))] * 2
        args += [caches[0], caches[1]]
    rb_a = min(256, A_GROUP * tq_a)
    oa = pl.pallas_call(
        functools.partial(_gqa_kernel, has_cache=has_cache, tk=tk, rb=rb_a),
        grid=(nb, A_KV_HEADS // hp, nq),
        in_specs=in_specs,
        out_specs=pl.BlockSpec((tq_a, wq), lambda b, h, i: (b * nq + i, h)),
        out_shape=jax.ShapeDtypeStruct((r, A_HEADS * HEAD_DIM), BF16),
        scratch_shapes=[pltpu.VMEM((A_GROUP * tq_a, lk), F32)],
        compiler_params=_cparams(3),
    )(*args)

    nq = length // tq_b
    in_specs = [pl.BlockSpec((tq_b, wb), lambda b, h, i: (b * nq + i, qb0 + h)),
                pl.BlockSpec((length, wb), lambda b, h, i: (b, kb0 + h)),
                pl.BlockSpec((length, wb), lambda b, h, i: (b, vb0 + h))]
    args = [qkv, qkv, qkv]
    if has_cache:
        in_specs += [pl.BlockSpec((pc, wb), lambda b, h, i: (b, h))] * 2
        args += [caches[2], caches[3]]
    in_specs += [pl.BlockSpec((4, HEAD_DIM), lambda b, h, i: (0, 0)),
                 pl.BlockSpec((1, B_V_DIM), lambda b, h, i: (0, 0))]
    args += [lam_p, subln_g]
    ob = pl.pallas_call(
        functools.partial(_diff_kernel, has_cache=has_cache, tk=tk, rb=min(128, tq_b), lam_init=lam_init),
        grid=(nb, B_HEADS // hp, nq),
        in_specs=in_specs,
        out_specs=pl.BlockSpec((tq_b, wb), lambda b, h, i: (b * nq + i, h)),
        out_shape=jax.ShapeDtypeStruct((r, B_HEADS * B_V_DIM), BF16),
        scratch_shapes=[pltpu.VMEM((tq_b, lk), F32)],
        compiler_params=_cparams(3),
    )(*args)
    return oa, ob


def _outproj_kernel(*refs, n_in):
    o_refs = refs[:n_in]
    w_refs = refs[n_in:2 * n_in]
    x_ref, g_ref, gate_ref, out_ref = refs[2 * n_in:]
    y = None
    for o_ref, w_ref in zip(o_refs, w_refs):
        t = jnp.dot(o_ref[...], w_ref[...], preferred_element_type=F32)
        y = t if y is None else y + t
    out_ref[...] = x_ref[...] + _gated_rms(y, g_ref[...], gate_ref[0])


def _outproj(os_, w, x, g, mod, row_fn, tm):
    r, d = x.shape
    n_in = len(os_)
    kw = os_[0].shape[1]
    assert all(o.shape[1] == kw for o in os_) and w.shape == (n_in * kw, d)
    in_specs = ([pl.BlockSpec((tm, kw), lambda i: (i, 0)) for _ in os_]
                + [pl.BlockSpec((kw, d), lambda i, k=k: (k, 0), pipeline_mode=pl.Buffered(1)) for k in range(n_in)]
                + [pl.BlockSpec((tm, d), lambda i: (i, 0)),
                   pl.BlockSpec((1, d), lambda i: (0, 0)),
                   _mod_spec(d, 2, row_fn)])
    return pl.pallas_call(
        functools.partial(_outproj_kernel, n_in=n_in),
        grid=(r // tm,),
        in_specs=in_specs,
        out_specs=pl.BlockSpec((tm, d), lambda i: (i, 0)),
        out_shape=jax.ShapeDtypeStruct((r, d), F32),
        compiler_params=_cparams(1),
    )(*os_, *([w] * n_in), x, g, mod)


def _mlp_kernel(x_ref, g2_ref, sh_ref, sc_ref, gate_ref, g3_ref, w1_ref, w2_ref, out_ref, n_scr):
    f = pl.program_id(1)

    def ffn_block(first):
        h = jnp.maximum(jnp.dot(n_scr[...], w1_ref[...], preferred_element_type=F32), 0.0)
        hb = (h * h).astype(BF16)
        for c0 in range(0, out_ref.shape[1], MLP_ACC_CHUNK):
            cols = slice(c0, c0 + MLP_ACC_CHUNK)
            t = jnp.dot(hb, w2_ref[:, cols], preferred_element_type=F32)
            if first:
                out_ref[:, cols] = t
            else:
                out_ref[:, cols] += t

    @pl.when(f == 0)
    def _():
        n_scr[...] = _modulate(x_ref[...], g2_ref[...], sc_ref[0], sh_ref[0]).astype(BF16)
        ffn_block(True)

    last = pl.num_programs(1) - 1

    @pl.when((f > 0) & (f < last))
    def _():
        ffn_block(False)

    @pl.when(f == last)
    def _():
        ffn_block(False)
        out_ref[...] = x_ref[...] + _gated_rms(out_ref[...], g3_ref[...], gate_ref[0])


def _mlp(x, g2, g3, mod, row_fn, w1, w2, layer, tm, tf):
    r, d = x.shape
    dff = w1.shape[2]
    assert dff // tf >= 2
    return pl.pallas_call(
        _mlp_kernel,
        grid=(r // tm, dff // tf),
        in_specs=[pl.BlockSpec((tm, d), lambda i, f: (i, 0)),
                  pl.BlockSpec((1, d), lambda i, f: (0, 0)),
                  _mod_spec(d, 3, row_fn), _mod_spec(d, 4, row_fn), _mod_spec(d, 5, row_fn),
                  pl.BlockSpec((1, d), lambda i, f: (0, 0)),
                  pl.BlockSpec((None, d, tf), lambda i, f: (layer, 0, f)),
                  pl.BlockSpec((None, tf, d), lambda i, f: (layer, f, 0))],
        out_specs=pl.BlockSpec((tm, d), lambda i, f: (i, 0)),
        out_shape=jax.ShapeDtypeStruct((r, d), F32),
        scratch_shapes=[pltpu.VMEM((tm, d), BF16)],
        compiler_params=_cparams(2),
    )(x, g2, mod, mod, mod, g3, w1, w2)


def _ssm_in_kernel(x_ref, g_ref, sh_ref, sc_ref, w_ref, u_ref):
    n = _modulate(x_ref[...], g_ref[...], sc_ref[0], sh_ref[0]).astype(BF16)
    for c0 in range(0, w_ref.shape[1], PROJ_CHUNK):
        u_ref[:, c0:c0 + PROJ_CHUNK] = jnp.dot(n, w_ref[:, c0:c0 + PROJ_CHUNK], preferred_element_type=F32)


def _ssm_in(x, g, mod, row_fn, w, tm):
    r, d = x.shape
    n = w.shape[1]
    return pl.pallas_call(
        _ssm_in_kernel,
        grid=(r // tm,),
        in_specs=[pl.BlockSpec((tm, d), lambda i: (i, 0)),
                  pl.BlockSpec((1, d), lambda i: (0, 0)),
                  _mod_spec(d, 0, row_fn), _mod_spec(d, 1, row_fn),
                  pl.BlockSpec((d, n), lambda i: (0, 0), pipeline_mode=pl.Buffered(1))],
        out_specs=pl.BlockSpec((tm, n), lambda i: (i, 0)),
        out_shape=jax.ShapeDtypeStruct((r, n), F32),
        compiler_params=_cparams(1),
    )(x, g, mod, mod, w)


def _ssm_params(a_re, a_im, log_dt, b_re, b_im, c_re, c_im, seg_chunks):
    g = a_re.shape[1]
    vec = lambda x: x.astype(F32).reshape(2, g, 1, -1)
    args = (vec(log_dt), vec(a_re), vec(a_im)) + tuple(x.astype(F32) for x in (b_re, b_im, c_re, c_im))
    pg = SSM_PARAM_GROUPS
    per_group = lambda a: pl.BlockSpec((2, pg) + a.shape[2:], lambda i: (0, i, 0, 0))
    blk = lambda a: pl.BlockSpec((pg,) + a.shape[1:], lambda i: (i, 0, 0))
    out_shape = [jax.ShapeDtypeStruct((g, SSM_KDIM, SSM_LANES), BF16),
                 jax.ShapeDtypeStruct((g, SSM_KDIM, SSM_KDIM), BF16),
                 jax.ShapeDtypeStruct((g, SSM_KDIM, SSM_LANES), BF16),
                 jax.ShapeDtypeStruct((g, 1, SSM_LANES), F32),
                 jax.ShapeDtypeStruct((g, 1, SSM_LANES), F32)]
    return pl.pallas_call(
        functools.partial(_ssm_params_kernel, seg_chunks=seg_chunks),
        grid=(g // pg,),
        in_specs=[per_group(a) for a in args],
        out_specs=[blk(o) for o in out_shape],
        out_shape=out_shape,
        compiler_params=_cparams(1),
    )(*args)


def _ssm_params_kernel(*refs, seg_chunks):
    for k in range(SSM_PARAM_GROUPS):
        _ssm_params_group(k, *refs, seg_chunks=seg_chunks)


def _ssm_params_group(k, ldt_ref, are_ref, aim_ref, br_ref, bi_ref, cr_ref, ci_ref,
                      win_ref, toep_ref, cout_ref, achunk_ref, aseg_ref, *, seg_chunks):
    t, nch, half = SSM_CHUNK, S5_GROUP_CH, 2 * S5_STATE
    hi = lax.Precision.HIGHEST
    both = lambda ref: jnp.concatenate([ref[0, k], ref[1, k]], axis=1)
    eye = (lax.broadcasted_iota(jnp.int32, (S5_STATE, S5_STATE), 0)
           == lax.broadcasted_iota(jnp.int32, (S5_STATE, S5_STATE), 1)).astype(F32)
    tr = lambda m: lax.dot_general(m, eye, (((0,), (0,)), ((), ())), precision=hi, preferred_element_type=F32)
    both_t = lambda ref: jnp.concatenate([tr(ref[0, k]), tr(ref[1, k])], axis=1)
    ar, ai = both(are_ref), both(aim_ref)
    dt = jnp.exp(jnp.concatenate([jnp.broadcast_to(ldt_ref[d, k], (1, S5_STATE)) for d in range(2)], axis=1))
    lr, li = ar * dt, ai * dt

    def power(k):
        mag = jnp.exp(lr * k)
        return mag * jnp.cos(li * k), mag * jnp.sin(li * k)

    pw = [(jnp.ones_like(lr), jnp.zeros_like(lr)), power(1.0)]
    for _ in range(t - 1):
        pw.append((pw[-1][0] * pw[1][0] - pw[-1][1] * pw[1][1], pw[-1][0] * pw[1][1] + pw[-1][1] * pw[1][0]))
    fwd1 = lax.broadcasted_iota(jnp.int32, (1, half), 1) < S5_STATE
    pick = lambda kf, kb: (jnp.where(fwd1, pw[kf][0], pw[kb][0]), jnp.where(fwd1, pw[kf][1], pw[kb][1]))
    xr, xi = pw[1][0] - 1.0, pw[1][1]
    den = ar * ar + ai * ai
    qr, qi = (xr * ar + xi * ai) / den, (xi * ar - xr * ai) / den
    btr, bti = both_t(br_ref), both_t(bi_ref)
    bbr = qr * btr - qi * bti
    bbi = qr * bti + qi * btr
    ccr, cci = both(cr_ref), both(ci_ref)

    for s in range(t):
        rows = slice(s * nch, (s + 1) * nch)
        er, ei = pick(t - 1 - s, s)
        win_ref[k, rows, 0:half] = (er * bbr - ei * bbi).astype(BF16)
        win_ref[k, rows, half:] = (er * bbi + ei * bbr).astype(BF16)
        er, ei = pick(s + 1, t - s)
        cout_ref[k, rows, 0:half] = (ccr * er - cci * ei).astype(BF16)
        cout_ref[k, rows, half:] = (-(ccr * ei + cci * er)).astype(BF16)

    fwd = lax.broadcasted_iota(jnp.int32, (nch, half), 1) < S5_STATE
    cp = [(ccr * pw[k][0] - cci * pw[k][1], ccr * pw[k][1] + cci * pw[k][0]) for k in range(t)]
    stack = lambda ks, j: jnp.concatenate([cp[k][j] for k in ks], axis=0)
    nt = lambda x, y: lax.dot_general(x, y, (((1,), (1,)), ((), ())), precision=hi, preferred_element_type=F32)
    zero = jnp.zeros_like(bbr)
    order, rev = list(range(t)), list(range(t - 1, -1, -1))
    taps_f = (nt(jnp.where(fwd, bbr, zero), stack(order, 0)) - nt(jnp.where(fwd, bbi, zero), stack(order, 1)))
    taps_b = (nt(jnp.where(fwd, zero, bbr), stack(rev, 0)) - nt(jnp.where(fwd, zero, bbi), stack(rev, 1)))
    lane = lax.broadcasted_iota(jnp.int32, (nch, SSM_KDIM), 1)
    for s in range(t):
        tf = taps_f if s == 0 else pltpu.roll(taps_f, s * nch, axis=1)
        tb = taps_b if s == t - 1 else pltpu.roll(taps_b, (s + 1) * nch, axis=1)
        blk = jnp.where(lane >= s * nch, tf, 0.0) + jnp.where(lane < (s + 1) * nch, tb, 0.0)
        toep_ref[k, s * nch:(s + 1) * nch, :] = blk.astype(BF16)

    for ref, (dr, di) in ((achunk_ref, pw[t]), (aseg_ref, power(float(t * seg_chunks)))):
        ref[k, :, 0:half] = dr
        ref[k, :, half:] = di


def _cmul_add(ar, ai, h_re, h_im, s_re, s_im):
    return ar * h_re - ai * h_im + s_re, ar * h_im + ai * h_re + s_im


def _regroup_perms():
    n = SSM_CHUNK * V7X_LANES
    idx = jnp.arange(n)
    tok, grp, ch = idx // V7X_LANES, idx % V7X_LANES // S5_GROUP_CH, idx % S5_GROUP_CH
    dst = grp * SSM_KDIM + tok * S5_GROUP_CH + ch
    p_in = (dst[:, None] == idx[None, :]).astype(BF16)
    return p_in, p_in.T


def _regroup_in(u_ref, pin_ref, lhs_scr, nrows):
    t, rbk = SSM_CHUNK, min(SSM_REGROUP_ROWS, nrows)

    def body(rb, carry):
        pieces = [u_ref[pl.ds(rb * (rbk * t) + tt, rbk, stride=t), :].astype(BF16) for tt in range(t)]
        out = jnp.dot(jnp.concatenate(pieces, axis=1), pin_ref[...], preferred_element_type=F32)
        rows = pl.ds(pl.multiple_of(rb * rbk, rbk), rbk)
        for gl in range(SSM_TILE_GROUPS):
            lhs_scr[gl, rows, :] = out[:, gl * SSM_KDIM:(gl + 1) * SSM_KDIM].astype(BF16)
        return carry

    lax.fori_loop(0, nrows // rbk, body, 0)


def _regroup_out(y_scr, pout_ref, y_ref, nrows):
    t, rbk = SSM_CHUNK, min(SSM_REGROUP_ROWS, nrows)

    def body(rb, carry):
        rows = pl.ds(pl.multiple_of(rb * rbk, rbk), rbk)
        rest = jnp.concatenate([y_scr[gl, rows, :] for gl in range(SSM_TILE_GROUPS)], axis=1)
        out = None
        for _ in range(SSM_Y_TERMS):
            term = rest.astype(BF16)
            rest = rest - term.astype(F32)
            part = jnp.dot(term, pout_ref[...], preferred_element_type=F32)
            out = part if out is None else out + part
        for tt in range(t):
            y_ref[pl.ds(rb * (rbk * t) + tt, rbk, stride=t), :] = out[:, tt * V7X_LANES:(tt + 1) * V7X_LANES]
        return carry

    lax.fori_loop(0, nrows // rbk, body, 0)


def _ssm_core_kernel(u_ref, pin_ref, pout_ref, win_ref, toep_ref, cout_ref, a_ref, aseg_ref, h0_ref, y_ref, hfin_ref,
                     lhs_scr, y_scr, s_re, s_im, tf_re, tb_re, tf_im, tb_im, *, nseq, nc, nsub):
    step = pl.program_id(1)
    nrows = nseq * nc

    @pl.when(step == 0)
    def _():
        _regroup_in(u_ref, pin_ref, lhs_scr, nrows)

    for k in range(SSM_GROUPS_PER_STEP):
        _ssm_group(k, step * SSM_GROUPS_PER_STEP + k, win_ref, toep_ref, cout_ref, a_ref, aseg_ref, h0_ref, hfin_ref,
                   lhs_scr, y_scr, s_re, s_im, tf_re, tb_re, tf_im, tb_im, nseq, nc, nsub)

    @pl.when(step == pl.num_programs(1) - 1)
    def _():
        _regroup_out(y_scr, pout_ref, y_ref, nrows)


def _ssm_group(k, g, win_ref, toep_ref, cout_ref, a_ref, aseg_ref, h0_ref, hfin_ref,
               lhs_scr, y_scr, s_re, s_im, tf_re, tb_re, tf_im, tb_im, nseq, nc, nsub):
    half = SSM_LANES // 2
    pitch = nc + SSM_PITCH_PAD
    nrows = nseq * nc
    nslab = nseq // V7X_SUBLANES
    base = k * nseq * pitch

    lhs = lhs_scr[g]
    s = jnp.dot(lhs, win_ref[k], preferred_element_type=F32)
    for p in range(nseq):
        s_re[base + p * pitch:base + p * pitch + nc, :] = s[p * nc:(p + 1) * nc, 0:half]
        s_im[base + p * pitch:base + p * pitch + nc, :] = s[p * nc:(p + 1) * nc, half:]

    ar, ai = a_ref[k][:, 0:half], a_ref[k][:, half:]
    fwd = lax.broadcasted_iota(jnp.int32, (V7X_SUBLANES, half), 1) < S5_STATE

    def slab(q, c):
        return pl.ds(base + q * V7X_SUBLANES * pitch + c, V7X_SUBLANES, stride=pitch)

    def scan(init, store):
        hs = list(init)
        for c in range(nc):
            cb = nc - 1 - c
            for q in range(nslab):
                h_re, h_im = hs[q]
                x_re = jnp.where(fwd, s_re[slab(q, c), :], s_re[slab(q, cb), :])
                x_im = jnp.where(fwd, s_im[slab(q, c), :], s_im[slab(q, cb), :])
                if store:
                    tf_re[slab(q, c), :] = h_re
                    tb_re[slab(q, cb), :] = h_re
                    tf_im[slab(q, c), :] = h_im
                    tb_im[slab(q, cb), :] = h_im
                hs[q] = _cmul_add(ar, ai, h_re, h_im, x_re, x_im)
        return hs

    rows8 = lambda q: slice(q * V7X_SUBLANES, (q + 1) * V7X_SUBLANES)
    init = [(h0_ref[k, rows8(q), 0:half], h0_ref[k, rows8(q), half:]) for q in range(nslab)]
    if nsub > 1:
        zero = jnp.zeros((V7X_SUBLANES, half), F32)
        ends = scan([(zero, zero)] * nslab, False)
        gr, gi = aseg_ref[k][:, 0:half], aseg_ref[k][:, half:]
        seg = lax.broadcasted_iota(jnp.int32, (V7X_SUBLANES, half), 0) % nsub
        carried = jnp.where(fwd, seg, nsub - 1 - seg) != 0
        shift = lambda x: jnp.where(fwd, pltpu.roll(x, 1, axis=0), pltpu.roll(x, V7X_SUBLANES - 1, axis=0))
        for q in range(nslab):
            (h0_re, h0_im), (f_re, f_im) = init[q], ends[q]
            e_re, e_im = h0_re, h0_im
            for _ in range(nsub - 1):
                x_re, x_im = _cmul_add(gr, gi, e_re, e_im, f_re, f_im)
                e_re = h0_re + jnp.where(carried, shift(x_re), 0.0)
                e_im = h0_im + jnp.where(carried, shift(x_im), 0.0)
            init[q] = (e_re, e_im)
    hs = scan(init, True)
    for q in range(nslab):
        hfin_ref[k, rows8(q), 0:half] = hs[q][0]
        hfin_ref[k, rows8(q), half:] = hs[q][1]

    gather = lambda ref: jnp.concatenate([ref[base + p * pitch:base + p * pitch + nc, :] for p in range(nseq)],
                                         axis=0)
    fwd_rows = lax.broadcasted_iota(jnp.int32, (nrows, half), 1) < S5_STATE
    hp = jnp.concatenate([jnp.where(fwd_rows, gather(tf_re), gather(tb_re)),
                          jnp.where(fwd_rows, gather(tf_im), gather(tb_im))], axis=1).astype(BF16)
    y_scr[g] = (jnp.dot(lhs, toep_ref[k], preferred_element_type=F32)
                + lax.dot_general(hp, cout_ref[k], (((1,), (1,)), ((), ())), preferred_element_type=F32))


def _ssm_core(u, ops, h0, nseq, nc, nsub):
    r, d = u.shape
    nrows = nseq * nc
    assert r == nrows * SSM_CHUNK and nseq % V7X_SUBLANES == 0 and V7X_SUBLANES % nsub == 0
    assert nrows % min(SSM_REGROUP_ROWS, nrows) == 0 and nc % V7X_SUBLANES == 0
    ntile = d // V7X_LANES
    gps = SSM_GROUPS_PER_STEP
    steps = SSM_TILE_GROUPS // gps
    grp = lambda j, g: (j * steps + g, 0, 0)
    scan_rows = gps * nseq * (nc + SSM_PITCH_PAD)
    p_in, p_out = _regroup_perms()
    perm_spec = pl.BlockSpec(p_in.shape, lambda j, g: (0, 0), pipeline_mode=pl.Buffered(1))
    return pl.pallas_call(
        functools.partial(_ssm_core_kernel, nseq=nseq, nc=nc, nsub=nsub),
        grid=(ntile, steps),
        in_specs=[pl.BlockSpec((r, V7X_LANES), lambda j, g: (0, j)),
                  perm_spec, perm_spec]
                 + [pl.BlockSpec((gps,) + o.shape[1:], grp) for o in ops]
                 + [pl.BlockSpec((gps, nseq, SSM_LANES), grp)],
        out_specs=[pl.BlockSpec((r, V7X_LANES), lambda j, g: (0, j)),
                   pl.BlockSpec((gps, nseq, SSM_LANES), grp)],
        out_shape=[jax.ShapeDtypeStruct((r, d), F32),
                   jax.ShapeDtypeStruct((d // S5_GROUP_CH, nseq, SSM_LANES), F32)],
        scratch_shapes=[pltpu.VMEM((SSM_TILE_GROUPS, nrows, SSM_KDIM), BF16),
                        pltpu.VMEM((SSM_TILE_GROUPS, nrows, SSM_KDIM), F32)]
                       + [pltpu.VMEM((scan_rows, V7X_LANES), F32)] * 6,
        compiler_params=_cparams(2),
    )(u, p_in, p_out, *ops, h0)


def _glu_kernel(y_ref, u_ref, d_ref, w_ref, z_ref):
    v = y_ref[...] + d_ref[...] * u_ref[...]
    z = 0.5 * v * (1.0 + jnp.tanh(math.sqrt(2.0 / math.pi) * (v + 0.044715 * (v * v * v))))
    zb = z.astype(BF16)
    for c0 in range(0, w_ref.shape[1], PROJ_CHUNK):
        cols = slice(c0, c0 + PROJ_CHUNK)
        gl = jnp.dot(zb, w_ref[:, cols], preferred_element_type=F32)
        z_ref[:, cols] = (z[:, cols] * _sigmoid(gl)).astype(BF16)


def _glu(y, u, d, w, tm):
    r, dm = y.shape
    return pl.pallas_call(
        _glu_kernel,
        grid=(r // tm,),
        in_specs=[pl.BlockSpec((tm, dm), lambda i: (i, 0)),
                  pl.BlockSpec((tm, dm), lambda i: (i, 0)),
                  pl.BlockSpec((1, dm), lambda i: (0, 0)),
                  pl.BlockSpec(w.shape, lambda i: (0, 0), pipeline_mode=pl.Buffered(1))],
        out_specs=pl.BlockSpec((tm, dm), lambda i: (i, 0)),
        out_shape=jax.ShapeDtypeStruct((r, dm), BF16),
        compiler_params=_cparams(1),
    )(y, u, d, w)


def _rope_tables(length):
    rows = length // GRID_W
    row = jnp.repeat(jnp.arange(rows, dtype=F32), GRID_W)
    col = jnp.tile(jnp.arange(GRID_W, dtype=F32), rows)
    inv = ROPE_BASE ** (-jnp.arange(ROPE_PAIRS_PER_AXIS, dtype=F32) / ROPE_PAIRS_PER_AXIS)
    ang = jnp.concatenate([row[:, None] * inv, col[:, None] * inv], axis=-1)
    cos, sin = jnp.cos(ang), jnp.sin(ang)
    return jnp.concatenate([cos, cos], axis=-1), jnp.concatenate([-sin, sin], axis=-1)


def kernel(x_prompt, x_sample, c, cache_a_k, cache_a_v, cache_b_k, cache_b_v, state_ssm, c_ctx, ada_w, ada_b, norm_g, mlp_w1, mlp_w2, attn_w_in, attn_w_out, attn_qk_norm, diff_lambda, diff_subln, ssm_w_in, ssm_a_re, ssm_a_im, ssm_log_dt, ssm_b_re, ssm_b_im, ssm_c_re, ssm_c_im, ssm_d, ssm_glu_w, ssm_w_out):
    bp, lp, d = x_prompt.shape
    bs, ls, _ = x_sample.shape
    depth = ada_w.shape[0]
    past = cache_a_k.shape[2]
    nsub = SSM_SEGMENTS
    assert bs + 1 <= 8 and bp % V7X_SUBLANES == 0
    assert lp % SSM_CHUNK == 0 and ls % (SSM_CHUNK * nsub) == 0 and ls % GRID_W == 0

    cond8 = jnp.zeros((8, d), F32).at[0].set(c_ctx).at[1:1 + bs].set(c)
    mods = _modulation(cond8, ada_w, ada_b).reshape(depth, 8 * N_MOD, 1, d)

    xp = x_prompt.reshape(bp * lp, d)
    xs = x_sample.reshape(bs * ls, d)
    tm = min(512, bp * lp, ls)
    tm_big = min(1024, bp * lp, ls)
    row_p = lambda i: 0
    row_s = lambda i: 1 + i // (ls // tm)
    row_s_big = lambda i: 1 + i // (ls // tm_big)
    w1_all = mlp_w1.astype(BF16)
    w2_all = mlp_w2.astype(BF16)
    new_ak = new_av = new_bk = new_bv = new_ssm = None
    for l in range(depth):
        mod = mods[l]
        g = norm_g[l][:, None, :]
        i = l // 2
        if l % 2 == 0:
            lam_init = 0.8 - 0.6 * math.exp(-0.3 * l)
            w_in = attn_w_in[i].astype(BF16)
            w_out = attn_w_out[i].astype(BF16)
            qkv, new_ak, new_av, new_bk, new_bv = _qkv_proj(xp, g[0], mod, row_p, w_in, attn_qk_norm[i], None, lp, tm)
            oa, ob = _attention(qkv, bp, None, diff_lambda[i], diff_subln[i][None, :], lam_init,
                                tq_a=lp, tq_b=lp, tk=512, hp=2)
            xp = _outproj([oa, ob], w_out, xp, g[1], mod, row_p, tm)
            (qkv,) = _qkv_proj(xs, g[0], mod, row_s, w_in, attn_qk_norm[i], _rope_tables(ls), 0, tm)
            caches = (cache_a_k[:, i].reshape(bs * past, A_KV_HEADS * HEAD_DIM),
                      cache_a_v[:, i].reshape(bs * past, A_KV_HEADS * HEAD_DIM),
                      cache_b_k[:, i].reshape(bs * past, B_HEADS * 2 * HEAD_DIM),
                      cache_b_v[:, i].reshape(bs * past, B_HEADS * B_V_DIM))
            oa, ob = _attention(qkv, bs, caches, diff_lambda[i], diff_subln[i][None, :], lam_init,
                                tq_a=min(512, ls), tq_b=min(1024, ls), tk=512, hp=1)
            xs = _outproj([oa, ob], w_out, xs, g[1], mod, row_s, tm)
        else:
            w_in = ssm_w_in[i].astype(BF16)
            u_p = _ssm_in(xp, g[0], mod, row_p, w_in, tm_big)
            u_s = _ssm_in(xs, g[0], mod, row_s_big, w_in, tm_big)
            ng = d // S5_GROUP_CH
            nc_p, nc_s = lp // SSM_CHUNK, ls // SSM_CHUNK // nsub
            ops = _ssm_params(ssm_a_re[i], ssm_a_im[i], ssm_log_dt[i], ssm_b_re[i], ssm_b_im[i],
                                                 ssm_c_re[i], ssm_c_im[i], nc_s)
            st = state_ssm[:, i].astype(F32)
            h0 = jnp.zeros((ng, bs, nsub, 4, S5_STATE), F32)
            h0 = h0.at[:, :, 0, 0].set(jnp.transpose(st[:, 0, :, :, 0], (1, 0, 2)))
            h0 = h0.at[:, :, 0, 2].set(jnp.transpose(st[:, 0, :, :, 1], (1, 0, 2)))
            h0 = h0.at[:, :, nsub - 1, 1].set(jnp.transpose(st[:, 1, :, :, 0], (1, 0, 2)))
            h0 = h0.at[:, :, nsub - 1, 3].set(jnp.transpose(st[:, 1, :, :, 1], (1, 0, 2)))
            h0 = h0.reshape(ng, bs * nsub, SSM_LANES)
            y_p, hfin = _ssm_core(u_p, ops, jnp.zeros((ng, bp, SSM_LANES), F32), bp, nc_p, 1)
            y_s, _ = _ssm_core(u_s, ops, h0, bs * nsub, nc_s, nsub)
            new_ssm = jnp.transpose(hfin.reshape(ng, bp, 2, 2, S5_STATE), (1, 3, 0, 4, 2))
            glu_w = ssm_glu_w[i].astype(BF16)
            w_out = ssm_w_out[i].astype(BF16)
            dvec = ssm_d[i][None, :]
            z_p = _glu(y_p, u_p, dvec, glu_w, tm)
            z_s = _glu(y_s, u_s, dvec, glu_w, tm)
            xp = _outproj([z_p], w_out, xp, g[1], mod, row_p, tm)
            xs = _outproj([z_s], w_out, xs, g[1], mod, row_s, tm)
        xp = _mlp(xp, g[2], g[3], mod, row_p, w1_all, w2_all, l, tm_big, 512)
        xs = _mlp(xs, g[2], g[3], mod, row_s_big, w1_all, w2_all, l, tm_big, 512)
    return (xp.reshape(bp, lp, d), xs.reshape(bs, ls, d), new_ak, new_av, new_bk, new_bv,
            new_ssm[:, None])
```
